```python
import math
import jax
import jax.numpy as jnp
from jax import lax
import numpy as np

D_MODEL = 1024
BATCH = 8
SEQ = 2048
DEPTH = 4

N_MIXERS = 4
NORM_EPS = 1e-6
MLA_HEADS = 16
MLA_Q_RANK = 384
MLA_KV_RANK = 256
MLA_NOPE = 64
MLA_ROPE = 32
MLA_V = 64
MLA_QK = MLA_NOPE + MLA_ROPE
MLA_WIDTH = MLA_HEADS * MLA_V
MLA_IN = MLA_Q_RANK + MLA_KV_RANK + MLA_ROPE + MLA_WIDTH
ROPE_THETA = 10000.0
Q_BLOCK = 128
GLA_HEADS = 4
GLA_DK = D_MODEL // (2 * GLA_HEADS)
GLA_DV = D_MODEL // GLA_HEADS
GLA_KEY = GLA_HEADS * GLA_DK
GLA_VAL = GLA_HEADS * GLA_DV
GLA_GATE_RANK = 16
GLA_TAU = 16.0
GLA_CHUNK = 64
GLA_IN = 2 * GLA_KEY + 2 * GLA_VAL + GLA_GATE_RANK
LRU_WIDTH = 5 * D_MODEL // 4
LRU_BLOCKS = 10
LRU_BLOCK = LRU_WIDTH // LRU_BLOCKS
LRU_C = 8.0
CONV_W = 4
SSD_INNER = 2 * D_MODEL
SSD_HEADDIM = 64
SSD_HEADS = SSD_INNER // SSD_HEADDIM
SSD_GROUPS = 8
SSD_HPG = SSD_HEADS // SSD_GROUPS
SSD_STATE = 128
SSD_CHUNK = 64
SSD_CONV_DIM = SSD_INNER + 2 * SSD_GROUPS * SSD_STATE
SSD_IN = SSD_INNER + SSD_CONV_DIM + SSD_HEADS

kernel_name = 'hybrid_mla_gla_rglru_ssd_trunk'


def _layers_of(m):
    return len(range(m, DEPTH, N_MIXERS))


def rmsnorm(x, g):
    xf = x.astype(jnp.float32)
    y = xf * lax.rsqrt(jnp.mean(xf * xf, axis=-1, keepdims=True) + NORM_EPS)
    return (y * g.astype(jnp.float32)).astype(x.dtype)


def rope_tables(positions):
    inv_freq = ROPE_THETA ** (-jnp.arange(0, MLA_ROPE, 2, dtype=jnp.float32) / MLA_ROPE)
    ang = positions.astype(jnp.float32)[..., None] * inv_freq
    return jnp.cos(ang), jnp.sin(ang)


def apply_rope(t, cos, sin):
    t1, t2 = jnp.split(t.astype(jnp.float32), 2, axis=-1)
    return jnp.concatenate([t1 * cos - t2 * sin, t2 * cos + t1 * sin], axis=-1).astype(t.dtype)


def causal_depthwise_conv(u, w, b):
    y = lax.conv_general_dilated(u, w[:, None, :].astype(u.dtype), window_strides=(1,),
                                 padding=[(CONV_W - 1, 0)], dimension_numbers=('NWC', 'WIO', 'NWC'),
                                 feature_group_count=u.shape[-1])
    return y + b.astype(u.dtype)


def mla_mixer(h, cos, sin, w_in, g_q, w_uq, g_kv, w_ukv, w_out):
    B, S, _ = h.shape
    c_q, c_kv, k_r, gate = jnp.split(h @ w_in, [MLA_Q_RANK, MLA_Q_RANK + MLA_KV_RANK,
                                                MLA_Q_RANK + MLA_KV_RANK + MLA_ROPE], axis=-1)
    q = (rmsnorm(c_q, g_q) @ w_uq).reshape(B, S, MLA_HEADS, MLA_QK)
    q_nope = q[..., :MLA_NOPE]
    q_rope = apply_rope(q[..., MLA_NOPE:], cos[:, :, None], sin[:, :, None])
    kv = (rmsnorm(c_kv, g_kv) @ w_ukv).reshape(B, S, MLA_HEADS, MLA_NOPE + MLA_V)
    k_nope, v = kv[..., :MLA_NOPE], kv[..., MLA_NOPE:]
    k_rope = apply_rope(k_r, cos, sin)
    scale = MLA_QK ** -0.5
    outs = []
    for start in range(0, S, Q_BLOCK):
        end = start + Q_BLOCK
        s = (jnp.einsum('bqhd,bkhd->bhqk', q_nope[:, start:end], k_nope[:, :end])
             + jnp.einsum('bqhr,bkr->bhqk', q_rope[:, start:end], k_rope[:, :end])).astype(jnp.float32) * scale
        mask = jnp.arange(start, end)[:, None] >= jnp.arange(end)[None, :]
        p = jax.nn.softmax(jnp.where(mask, s, -jnp.inf), axis=-1).astype(v.dtype)
        outs.append(jnp.einsum('bhqk,bkhd->bqhd', p, v[:, :end]))
    o = jnp.concatenate(outs, axis=1).reshape(B, S, MLA_WIDTH)
    return (o * jax.nn.silu(gate)) @ w_out


def gla_mixer(h, w_in, w_gk2, b_gk, g_o, w_out):
    B, S, _ = h.shape
    N, C = S // GLA_CHUNK, GLA_CHUNK
    f32 = jnp.float32
    q, k, v, gate, gk = jnp.split(h @ w_in, [GLA_KEY, 2 * GLA_KEY, 2 * GLA_KEY + GLA_VAL,
                                             2 * GLA_KEY + 2 * GLA_VAL], axis=-1)
    log_a = jax.nn.log_sigmoid((gk @ w_gk2 + b_gk).astype(f32)) / GLA_TAU

    def chunks(t, d):
        return t.reshape(B, N, C, GLA_HEADS, d).transpose(0, 3, 1, 2, 4).astype(f32)

    q = chunks(q, GLA_DK) * GLA_DK ** -0.5
    k = chunks(k, GLA_DK)
    v = chunks(v, GLA_DV)
    b = jnp.cumsum(chunks(log_a, GLA_DK), axis=3)
    q_t = q * jnp.exp(b)
    k_t = k * jnp.exp(-b)
    causal = jnp.tril(jnp.ones((C, C), dtype=bool))
    att = jnp.where(causal, jnp.einsum('bhnik,bhnjk->bhnij', q_t, k_t), 0.0)
    o_intra = jnp.einsum('bhnij,bhnjv->bhniv', att, v)
    b_last = b[:, :, :, -1]
    d_state = jnp.einsum('bhnck,bhncv->bhnkv', k * jnp.exp(b_last[:, :, :, None] - b), v)

    def step(s_prev, inp):
        decay, ds = inp
        return decay[..., None] * s_prev + ds, s_prev

    s0 = jnp.zeros((B, GLA_HEADS, GLA_DK, GLA_DV), f32)
    _, s_prev = lax.scan(step, s0, (jnp.moveaxis(jnp.exp(b_last), 2, 0), jnp.moveaxis(d_state, 2, 0)))
    s_prev = jnp.moveaxis(s_prev, 0, 2)
    o = o_intra + jnp.einsum('bhnck,bhnkv->bhncv', q_t, s_prev)
    o = o.transpose(0, 2, 3, 1, 4).reshape(B, S, GLA_HEADS, GLA_DV)
    o = rmsnorm(o, g_o).reshape(B, S, GLA_VAL).astype(h.dtype)
    return (o * jax.nn.silu(gate)) @ w_out


def rglru_mixer(h, w_in, conv_w, conv_b, w_a, b_a, w_x, b_x, lam, w_out):
    B, S, _ = h.shape
    f32 = jnp.float32
    gate, u = jnp.split(h @ w_in, 2, axis=-1)
    u = causal_depthwise_conv(u, conv_w, conv_b)
    ub = u.reshape(B, S, LRU_BLOCKS, LRU_BLOCK)
    r = jax.nn.sigmoid(jnp.einsum('bsni,nij->bsnj', ub, w_a).reshape(B, S, LRU_WIDTH) + b_a).astype(f32)
    i = jax.nn.sigmoid(jnp.einsum('bsni,nij->bsnj', ub, w_x).reshape(B, S, LRU_WIDTH) + b_x).astype(f32)
    log_a = -LRU_C * r * jax.nn.softplus(-lam.astype(f32))
    a = jnp.exp(log_a)
    b = jnp.sqrt(-jnp.expm1(2.0 * log_a)) * (i * u.astype(f32))

    def combine(left, right):
        a1, b1 = left
        a2, b2 = right
        return a1 * a2, a2 * b1 + b2

    _, hs = lax.associative_scan(combine, (a, b), axis=1)
    return (hs.astype(h.dtype) * jax.nn.silu(gate)) @ w_out


def ssd_mixer(h, w_in, conv_w, conv_b, dt_bias, a_log, d_skip, g_norm, w_out):
    B, S, _ = h.shape
    N, L, G, HG, P, NS = S // SSD_CHUNK, SSD_CHUNK, SSD_GROUPS, SSD_HPG, SSD_HEADDIM, SSD_STATE
    f32 = jnp.float32
    z, xbc, dt = jnp.split(h @ w_in, [SSD_INNER, SSD_INNER + SSD_CONV_DIM], axis=-1)
    xbc = jax.nn.silu(causal_depthwise_conv(xbc, conv_w, conv_b))
    x, bm, cm = jnp.split(xbc, [SSD_INNER, SSD_INNER + G * NS], axis=-1)
    dt = jax.nn.softplus(dt.astype(f32) + dt_bias.astype(f32))
    A = -jnp.exp(a_log.astype(f32))
    x = x.astype(f32).reshape(B, N, L, G, HG, P)
    bm = bm.astype(f32).reshape(B, N, L, G, NS)
    cm = cm.astype(f32).reshape(B, N, L, G, NS)
    dt_c = dt.reshape(B, N, L, G, HG)
    cs = jnp.cumsum(jnp.moveaxis(dt_c * A.reshape(G, HG), 2, -1), axis=-1)
    xdt = x * dt_c[..., None]
    causal = jnp.tril(jnp.ones((L, L), dtype=bool))
    seg = cs[..., :, None] - cs[..., None, :]
    lmat = jnp.exp(jnp.where(causal, seg, -jnp.inf))
    cb = jnp.einsum('bnigs,bnjgs->bngij', cm, bm)
    y_diag = jnp.einsum('bngij,bnghij,bnjghp->bnighp', cb, lmat, xdt)
    decay = jnp.exp(cs[..., -1:] - cs)
    states = jnp.einsum('bnjgs,bnghj,bnjghp->bnghps', bm, decay, xdt)

    def step(s_prev, inp):
        dec, st = inp
        return dec[..., None, None] * s_prev + st, s_prev

    s0 = jnp.zeros((B, G, HG, P, NS), f32)
    _, s_prev = lax.scan(step, s0, (jnp.moveaxis(jnp.exp(cs[..., -1]), 1, 0), jnp.moveaxis(states, 1, 0)))
    s_prev = jnp.moveaxis(s_prev, 0, 1)
    y_off = jnp.einsum('bnigs,bnghps,bnghi->bnighp', cm, s_prev, jnp.exp(cs))
    y = (y_diag + y_off).reshape(B, S, SSD_HEADS, P) + d_skip.astype(f32)[:, None] * x.reshape(B, S, SSD_HEADS, P)
    y = y.reshape(B, S, SSD_INNER) * jax.nn.silu(z.astype(f32))
    y = rmsnorm(y.reshape(B, S, G, SSD_INNER // G), g_norm.reshape(G, SSD_INNER // G))
    return y.reshape(B, S, SSD_INNER).astype(h.dtype) @ w_out


def setup_inputs(seed: int = 0) -> dict:
    key = jax.random.key(seed)
    ks = iter(jax.random.split(key, 48))
    f32 = jnp.float32

    def dense(shape, fan_in):
        return jax.random.normal(next(ks), shape, f32) * fan_in ** -0.5

    def gain(shape):
        return 1.0 + 0.02 * jax.random.normal(next(ks), shape, f32)

    def small(shape, scale=0.02):
        return scale * jax.random.normal(next(ks), shape, f32)

    nA, nB, nC, nD = (_layers_of(m) for m in range(N_MIXERS))
    x = jax.random.normal(next(ks), (BATCH, SEQ, D_MODEL), f32)
    positions = (jnp.arange(SEQ, dtype=jnp.int32)[None, :]
                 + jax.random.randint(next(ks), (BATCH, 1), 0, 4096, dtype=jnp.int32))
    norm_g = gain((DEPTH, D_MODEL))
    final_g = gain((D_MODEL,))
    mla_w_in = dense((nA, D_MODEL, MLA_IN), D_MODEL)
    mla_g_q = gain((nA, MLA_Q_RANK))
    mla_w_uq = dense((nA, MLA_Q_RANK, MLA_HEADS * MLA_QK), MLA_Q_RANK)
    mla_g_kv = gain((nA, MLA_KV_RANK))
    mla_w_ukv = dense((nA, MLA_KV_RANK, MLA_HEADS * (MLA_NOPE + MLA_V)), MLA_KV_RANK)
    mla_w_out = dense((nA, MLA_WIDTH, D_MODEL), MLA_WIDTH)
    gla_w_in = dense((nB, D_MODEL, GLA_IN), D_MODEL)
    gla_w_gk2 = dense((nB, GLA_GATE_RANK, GLA_KEY), GLA_GATE_RANK)
    gla_b_gk = small((nB, GLA_KEY), 0.1)
    gla_g_o = gain((nB, GLA_DV))
    gla_w_out = dense((nB, GLA_VAL, D_MODEL), GLA_VAL)
    lru_w_in = dense((nC, D_MODEL, 2 * LRU_WIDTH), D_MODEL)
    lru_conv_w = dense((nC, CONV_W, LRU_WIDTH), CONV_W)
    lru_conv_b = small((nC, LRU_WIDTH))
    lru_w_a = dense((nC, LRU_BLOCKS, LRU_BLOCK, LRU_BLOCK), LRU_BLOCK)
    lru_b_a = small((nC, LRU_WIDTH))
    lru_w_x = dense((nC, LRU_BLOCKS, LRU_BLOCK, LRU_BLOCK), LRU_BLOCK)
    lru_b_x = small((nC, LRU_WIDTH))
    a0 = jax.random.uniform(next(ks), (nC, LRU_WIDTH), f32, 0.9, 0.999) ** (1.0 / LRU_C)
    lru_lam = jnp.log(a0) - jnp.log1p(-a0)
    lru_w_out = dense((nC, LRU_WIDTH, D_MODEL), LRU_WIDTH)
    ssd_w_in = dense((nD, D_MODEL, SSD_IN), D_MODEL)
    ssd_conv_w = dense((nD, CONV_W, SSD_CONV_DIM), CONV_W)
    ssd_conv_b = small((nD, SSD_CONV_DIM))
    dt0 = jnp.exp(jax.random.uniform(next(ks), (nD, SSD_HEADS), f32, math.log(1e-3), math.log(1e-1)))
    ssd_dt_bias = dt0 + jnp.log(-jnp.expm1(-dt0))
    ssd_a_log = jnp.log(jax.random.uniform(next(ks), (nD, SSD_HEADS), f32, 1.0, 16.0))
    ssd_d = gain((nD, SSD_HEADS))
    ssd_g_norm = gain((nD, SSD_INNER))
    ssd_w_out = dense((nD, SSD_INNER, D_MODEL), SSD_INNER)
    return {'x': x, 'positions': positions, 'norm_g': norm_g, 'final_g': final_g,
            'mla_w_in': mla_w_in, 'mla_g_q': mla_g_q, 'mla_w_uq': mla_w_uq, 'mla_g_kv': mla_g_kv,
            'mla_w_ukv': mla_w_ukv, 'mla_w_out': mla_w_out,
            'gla_w_in': gla_w_in, 'gla_w_gk2': gla_w_gk2, 'gla_b_gk': gla_b_gk, 'gla_g_o': gla_g_o,
            'gla_w_out': gla_w_out,
            'lru_w_in': lru_w_in, 'lru_conv_w': lru_conv_w, 'lru_conv_b': lru_conv_b, 'lru_w_a': lru_w_a,
            'lru_b_a': lru_b_a, 'lru_w_x': lru_w_x, 'lru_b_x': lru_b_x, 'lru_lam': lru_lam,
            'lru_w_out': lru_w_out,
            'ssd_w_in': ssd_w_in, 'ssd_conv_w': ssd_conv_w, 'ssd_conv_b': ssd_conv_b,
            'ssd_dt_bias': ssd_dt_bias, 'ssd_a_log': ssd_a_log, 'ssd_d': ssd_d, 'ssd_g_norm': ssd_g_norm,
            'ssd_w_out': ssd_w_out}


def reference(x, positions, norm_g, final_g,
              mla_w_in, mla_g_q, mla_w_uq, mla_g_kv, mla_w_ukv, mla_w_out,
              gla_w_in, gla_w_gk2, gla_b_gk, gla_g_o, gla_w_out,
              lru_w_in, lru_conv_w, lru_conv_b, lru_w_a, lru_b_a, lru_w_x, lru_b_x, lru_lam, lru_w_out,
              ssd_w_in, ssd_conv_w, ssd_conv_b, ssd_dt_bias, ssd_a_log, ssd_d, ssd_g_norm, ssd_w_out):
    cos, sin = rope_tables(positions)
    h = x
    for i in range(DEPTH):
        m, j = i % N_MIXERS, i // N_MIXERS
        u = rmsnorm(h, norm_g[i])
        if m == 0:
            y = mla_mixer(u, cos, sin, mla_w_in[j], mla_g_q[j], mla_w_uq[j], mla_g_kv[j], mla_w_ukv[j], mla_w_out[j])
        elif m == 1:
            y = gla_mixer(u, gla_w_in[j], gla_w_gk2[j], gla_b_gk[j], gla_g_o[j], gla_w_out[j])
        elif m == 2:
            y = rglru_mixer(u, lru_w_in[j], lru_conv_w[j], lru_conv_b[j], lru_w_a[j], lru_b_a[j],
                            lru_w_x[j], lru_b_x[j], lru_lam[j], lru_w_out[j])
        else:
            y = ssd_mixer(u, ssd_w_in[j], ssd_conv_w[j], ssd_conv_b[j], ssd_dt_bias[j], ssd_a_log[j],
                          ssd_d[j], ssd_g_norm[j], ssd_w_out[j])
        h = h + y
    return rmsnorm(h, final_g)
```

```python
import functools

import jax
import jax.numpy as jnp
from jax import lax
from jax.experimental import pallas as pl
from jax.experimental.pallas import tpu as pltpu

F32 = jnp.float32
BF16 = jnp.bfloat16

NORM_EPS = 1e-6
N_MIXERS = 4
MLA_HEADS = 16
MLA_Q_RANK = 384
MLA_KV_RANK = 256
MLA_NOPE = 64
MLA_ROPE = 32
MLA_V = 64
ROPE_THETA = 10000.0
GLA_HEADS = 4
GLA_GATE_RANK = 16
GLA_TAU = 16.0
GLA_CHUNK = 64
LRU_BLOCKS = 10
LRU_C = 8.0
CONV_W = 4
SSD_HEADDIM = 64
SSD_GROUPS = 8
SSD_STATE = 128
SSD_CHUNK = 64

LANES = 128
SUBLANES = 8
VMEM_LIMIT = 56 * 1024 * 1024


def _dot(a, b):
    return jnp.dot(a, b, preferred_element_type=F32)


def _dot_nt(a, b):
    return lax.dot_general(a, b, (((1,), (1,)), ((), ())), preferred_element_type=F32)


def _dot_tn(a, b):
    return lax.dot_general(a, b, (((0,), (0,)), ((), ())), preferred_element_type=F32)


def _rms(x, g):
    ms = jnp.mean(x * x, axis=-1, keepdims=True)
    return x * lax.rsqrt(ms + NORM_EPS) * g


def _softplus(x):
    return jnp.maximum(x, 0.0) + jnp.log1p(jnp.exp(-jnp.abs(x)))


def _silu(x):
    return x * jax.nn.sigmoid(x)


def _residual(x, y, fg_ref, final):
    r = x + y
    if final:
        r = _rms(r, fg_ref[...])
    return r


def _chunk_tri(n, chunk):
    r = lax.broadcasted_iota(jnp.int32, (n, n), 0)
    c = lax.broadcasted_iota(jnp.int32, (n, n), 1)
    keep = jnp.logical_and(r // chunk == c // chunk, c <= r)
    return jnp.where(keep, 1.0, 0.0).astype(BF16)


def _split_bf16(x, parts):
    out = []
    for _ in range(parts - 1):
        hi = x.astype(BF16)
        out.append(hi)
        x = x - hi.astype(F32)
    out.append(x.astype(BF16))
    return out


def _const_spec(shape):
    n = len(shape)
    return pl.BlockSpec(shape, lambda *_: (0,) * n, pipeline_mode=pl.Buffered(1))


def _seq_spec(tm, d):
    return pl.BlockSpec((1, tm, d), lambda b, j: (b, j, 0))


def _lru_kernel(h_ref, ng_ref, win_ref, cw_ref, cb_ref, wax_ref, ba_ref, bx_ref, lam_ref, wout_ref,
                fg_ref, o_ref, ubuf, abuf, bbuf, gbuf, hst, *, tm, width, final):
    @pl.when(pl.program_id(1) == 0)
    def _():
        ubuf[0:SUBLANES, :] = jnp.zeros((SUBLANES, width), F32)
        hst[...] = jnp.zeros_like(hst)

    x = h_ref[0]
    un = _rms(x, ng_ref[...]).astype(BF16)
    t = _dot(un, win_ref[...])
    gbuf[...] = _silu(t[:, :width])
    ubuf[SUBLANES:SUBLANES + tm, :] = t[:, width:]
    cw = cw_ref[...]
    conv = cb_ref[...] + cw[CONV_W - 1:CONV_W] * ubuf[SUBLANES:SUBLANES + tm, :]
    for k in range(CONV_W - 1):
        d = CONV_W - 1 - k
        conv = conv + cw[k:k + 1] * ubuf[SUBLANES - d:SUBLANES - d + tm, :]
    ubuf[0:SUBLANES, :] = ubuf[tm:tm + SUBLANES, :]

    sp = _softplus(-lam_ref[...])
    for n in range(width // LANES):
        sl = slice(n * LANES, (n + 1) * LANES)
        cn = conv[:, sl]
        ra = _dot(cn.astype(BF16), wax_ref[n])
        r = jax.nn.sigmoid(ra[:, :LANES] + ba_ref[:, sl])
        i = jax.nn.sigmoid(ra[:, LANES:] + bx_ref[:, sl])
        log_a = -LRU_C * r * sp[:, sl]
        th = jnp.tanh(log_a)
        abuf[:, sl] = jnp.exp(log_a)
        bbuf[:, sl] = jnp.sqrt(-2.0 * th / (1.0 - th)) * (i * cn)

    rows = lax.broadcasted_iota(jnp.int32, (SUBLANES, width), 0)

    def scan_rows(g, hprev):
        r0 = pl.multiple_of(g * SUBLANES, SUBLANES)
        a = abuf[pl.ds(r0, SUBLANES), :]
        b = bbuf[pl.ds(r0, SUBLANES), :]
        d = 1
        while d < SUBLANES:
            keep = rows >= d
            b = jnp.where(keep, a * pltpu.roll(b, d, 0) + b, b)
            a = jnp.where(keep, a * pltpu.roll(a, d, 0), a)
            d *= 2
        hs = a * hprev + b
        bbuf[pl.ds(r0, SUBLANES), :] = hs
        return jnp.broadcast_to(hs[SUBLANES - 1:SUBLANES, :], (SUBLANES, width))

    hst[...] = lax.fori_loop(0, tm // SUBLANES, scan_rows, hst[...])
    y = (bbuf[...] * gbuf[...]).astype(BF16)
    o_ref[0] = _residual(x, _dot(y, wout_ref[...]), fg_ref, final)


def _lru_layer(h, ng, fg, w_in, conv_w, conv_b, w_a, b_a, w_x, b_x, lam, w_out, *, final, tm=256):
    B, S, D = h.shape
    width = w_out.shape[0]
    wax = jnp.concatenate([w_a, w_x], axis=-1).astype(BF16)
    row = lambda v: v.reshape(1, -1).astype(F32)
    kern = functools.partial(_lru_kernel, tm=tm, width=width, final=final)
    return pl.pallas_call(
        kern,
        grid=(B, S // tm),
        in_specs=[_seq_spec(tm, D), _const_spec((1, D)), _const_spec((D, 2 * width)),
                  _const_spec((CONV_W, width)), _const_spec((1, width)), _const_spec(wax.shape),
                  _const_spec((1, width)), _const_spec((1, width)), _const_spec((1, width)),
                  _const_spec((width, D)), _const_spec((1, D))],
        out_specs=_seq_spec(tm, D),
        out_shape=jax.ShapeDtypeStruct((B, S, D), F32),
        scratch_shapes=[pltpu.VMEM((tm + SUBLANES, width), F32), pltpu.VMEM((tm, width), F32),
                        pltpu.VMEM((tm, width), F32), pltpu.VMEM((tm, width), F32),
                        pltpu.VMEM((SUBLANES, width), F32)],
        compiler_params=pltpu.CompilerParams(dimension_semantics=("arbitrary", "arbitrary"),
                                             vmem_limit_bytes=VMEM_LIMIT),
        name="lru_layer",
    )(h, row(ng), w_in.astype(BF16), conv_w.astype(F32), row(conv_b), wax, row(b_a), row(b_x),
      row(lam), w_out.astype(BF16), row(fg))


def _gla_kernel(h_ref, ng_ref, wq_ref, wgk_ref, wgk2_ref, bgk_ref, go_ref, wout_ref, fg_ref, o_ref,
                sst, obuf, *, tm, key, val, final):
    heads, chunk = GLA_HEADS, GLA_CHUNK
    dk, dv = key // heads, val // heads

    @pl.when(pl.program_id(1) == 0)
    def _():
        sst[...] = jnp.zeros_like(sst)

    x = h_ref[0]
    un = _rms(x, ng_ref[...]).astype(BF16)
    t = _dot(un, wq_ref[...])
    q = t[:, :key] * dk ** -0.5
    k = t[:, key:2 * key]
    v = t[:, 2 * key:2 * key + val].astype(BF16)
    gate = t[:, 2 * key + val:]
    gk = _dot(un, wgk_ref[...]).astype(BF16)
    log_a = -_softplus(-(_dot(gk, wgk2_ref[...]) + bgk_ref[...])) * (1.0 / GLA_TAU)
    tri = _chunk_tri(tm, chunk)
    b = sum(_dot(tri, part) for part in _split_bf16(log_a, 2))

    rr = lax.broadcasted_iota(jnp.int32, (chunk, chunk), 0)
    cc = lax.broadcasted_iota(jnp.int32, (chunk, chunk), 1)
    causal = cc <= rr
    for ci in range(tm // chunk):
        rs = slice(ci * chunk, (ci + 1) * chunk)
        bc = b[rs]
        bl = bc[chunk - 1:chunk, :]
        qt = (q[rs] * jnp.exp(bc)).astype(BF16)
        kt = (k[rs] * jnp.exp(-bc)).astype(BF16)
        ke = (k[rs] * jnp.exp(bl - bc)).astype(BF16)
        dec = jnp.exp(bl)
        for hh in range(heads):
            ks = slice(hh * dk, (hh + 1) * dk)
            vs = slice(hh * dv, (hh + 1) * dv)
            att = jnp.where(causal, _dot_nt(qt[:, ks], kt[:, ks]), 0.0).astype(BF16)
            s_prev = sst[hh]
            obuf[rs, vs] = _dot(att, v[rs, vs]) + _dot_nt(qt[:, ks], s_prev.astype(BF16))
            sst[hh] = dec[:, ks] * s_prev + _dot_tn(v[rs, vs], ke[:, ks])

    parts = []
    for hh in range(heads):
        oh = obuf[:, hh * dv:(hh + 1) * dv]
        parts.append(oh * lax.rsqrt(jnp.mean(oh * oh, axis=-1, keepdims=True) + NORM_EPS))
    on = jnp.concatenate(parts, axis=1) * go_ref[...]
    y = (on * _silu(gate)).astype(BF16)
    o_ref[0] = _residual(x, _dot(y, wout_ref[...]), fg_ref, final)


def _gla_layer(h, ng, fg, w_in, w_gk2, b_gk, g_o, w_out, *, final, tm=256):
    B, S, D = h.shape
    val = w_out.shape[0]
    key = (w_in.shape[1] - 2 * val - GLA_GATE_RANK) // 2
    main = 2 * key + 2 * val
    pad = LANES - GLA_GATE_RANK
    wq = w_in[:, :main].astype(BF16)
    wgk = jnp.pad(w_in[:, main:], ((0, 0), (0, pad))).astype(BF16)
    wgk2 = jnp.pad(w_gk2, ((0, pad), (0, 0))).astype(BF16)
    row = lambda v: v.reshape(1, -1).astype(F32)
    kern = functools.partial(_gla_kernel, tm=tm, key=key, val=val, final=final)
    return pl.pallas_call(
        kern,
        grid=(B, S // tm),
        in_specs=[_seq_spec(tm, D), _const_spec((1, D)), _const_spec((D, main)), _const_spec((D, LANES)),
                  _const_spec((LANES, key)), _const_spec((1, key)), _const_spec((1, val)),
                  _const_spec((val, D)), _const_spec((1, D))],
        out_specs=_seq_spec(tm, D),
        out_shape=jax.ShapeDtypeStruct((B, S, D), F32),
        scratch_shapes=[pltpu.VMEM((GLA_HEADS, val // GLA_HEADS, key // GLA_HEADS), F32),
                        pltpu.VMEM((tm, val), F32)],
        compiler_params=pltpu.CompilerParams(dimension_semantics=("arbitrary", "arbitrary"),
                                             vmem_limit_bytes=VMEM_LIMIT),
        name="gla_layer",
    )(h, row(ng), wq, wgk, wgk2, row(b_gk), row(jnp.tile(g_o, GLA_HEADS)), w_out.astype(BF16), row(fg))


def _ssd_kernel(h_ref, ng_ref, win_ref, cw_ref, cb_ref, dtb_ref, alog_ref, dexp_ref, gn_ref, e_ref,
                wout_ref, fg_ref, o_ref, xbuf, cbuf, csx, dtx, ybuf, sst, *, tm, inner, final):
    groups, ns, chunk = SSD_GROUPS, SSD_STATE, SSD_CHUNK
    gw = inner // groups
    conv_dim = inner + 2 * groups * ns
    half = SSD_HEADDIM

    @pl.when(pl.program_id(1) == 0)
    def _():
        xbuf[0:SUBLANES, :] = jnp.zeros((SUBLANES, conv_dim), F32)
        sst[...] = jnp.zeros_like(sst)

    x = h_ref[0]
    un = _rms(x, ng_ref[...]).astype(BF16)
    z = _dot(un, win_ref[:, 0:inner])
    xbuf[SUBLANES:SUBLANES + tm, :] = _dot(un, win_ref[:, inner:inner + conv_dim])
    dt_raw = _dot(un, win_ref[:, inner + conv_dim:])

    cw = cw_ref[...]
    conv = cb_ref[...] + cw[CONV_W - 1:CONV_W] * xbuf[SUBLANES:SUBLANES + tm, :]
    for k in range(CONV_W - 1):
        d = CONV_W - 1 - k
        conv = conv + cw[k:k + 1] * xbuf[SUBLANES - d:SUBLANES - d + tm, :]
    xbuf[0:SUBLANES, :] = xbuf[tm:tm + SUBLANES, :]
    cbuf[...] = _silu(conv)

    dt = _softplus(dt_raw + dtb_ref[...])
    da = dt * (-jnp.exp(alog_ref[...]))
    tri = _chunk_tri(tm, chunk)
    cs = sum(_dot(tri, part) for part in _split_bf16(da, 3))
    e = e_ref[...]
    csx[...] = sum(_dot(part, e) for part in _split_bf16(cs, 2))
    dtx[...] = _dot(dt.astype(BF16), e)

    ii = lax.broadcasted_iota(jnp.int32, (chunk, LANES), 0)
    ll = lax.broadcasted_iota(jnp.int32, (chunk, LANES), 1)
    eye2 = (ll % half == ii).astype(F32)
    causal2 = ll % half <= ii
    low = ll < half
    dexp = dexp_ref[...]

    def chunk_body(ci, carry):
        r0 = pl.multiple_of(ci * chunk, chunk)
        rs = pl.ds(r0, chunk)
        cs_last = csx[pl.ds(r0 + chunk - 1, 1), :]
        for g in range(groups):
            gl = slice(g * gw, (g + 1) * gw)
            bg = cbuf[rs, inner + g * ns:inner + (g + 1) * ns].astype(BF16)
            cg = cbuf[rs, inner + (groups + g) * ns:inner + (groups + g + 1) * ns].astype(BF16)
            cb2 = _dot_nt(cg, jnp.concatenate([bg, bg], axis=0))
            xg = cbuf[rs, gl]
            csg = csx[rs, gl]
            dtg = dtx[rs, gl]
            yd = []
            for pp in range(gw // LANES):
                ps = slice(pp * LANES, (pp + 1) * LANES)
                col = csg[:, ps]
                row = jnp.sum(col * eye2, axis=0, keepdims=True)
                dtrow = jnp.sum(dtg[:, ps] * eye2, axis=0, keepdims=True)
                lm = jnp.where(causal2, jnp.exp(jnp.minimum(col - row, 0.0)), 0.0)
                w = (cb2 * lm * dtrow).astype(BF16)
                xp = xg[:, ps]
                bd = jnp.concatenate([jnp.where(low, xp, 0.0), jnp.where(low, 0.0, xp)],
                                     axis=0).astype(BF16)
                yd.append(_dot(w, bd))
            s_prev = sst[g]
            y_off = _dot(cg, s_prev.astype(BF16)) * jnp.exp(csg)
            ybuf[rs, gl] = jnp.concatenate(yd, axis=1) + y_off + dexp[:, gl] * xg
            cl = cs_last[:, gl]
            w1 = jnp.exp(cl - csg) * dtg
            sst[g] = s_prev * jnp.exp(cl) + _dot_tn(bg, (xg * w1).astype(BF16))
        return carry

    lax.fori_loop(0, tm // chunk, chunk_body, 0)

    y = ybuf[...] * _silu(z)
    parts = []
    for g in range(groups):
        yg = y[:, g * gw:(g + 1) * gw]
        parts.append(yg * lax.rsqrt(jnp.mean(yg * yg, axis=-1, keepdims=True) + NORM_EPS))
    yn = (jnp.concatenate(parts, axis=1) * gn_ref[...]).astype(BF16)
    o_ref[0] = _residual(x, _dot(yn, wout_ref[...]), fg_ref, final)


def _ssd_layer(h, ng, fg, w_in, conv_w, conv_b, dt_bias, a_log, d_skip, g_norm, w_out, *, final, tm=256):
    B, S, D = h.shape
    inner = w_out.shape[0]
    heads = inner // SSD_HEADDIM
    conv_dim = inner + 2 * SSD_GROUPS * SSD_STATE
    pad = LANES - heads
    win = jnp.pad(w_in, ((0, 0), (0, pad))).astype(BF16)
    row = lambda v: v.reshape(1, -1).astype(F32)
    padrow = lambda v: jnp.pad(v.astype(F32), (0, pad)).reshape(1, LANES)
    expand = (lax.broadcasted_iota(jnp.int32, (LANES, inner), 1) // SSD_HEADDIM
              == lax.broadcasted_iota(jnp.int32, (LANES, inner), 0)).astype(BF16)
    kern = functools.partial(_ssd_kernel, tm=tm, inner=inner, final=final)
    gw = inner // SSD_GROUPS
    return pl.pallas_call(
        kern,
        grid=(B, S // tm),
        in_specs=[_seq_spec(tm, D), _const_spec((1, D)), _const_spec(win.shape),
                  _const_spec((CONV_W, conv_dim)), _const_spec((1, conv_dim)), _const_spec((1, LANES)),
                  _const_spec((1, LANES)), _const_spec((1, inner)), _const_spec((1, inner)),
                  _const_spec((LANES, inner)), _const_spec((inner, D)), _const_spec((1, D))],
        out_specs=_seq_spec(tm, D),
        out_shape=jax.ShapeDtypeStruct((B, S, D), F32),
        scratch_shapes=[pltpu.VMEM((tm + SUBLANES, conv_dim), F32), pltpu.VMEM((tm, conv_dim), F32),
                        pltpu.VMEM((tm, inner), F32), pltpu.VMEM((tm, inner), F32),
                        pltpu.VMEM((tm, inner), F32), pltpu.VMEM((SSD_GROUPS, SSD_STATE, gw), F32)],
        compiler_params=pltpu.CompilerParams(dimension_semantics=("arbitrary", "arbitrary"),
                                             vmem_limit_bytes=VMEM_LIMIT),
        name="ssd_layer",
    )(h, row(ng), win, conv_w.astype(F32), row(conv_b), padrow(dt_bias), padrow(a_log),
      row(jnp.repeat(d_skip, SSD_HEADDIM)), row(g_norm), expand, w_out.astype(BF16), row(fg))


def _mla_proj_kernel(h_ref, pos_ref, ng_ref, win_ref, gq_ref, wuq_ref, gkv_ref, wukv_ref, invf_ref,
                     sgn_ref, q_ref, k_ref, v_ref, g_ref, *, tm):
    heads = MLA_HEADS
    kw = heads * LANES
    x = h_ref[0]
    un = _rms(x, ng_ref[...]).astype(BF16)
    t = _dot(un, win_ref[...])
    c_q = t[:, :MLA_Q_RANK]
    c_kv = t[:, MLA_Q_RANK:MLA_Q_RANK + MLA_KV_RANK]
    g0 = MLA_Q_RANK + MLA_KV_RANK
    gate = t[:, g0:g0 + heads * MLA_V]
    k_r = t[:, g0 + heads * MLA_V:]
    g_ref[0] = _silu(gate).astype(BF16)

    ang = pos_ref[0].astype(F32) * invf_ref[...]
    cos = jnp.cos(ang)
    sin = jnp.sin(ang) * sgn_ref[...]
    lane = lax.broadcasted_iota(jnp.int32, (tm, LANES), 1)
    first_half = lane < MLA_NOPE + MLA_ROPE // 2

    def rope(a):
        swapped = jnp.where(first_half, pltpu.roll(a, LANES - MLA_ROPE // 2, 1),
                            pltpu.roll(a, MLA_ROPE // 2, 1))
        return a * cos + swapped * sin

    k_rope = rope(k_r)
    qn = _dot(_rms(c_q, gq_ref[...]).astype(BF16), wuq_ref[...])
    kvn = _dot(_rms(c_kv, gkv_ref[...]).astype(BF16), wukv_ref[...])
    scale = (MLA_NOPE + MLA_ROPE) ** -0.5
    for hh in range(heads):
        hs = slice(hh * LANES, (hh + 1) * LANES)
        q_ref[0, hh] = (rope(qn[:, hs]) * scale).astype(BF16)
        k_ref[0, hh] = (kvn[:, hs] + k_rope).astype(BF16)
    for hp in range(heads // 2):
        v_ref[0, hp] = kvn[:, kw + hp * LANES:kw + (hp + 1) * LANES].astype(BF16)


def _mla_attn_kernel(q_ref, k_ref, v_ref, g_ref, h_ref, wout_ref, fg_ref, o_ref, obuf, *, tq, final):
    heads = MLA_HEADS
    j = pl.program_id(1)
    lane = lax.broadcasted_iota(jnp.int32, (tq, LANES), 1)
    rr = lax.broadcasted_iota(jnp.int32, (tq, tq), 0)
    cc = lax.broadcasted_iota(jnp.int32, (tq, tq), 1)
    diag = cc <= rr

    def pair_body(hp, carry):
        outs = []
        for e in range(2):
            hh = hp * 2 + e
            q = q_ref[0, hh]

            def update(kt, state, masked):
                m, l, acc = state
                k0 = pl.multiple_of(kt * tq, tq)
                s = _dot_nt(q, k_ref[0, hh, pl.ds(k0, tq), :])
                if masked:
                    s = jnp.where(diag, s, -jnp.inf)
                m_new = jnp.maximum(m, jnp.max(s, axis=-1, keepdims=True))
                alpha = jnp.exp(m - m_new)
                p = jnp.exp(s - m_new)
                l = alpha * l + jnp.sum(p, axis=-1, keepdims=True)
                acc = alpha * acc + _dot(p.astype(BF16), v_ref[0, hp, pl.ds(k0, tq), :])
                return m_new, l, acc

            init = (jnp.full((tq, 1), -jnp.inf, F32), jnp.zeros((tq, 1), F32),
                    jnp.zeros((tq, LANES), F32))
            state = lax.fori_loop(0, j, lambda kt, st: update(kt, st, False), init)
            _, l, acc = update(j, state, True)
            outs.append(acc / l)
        obuf[hp] = jnp.where(lane < MLA_V, outs[0], outs[1])
        return carry

    lax.fori_loop(0, heads // 2, pair_body, 0)
    o = jnp.concatenate([obuf[hp] for hp in range(heads // 2)], axis=1)
    y = (o * g_ref[0].astype(F32)).astype(BF16)
    o_ref[0] = _residual(h_ref[0], _dot(y, wout_ref[...]), fg_ref, final)


def _mla_layer(h, positions, ng, fg, w_in, g_q, w_uq, g_kv, w_ukv, w_out, *, final, tm=256, tq=256):
    B, S, D = h.shape
    heads, half = MLA_HEADS, MLA_ROPE // 2
    qk = MLA_NOPE + MLA_ROPE
    width = heads * MLA_V
    c0 = MLA_Q_RANK + MLA_KV_RANK
    w_kr = jnp.pad(w_in[:, c0:c0 + MLA_ROPE], ((0, 0), (MLA_NOPE, LANES - qk)))
    win = jnp.concatenate([w_in[:, :c0], w_in[:, c0 + MLA_ROPE:], w_kr], axis=1).astype(BF16)
    wuq = jnp.pad(w_uq.reshape(MLA_Q_RANK, heads, qk), ((0, 0), (0, 0), (0, LANES - qk)))
    wuq = wuq.reshape(MLA_Q_RANK, heads * LANES).astype(BF16)
    wkv = w_ukv.reshape(MLA_KV_RANK, heads, MLA_NOPE + MLA_V)
    wk = jnp.pad(wkv[:, :, :MLA_NOPE], ((0, 0), (0, 0), (0, LANES - MLA_NOPE)))
    wukv = jnp.concatenate([wk.reshape(MLA_KV_RANK, heads * LANES),
                            wkv[:, :, MLA_NOPE:].reshape(MLA_KV_RANK, width)], axis=1).astype(BF16)
    inv_freq = ROPE_THETA ** (-jnp.arange(0, MLA_ROPE, 2, dtype=F32) / MLA_ROPE)
    zeros = lambda n: jnp.zeros((n,), F32)
    invf = jnp.concatenate([zeros(MLA_NOPE), inv_freq, inv_freq, zeros(LANES - qk)]).reshape(1, LANES)
    sgn = jnp.concatenate([jnp.ones((MLA_NOPE,), F32), -jnp.ones((half,), F32),
                           jnp.ones((LANES - MLA_NOPE - half,), F32)]).reshape(1, LANES)
    row = lambda v: v.reshape(1, -1).astype(F32)

    head_spec = lambda n: pl.BlockSpec((1, n, tm, LANES), lambda b, j: (b, 0, j, 0))
    q, k, v, g = pl.pallas_call(
        functools.partial(_mla_proj_kernel, tm=tm),
        grid=(B, S // tm),
        in_specs=[_seq_spec(tm, D), _seq_spec(tm, 1), _const_spec((1, D)), _const_spec(win.shape),
                  _const_spec((1, MLA_Q_RANK)), _const_spec(wuq.shape), _const_spec((1, MLA_KV_RANK)),
                  _const_spec(wukv.shape), _const_spec((1, LANES)), _const_spec((1, LANES))],
        out_specs=[head_spec(heads), head_spec(heads), head_spec(heads // 2), _seq_spec(tm, width)],
        out_shape=[jax.ShapeDtypeStruct((B, heads, S, LANES), BF16),
                   jax.ShapeDtypeStruct((B, heads, S, LANES), BF16),
                   jax.ShapeDtypeStruct((B, heads // 2, S, LANES), BF16),
                   jax.ShapeDtypeStruct((B, S, width), BF16)],
        compiler_params=pltpu.CompilerParams(dimension_semantics=("arbitrary", "arbitrary"),
                                             vmem_limit_bytes=VMEM_LIMIT),
        name="mla_proj",
    )(h, positions.reshape(B, S, 1), row(ng), win, row(g_q), wuq, row(g_kv), wukv, invf, sgn)

    full_spec = lambda n: pl.BlockSpec((1, n, S, LANES), lambda b, j: (b, 0, 0, 0))
    return pl.pallas_call(
        functools.partial(_mla_attn_kernel, tq=tq, final=final),
        grid=(B, S // tq),
        in_specs=[pl.BlockSpec((1, heads, tq, LANES), lambda b, j: (b, 0, j, 0)), full_spec(heads),
                  full_spec(heads // 2), _seq_spec(tq, width), _seq_spec(tq, D),
                  _const_spec((width, D)), _const_spec((1, D))],
        out_specs=_seq_spec(tq, D),
        out_shape=jax.ShapeDtypeStruct((B, S, D), F32),
        scratch_shapes=[pltpu.VMEM((heads // 2, tq, LANES), F32)],
        compiler_params=pltpu.CompilerParams(dimension_semantics=("arbitrary", "arbitrary"),
                                             vmem_limit_bytes=VMEM_LIMIT),
        name="mla_attn",
    )(q, k, v, g, h, w_out.astype(BF16), row(fg))


def kernel(x, positions, norm_g, final_g, mla_w_in, mla_g_q, mla_w_uq, mla_g_kv, mla_w_ukv, mla_w_out, gla_w_in, gla_w_gk2, gla_b_gk, gla_g_o, gla_w_out, lru_w_in, lru_conv_w, lru_conv_b, lru_w_a, lru_b_a, lru_w_x, lru_b_x, lru_lam, lru_w_out, ssd_w_in, ssd_conv_w, ssd_conv_b, ssd_dt_bias, ssd_a_log, ssd_d, ssd_g_norm, ssd_w_out):
    depth = norm_g.shape[0]
    h = x
    for i in range(depth):
        m, j = i % N_MIXERS, i // N_MIXERS
        common = dict(final=(i == depth - 1))
        if m == 0:
            h = _mla_layer(h, positions, norm_g[i], final_g, mla_w_in[j], mla_g_q[j], mla_w_uq[j],
                           mla_g_kv[j], mla_w_ukv[j], mla_w_out[j], **common)
        elif m == 1:
            h = _gla_layer(h, norm_g[i], final_g, gla_w_in[j], gla_w_gk2[j], gla_b_gk[j], gla_g_o[j],
                           gla_w_out[j], **common)
        elif m == 2:
            h = _lru_layer(h, norm_g[i], final_g, lru_w_in[j], lru_conv_w[j], lru_conv_b[j], lru_w_a[j],
                           lru_b_a[j], lru_w_x[j], lru_b_x[j], lru_lam[j], lru_w_out[j], **common)
        else:
            h = _ssd_layer(h, norm_g[i], final_g, ssd_w_in[j], ssd_conv_w[j], ssd_conv_b[j],
                           ssd_dt_bias[j], ssd_a_log[j], ssd_d[j], ssd_g_norm[j], ssd_w_out[j], **common)
    return h
```

```python
import functools

import jax
import jax.numpy as jnp
from jax import lax
from jax.experimental import pallas as pl
from jax.experimental.pallas import tpu as pltpu

F32 = jnp.float32
BF16 = jnp.bfloat16

NORM_EPS = 1e-6
N_MIXERS = 4
MLA_HEADS = 16
MLA_Q_RANK = 384
MLA_KV_RANK = 256
MLA_NOPE = 64
MLA_ROPE = 32
MLA_V = 64
MLA_LOOKAHEAD = 4
ROPE_THETA = 10000.0
GLA_HEADS = 4
GLA_GATE_RANK = 16
GLA_TAU = 16.0
GLA_CHUNK = 64
LRU_BLOCKS = 10
LRU_C = 8.0
CONV_W = 4
SSD_HEADDIM = 64
SSD_GROUPS = 8
SSD_STATE = 128
SSD_CHUNK = 64

LOG2_E = 1.4426950408889634
LANES = 128
SUBLANES = 8
VMEM_LIMIT = 56 * 1024 * 1024


def _dot(a, b):
    return jnp.dot(a, b, preferred_element_type=F32)


def _dot_nt(a, b):
    return lax.dot_general(a, b, (((1,), (1,)), ((), ())), preferred_element_type=F32)


def _dot_tn(a, b):
    return lax.dot_general(a, b, (((0,), (0,)), ((), ())), preferred_element_type=F32)


def _rms(x, g):
    ms = jnp.mean(x * x, axis=-1, keepdims=True)
    return x * lax.rsqrt(ms + NORM_EPS) * g


def _softplus(x):
    return jnp.maximum(x, 0.0) + jnp.log1p(jnp.exp(-jnp.abs(x)))


def _silu(x):
    return x * jax.nn.sigmoid(x)


def _residual(x, y, fg_ref, final):
    r = x + y
    if final:
        r = _rms(r, fg_ref[...])
    return r


def _chunk_tri(n, chunk):
    r = lax.broadcasted_iota(jnp.int32, (n, n), 0)
    c = lax.broadcasted_iota(jnp.int32, (n, n), 1)
    keep = jnp.logical_and(r // chunk == c // chunk, c <= r)
    return jnp.where(keep, 1.0, 0.0).astype(BF16)


def _split_bf16(x, parts):
    out = []
    for _ in range(parts - 1):
        hi = x.astype(BF16)
        out.append(hi)
        x = x - hi.astype(F32)
    out.append(x.astype(BF16))
    return out


def _const_spec(shape):
    n = len(shape)
    return pl.BlockSpec(shape, lambda *_: (0,) * n, pipeline_mode=pl.Buffered(1))


def _seq_spec(tm, d):
    return pl.BlockSpec((1, tm, d), lambda b, j: (b, j, 0))


def _lru_kernel(h_ref, ng_ref, win_ref, cw_ref, cb_ref, wax_ref, ba_ref, bx_ref, lam_ref, wout_ref,
                fg_ref, o_ref, ubuf, abuf, bbuf, gbuf, hst, *, tm, width, final):
    @pl.when(pl.program_id(1) == 0)
    def _():
        ubuf[0:SUBLANES, :] = jnp.zeros((SUBLANES, width), F32)
        hst[...] = jnp.zeros_like(hst)

    x = h_ref[0]
    un = _rms(x, ng_ref[...]).astype(BF16)
    t = _dot(un, win_ref[...])
    gbuf[...] = _silu(t[:, :width])
    ubuf[SUBLANES:SUBLANES + tm, :] = t[:, width:]
    cw = cw_ref[...]
    conv = cb_ref[...] + cw[CONV_W - 1:CONV_W] * ubuf[SUBLANES:SUBLANES + tm, :]
    for k in range(CONV_W - 1):
        d = CONV_W - 1 - k
        conv = conv + cw[k:k + 1] * ubuf[SUBLANES - d:SUBLANES - d + tm, :]
    ubuf[0:SUBLANES, :] = ubuf[tm:tm + SUBLANES, :]

    sp = _softplus(-lam_ref[...])
    for n in range(width // LANES):
        sl = slice(n * LANES, (n + 1) * LANES)
        cn = conv[:, sl]
        ra = _dot(cn.astype(BF16), wax_ref[n])
        r = jax.nn.sigmoid(ra[:, :LANES] + ba_ref[:, sl])
        i = jax.nn.sigmoid(ra[:, LANES:] + bx_ref[:, sl])
        log_a = -LRU_C * r * sp[:, sl]
        th = jnp.tanh(log_a)
        abuf[:, sl] = jnp.exp(log_a)
        bbuf[:, sl] = jnp.sqrt(-2.0 * th / (1.0 - th)) * (i * cn)

    rows = lax.broadcasted_iota(jnp.int32, (SUBLANES, width), 0)

    def scan_rows(g, hprev):
        r0 = pl.multiple_of(g * SUBLANES, SUBLANES)
        a = abuf[pl.ds(r0, SUBLANES), :]
        b = bbuf[pl.ds(r0, SUBLANES), :]
        d = 1
        while d < SUBLANES:
            keep = rows >= d
            b = jnp.where(keep, a * pltpu.roll(b, d, 0) + b, b)
            a = jnp.where(keep, a * pltpu.roll(a, d, 0), a)
            d *= 2
        hs = a * hprev + b
        bbuf[pl.ds(r0, SUBLANES), :] = hs
        return jnp.broadcast_to(hs[SUBLANES - 1:SUBLANES, :], (SUBLANES, width))

    hst[...] = lax.fori_loop(0, tm // SUBLANES, scan_rows, hst[...])
    y = (bbuf[...] * gbuf[...]).astype(BF16)
    o_ref[0] = _residual(x, _dot(y, wout_ref[...]), fg_ref, final)


def _lru_layer(h, ng, fg, w_in, conv_w, conv_b, w_a, b_a, w_x, b_x, lam, w_out, *, final, tm=256):
    B, S, D = h.shape
    width = w_out.shape[0]
    wax = jnp.concatenate([w_a, w_x], axis=-1).astype(BF16)
    row = lambda v: v.reshape(1, -1).astype(F32)
    kern = functools.partial(_lru_kernel, tm=tm, width=width, final=final)
    return pl.pallas_call(
        kern,
        grid=(B, S // tm),
        in_specs=[_seq_spec(tm, D), _const_spec((1, D)), _const_spec((D, 2 * width)),
                  _const_spec((CONV_W, width)), _const_spec((1, width)), _const_spec(wax.shape),
                  _const_spec((1, width)), _const_spec((1, width)), _const_spec((1, width)),
                  _const_spec((width, D)), _const_spec((1, D))],
        out_specs=_seq_spec(tm, D),
        out_shape=jax.ShapeDtypeStruct((B, S, D), F32),
        scratch_shapes=[pltpu.VMEM((tm + SUBLANES, width), F32), pltpu.VMEM((tm, width), F32),
                        pltpu.VMEM((tm, width), F32), pltpu.VMEM((tm, width), F32),
                        pltpu.VMEM((SUBLANES, width), F32)],
        compiler_params=pltpu.CompilerParams(dimension_semantics=("arbitrary", "arbitrary"),
                                             vmem_limit_bytes=VMEM_LIMIT),
        name="lru_layer",
    )(h, row(ng), w_in.astype(BF16), conv_w.astype(F32), row(conv_b), wax, row(b_a), row(b_x),
      row(lam), w_out.astype(BF16), row(fg))


def _gla_kernel(h_ref, ng_ref, wq_ref, wgk_ref, wgk2_ref, bgk_ref, go_ref, wout_ref, fg_ref, o_ref,
                sst, obuf, *, tm, key, val, final):
    heads, chunk = GLA_HEADS, GLA_CHUNK
    dk, dv = key // heads, val // heads

    @pl.when(pl.program_id(1) == 0)
    def _():
        sst[...] = jnp.zeros_like(sst)

    x = h_ref[0]
    un = _rms(x, ng_ref[...]).astype(BF16)
    t = _dot(un, wq_ref[...])
    q = t[:, :key] * dk ** -0.5
    k = t[:, key:2 * key]
    v = t[:, 2 * key:2 * key + val].astype(BF16)
    gate = t[:, 2 * key + val:]
    gk = _dot(un, wgk_ref[...]).astype(BF16)
    log_a = -_softplus(-(_dot(gk, wgk2_ref[...]) + bgk_ref[...])) * (1.0 / GLA_TAU)
    tri = _chunk_tri(tm, chunk)
    b = sum(_dot(tri, part) for part in _split_bf16(log_a, 2))

    rr = lax.broadcasted_iota(jnp.int32, (chunk, chunk), 0)
    cc = lax.broadcasted_iota(jnp.int32, (chunk, chunk), 1)
    causal = cc <= rr
    for ci in range(tm // chunk):
        rs = slice(ci * chunk, (ci + 1) * chunk)
        bc = b[rs]
        bl = bc[chunk - 1:chunk, :]
        qt = (q[rs] * jnp.exp(bc)).astype(BF16)
        kt = (k[rs] * jnp.exp(-bc)).astype(BF16)
        ke = (k[rs] * jnp.exp(bl - bc)).astype(BF16)
        dec = jnp.exp(bl)
        for hh in range(heads):
            ks = slice(hh * dk, (hh + 1) * dk)
            vs = slice(hh * dv, (hh + 1) * dv)
            att = jnp.where(causal, _dot_nt(qt[:, ks], kt[:, ks]), 0.0).astype(BF16)
            s_prev = sst[hh]
            obuf[rs, vs] = _dot(att, v[rs, vs]) + _dot_nt(qt[:, ks], s_prev.astype(BF16))
            sst[hh] = dec[:, ks] * s_prev + _dot_tn(v[rs, vs], ke[:, ks])

    parts = []
    for hh in range(heads):
        oh = obuf[:, hh * dv:(hh + 1) * dv]
        parts.append(oh * lax.rsqrt(jnp.mean(oh * oh, axis=-1, keepdims=True) + NORM_EPS))
    on = jnp.concatenate(parts, axis=1) * go_ref[...]
    y = (on * _silu(gate)).astype(BF16)
    o_ref[0] = _residual(x, _dot(y, wout_ref[...]), fg_ref, final)


def _gla_layer(h, ng, fg, w_in, w_gk2, b_gk, g_o, w_out, *, final, tm=256):
    B, S, D = h.shape
    val = w_out.shape[0]
    key = (w_in.shape[1] - 2 * val - GLA_GATE_RANK) // 2
    main = 2 * key + 2 * val
    pad = LANES - GLA_GATE_RANK
    wq = w_in[:, :main].astype(BF16)
    wgk = jnp.pad(w_in[:, main:], ((0, 0), (0, pad))).astype(BF16)
    wgk2 = jnp.pad(w_gk2, ((0, pad), (0, 0))).astype(BF16)
    row = lambda v: v.reshape(1, -1).astype(F32)
    kern = functools.partial(_gla_kernel, tm=tm, key=key, val=val, final=final)
    return pl.pallas_call(
        kern,
        grid=(B, S // tm),
        in_specs=[_seq_spec(tm, D), _const_spec((1, D)), _const_spec((D, main)), _const_spec((D, LANES)),
                  _const_spec((LANES, key)), _const_spec((1, key)), _const_spec((1, val)),
                  _const_spec((val, D)), _const_spec((1, D))],
        out_specs=_seq_spec(tm, D),
        out_shape=jax.ShapeDtypeStruct((B, S, D), F32),
        scratch_shapes=[pltpu.VMEM((GLA_HEADS, val // GLA_HEADS, key // GLA_HEADS), F32),
                        pltpu.VMEM((tm, val), F32)],
        compiler_params=pltpu.CompilerParams(dimension_semantics=("arbitrary", "arbitrary"),
                                             vmem_limit_bytes=VMEM_LIMIT),
        name="gla_layer",
    )(h, row(ng), wq, wgk, wgk2, row(b_gk), row(jnp.tile(g_o, GLA_HEADS)), w_out.astype(BF16), row(fg))


def _ssd_kernel(h_ref, ng_ref, win_ref, cw_ref, cb_ref, dtb_ref, alog_ref, dexp_ref, gn_ref, e_ref,
                wout_ref, fg_ref, o_ref, xbuf, cbuf, csx, dtx, ybuf, sst, *, tm, inner, final):
    groups, ns, chunk = SSD_GROUPS, SSD_STATE, SSD_CHUNK
    gw = inner // groups
    conv_dim = inner + 2 * groups * ns
    half = SSD_HEADDIM

    @pl.when(pl.program_id(1) == 0)
    def _():
        xbuf[0:SUBLANES, :] = jnp.zeros((SUBLANES, conv_dim), F32)
        sst[...] = jnp.zeros_like(sst)

    x = h_ref[0]
    un = _rms(x, ng_ref[...]).astype(BF16)
    z = _dot(un, win_ref[:, 0:inner])
    xbuf[SUBLANES:SUBLANES + tm, :] = _dot(un, win_ref[:, inner:inner + conv_dim])
    dt_raw = _dot(un, win_ref[:, inner + conv_dim:])

    cw = cw_ref[...]
    conv = cb_ref[...] + cw[CONV_W - 1:CONV_W] * xbuf[SUBLANES:SUBLANES + tm, :]
    for k in range(CONV_W - 1):
        d = CONV_W - 1 - k
        conv = conv + cw[k:k + 1] * xbuf[SUBLANES - d:SUBLANES - d + tm, :]
    xbuf[0:SUBLANES, :] = xbuf[tm:tm + SUBLANES, :]
    cbuf[...] = _silu(conv)

    dt = _softplus(dt_raw + dtb_ref[...])
    da = dt * (-jnp.exp(alog_ref[...]))
    tri = _chunk_tri(tm, chunk)
    cs = sum(_dot(tri, part) for part in _split_bf16(da, 3))
    e = e_ref[...]
    csx[...] = sum(_dot(part, e) for part in _split_bf16(cs, 2))
    dtx[...] = _dot(dt.astype(BF16), e)

    ii = lax.broadcasted_iota(jnp.int32, (chunk, LANES), 0)
    ll = lax.broadcasted_iota(jnp.int32, (chunk, LANES), 1)
    eye2 = (ll % half == ii).astype(F32)
    causal2 = ll % half <= ii
    low = ll < half
    dexp = dexp_ref[...]

    def chunk_body(ci, carry):
        r0 = pl.multiple_of(ci * chunk, chunk)
        rs = pl.ds(r0, chunk)
        cs_last = csx[pl.ds(r0 + chunk - 1, 1), :]
        for g in range(groups):
            gl = slice(g * gw, (g + 1) * gw)
            bg = cbuf[rs, inner + g * ns:inner + (g + 1) * ns].astype(BF16)
            cg = cbuf[rs, inner + (groups + g) * ns:inner + (groups + g + 1) * ns].astype(BF16)
            cb2 = _dot_nt(cg, jnp.concatenate([bg, bg], axis=0))
            xg = cbuf[rs, gl]
            csg = csx[rs, gl]
            dtg = dtx[rs, gl]
            yd = []
            for pp in range(gw // LANES):
                ps = slice(pp * LANES, (pp + 1) * LANES)
                col = csg[:, ps]
                row = jnp.sum(col * eye2, axis=0, keepdims=True)
                dtrow = jnp.sum(dtg[:, ps] * eye2, axis=0, keepdims=True)
                lm = jnp.where(causal2, jnp.exp(jnp.minimum(col - row, 0.0)), 0.0)
                w = (cb2 * lm * dtrow).astype(BF16)
                xp = xg[:, ps]
                bd = jnp.concatenate([jnp.where(low, xp, 0.0), jnp.where(low, 0.0, xp)],
                                     axis=0).astype(BF16)
                yd.append(_dot(w, bd))
            s_prev = sst[g]
            y_off = _dot(cg, s_prev.astype(BF16)) * jnp.exp(csg)
            ybuf[rs, gl] = jnp.concatenate(yd, axis=1) + y_off + dexp[:, gl] * xg
            cl = cs_last[:, gl]
            w1 = jnp.exp(cl - csg) * dtg
            sst[g] = s_prev * jnp.exp(cl) + _dot_tn(bg, (xg * w1).astype(BF16))
        return carry

    lax.fori_loop(0, tm // chunk, chunk_body, 0)

    y = ybuf[...] * _silu(z)
    parts = []
    for g in range(groups):
        yg = y[:, g * gw:(g + 1) * gw]
        parts.append(yg * lax.rsqrt(jnp.mean(yg * yg, axis=-1, keepdims=True) + NORM_EPS))
    yn = (jnp.concatenate(parts, axis=1) * gn_ref[...]).astype(BF16)
    o_ref[0] = _residual(x, _dot(yn, wout_ref[...]), fg_ref, final)


def _ssd_layer(h, ng, fg, w_in, conv_w, conv_b, dt_bias, a_log, d_skip, g_norm, w_out, *, final, tm=256):
    B, S, D = h.shape
    inner = w_out.shape[0]
    heads = inner // SSD_HEADDIM
    conv_dim = inner + 2 * SSD_GROUPS * SSD_STATE
    pad = LANES - heads
    win = jnp.pad(w_in, ((0, 0), (0, pad))).astype(BF16)
    row = lambda v: v.reshape(1, -1).astype(F32)
    padrow = lambda v: jnp.pad(v.astype(F32), (0, pad)).reshape(1, LANES)
    expand = (lax.broadcasted_iota(jnp.int32, (LANES, inner), 1) // SSD_HEADDIM
              == lax.broadcasted_iota(jnp.int32, (LANES, inner), 0)).astype(BF16)
    kern = functools.partial(_ssd_kernel, tm=tm, inner=inner, final=final)
    gw = inner // SSD_GROUPS
    return pl.pallas_call(
        kern,
        grid=(B, S // tm),
        in_specs=[_seq_spec(tm, D), _const_spec((1, D)), _const_spec(win.shape),
                  _const_spec((CONV_W, conv_dim)), _const_spec((1, conv_dim)), _const_spec((1, LANES)),
                  _const_spec((1, LANES)), _const_spec((1, inner)), _const_spec((1, inner)),
                  _const_spec((LANES, inner)), _const_spec((inner, D)), _const_spec((1, D))],
        out_specs=_seq_spec(tm, D),
        out_shape=jax.ShapeDtypeStruct((B, S, D), F32),
        scratch_shapes=[pltpu.VMEM((tm + SUBLANES, conv_dim), F32), pltpu.VMEM((tm, conv_dim), F32),
                        pltpu.VMEM((tm, inner), F32), pltpu.VMEM((tm, inner), F32),
                        pltpu.VMEM((tm, inner), F32), pltpu.VMEM((SSD_GROUPS, SSD_STATE, gw), F32)],
        compiler_params=pltpu.CompilerParams(dimension_semantics=("arbitrary", "arbitrary"),
                                             vmem_limit_bytes=VMEM_LIMIT),
        name="ssd_layer",
    )(h, row(ng), win, conv_w.astype(F32), row(conv_b), padrow(dt_bias), padrow(a_log),
      row(jnp.repeat(d_skip, SSD_HEADDIM)), row(g_norm), expand, w_out.astype(BF16), row(fg))


def _mla_proj_kernel(h_ref, pos_ref, ng_ref, win_ref, gq_ref, wuq_ref, gkv_ref, wukv_ref, invf_ref,
                     sgn_ref, q_ref, k_ref, v_ref, g_ref, *, tm):
    heads = MLA_HEADS
    kw = heads * LANES
    x = h_ref[0]
    un = _rms(x, ng_ref[...]).astype(BF16)
    t = _dot(un, win_ref[...])
    c_q = t[:, :MLA_Q_RANK]
    c_kv = t[:, MLA_Q_RANK:MLA_Q_RANK + MLA_KV_RANK]
    g0 = MLA_Q_RANK + MLA_KV_RANK
    gate = t[:, g0:g0 + heads * MLA_V]
    k_r = t[:, g0 + heads * MLA_V:]
    g_ref[0] = _silu(gate).astype(BF16)

    ang = pos_ref[0].astype(F32) * invf_ref[...]
    cos = jnp.cos(ang)
    sin = jnp.sin(ang) * sgn_ref[...]
    lane = lax.broadcasted_iota(jnp.int32, (tm, LANES), 1)
    first_half = lane < MLA_NOPE + MLA_ROPE // 2

    def rope(a):
        swapped = jnp.where(first_half, pltpu.roll(a, LANES - MLA_ROPE // 2, 1),
                            pltpu.roll(a, MLA_ROPE // 2, 1))
        return a * cos + swapped * sin

    k_rope = rope(k_r)
    qn = _dot(_rms(c_q, gq_ref[...]).astype(BF16), wuq_ref[...])
    kvn = _dot(_rms(c_kv, gkv_ref[...]).astype(BF16), wukv_ref[...])
    scale = (MLA_NOPE + MLA_ROPE) ** -0.5 * LOG2_E
    for hh in range(heads):
        hs = slice(hh * LANES, (hh + 1) * LANES)
        q_ref[0, hh] = (rope(qn[:, hs]) * scale).astype(BF16)
        k_ref[0, hh] = (kvn[:, hs] + k_rope).astype(BF16)
    ones_row = jnp.where(lax.broadcasted_iota(jnp.int32, (LANES - MLA_V, tm), 0) == 0, 1.0, 0.0)
    for hp in range(heads // 2):
        vt = kvn[:, kw + hp * LANES:kw + (hp + 1) * LANES].T
        for e in range(2):
            v_ref[0, 2 * hp + e, 0] = jnp.concatenate(
                [vt[e * MLA_V:(e + 1) * MLA_V], ones_row], axis=0).astype(BF16)


def _mla_attn_kernel(q_ref, k_ref, vt_ref, g_ref, h_ref, wout_ref, fg_ref, o_ref, m_sc, acc_sc, obuf,
                     *, tq, final):
    heads = MLA_HEADS
    j = pl.program_id(1)
    key_idx = lax.broadcasted_iota(jnp.int32, (tq, tq), 0)
    qry_idx = lax.broadcasted_iota(jnp.int32, (tq, tq), 1)
    diag = key_idx <= qry_idx
    m_sc[...] = jnp.full(m_sc.shape, -jnp.inf, F32)
    acc_sc[...] = jnp.zeros(acc_sc.shape, F32)

    def kv_tile(kt, masked):
        k0 = pl.multiple_of(kt * tq, tq)

        def scores(hh):
            return _dot_nt(k_ref[0, hh, pl.ds(k0, tq), :], q_ref[0, hh])

        ready = [scores(hh) for hh in range(MLA_LOOKAHEAD)]
        for hh in range(heads):
            if hh % MLA_LOOKAHEAD == 0:
                nxt = range(hh + MLA_LOOKAHEAD, min(hh + 2 * MLA_LOOKAHEAD, heads))
                ready.extend(scores(h2) for h2 in nxt)
            st = ready.pop(0)
            if masked:
                st = jnp.where(diag, st, -jnp.inf)
            m_old = m_sc[hh]
            m_new = jnp.maximum(m_old, jnp.max(st, axis=0, keepdims=True))
            pt = jnp.exp2(st - m_new).astype(BF16)
            acc_sc[hh] = jnp.exp2(m_old - m_new) * acc_sc[hh] + _dot(vt_ref[0, hh, kt], pt)
            m_sc[hh] = m_new

    def full_tile(kt, carry):
        kv_tile(kt, False)
        return carry

    lax.fori_loop(0, j, full_tile, 0)
    kv_tile(j, True)

    for hp in range(heads // 2):
        halves = []
        for e in range(2):
            acc = acc_sc[2 * hp + e]
            halves.append(acc[:MLA_V] * (1.0 / acc[MLA_V:MLA_V + 1]))
        obuf[:, hp * LANES:(hp + 1) * LANES] = jnp.concatenate(halves, axis=0).T
    y = (obuf[...] * g_ref[0].astype(F32)).astype(BF16)
    o_ref[0] = _residual(h_ref[0], _dot(y, wout_ref[...]), fg_ref, final)


def _mla_layer(h, positions, ng, fg, w_in, g_q, w_uq, g_kv, w_ukv, w_out, *, final, tq=256):
    B, S, D = h.shape
    heads, half = MLA_HEADS, MLA_ROPE // 2
    qk = MLA_NOPE + MLA_ROPE
    width = heads * MLA_V
    c0 = MLA_Q_RANK + MLA_KV_RANK
    w_kr = jnp.pad(w_in[:, c0:c0 + MLA_ROPE], ((0, 0), (MLA_NOPE, LANES - qk)))
    win = jnp.concatenate([w_in[:, :c0], w_in[:, c0 + MLA_ROPE:], w_kr], axis=1).astype(BF16)
    wuq = jnp.pad(w_uq.reshape(MLA_Q_RANK, heads, qk), ((0, 0), (0, 0), (0, LANES - qk)))
    wuq = wuq.reshape(MLA_Q_RANK, heads * LANES).astype(BF16)
    wkv = w_ukv.reshape(MLA_KV_RANK, heads, MLA_NOPE + MLA_V)
    wk = jnp.pad(wkv[:, :, :MLA_NOPE], ((0, 0), (0, 0), (0, LANES - MLA_NOPE)))
    wukv = jnp.concatenate([wk.reshape(MLA_KV_RANK, heads * LANES),
                            wkv[:, :, MLA_NOPE:].reshape(MLA_KV_RANK, width)], axis=1).astype(BF16)
    inv_freq = ROPE_THETA ** (-jnp.arange(0, MLA_ROPE, 2, dtype=F32) / MLA_ROPE)
    zeros = lambda n: jnp.zeros((n,), F32)
    invf = jnp.concatenate([zeros(MLA_NOPE), inv_freq, inv_freq, zeros(LANES - qk)]).reshape(1, LANES)
    sgn = jnp.concatenate([jnp.ones((MLA_NOPE,), F32), -jnp.ones((half,), F32),
                           jnp.ones((LANES - MLA_NOPE - half,), F32)]).reshape(1, LANES)
    row = lambda v: v.reshape(1, -1).astype(F32)

    tm = tq
    head_spec = pl.BlockSpec((1, heads, tm, LANES), lambda b, j: (b, 0, j, 0))
    vt_spec = pl.BlockSpec((1, heads, 1, LANES, tm), lambda b, j: (b, 0, j, 0, 0))
    q, k, vt, g = pl.pallas_call(
        functools.partial(_mla_proj_kernel, tm=tm),
        grid=(B, S // tm),
        in_specs=[_seq_spec(tm, D), _seq_spec(tm, 1), _const_spec((1, D)), _const_spec(win.shape),
                  _const_spec((1, MLA_Q_RANK)), _const_spec(wuq.shape), _const_spec((1, MLA_KV_RANK)),
                  _const_spec(wukv.shape), _const_spec((1, LANES)), _const_spec((1, LANES))],
        out_specs=[head_spec, head_spec, vt_spec, _seq_spec(tm, width)],
        out_shape=[jax.ShapeDtypeStruct((B, heads, S, LANES), BF16),
                   jax.ShapeDtypeStruct((B, heads, S, LANES), BF16),
                   jax.ShapeDtypeStruct((B, heads, S // tm, LANES, tm), BF16),
                   jax.ShapeDtypeStruct((B, S, width), BF16)],
        compiler_params=pltpu.CompilerParams(dimension_semantics=("arbitrary", "arbitrary"),
                                             vmem_limit_bytes=VMEM_LIMIT),
        name="mla_proj",
    )(h, positions.reshape(B, S, 1), row(ng), win, row(g_q), wuq, row(g_kv), wukv, invf, sgn)

    return pl.pallas_call(
        functools.partial(_mla_attn_kernel, tq=tq, final=final),
        grid=(B, S // tq),
        in_specs=[head_spec, pl.BlockSpec((1, heads, S, LANES), lambda b, j: (b, 0, 0, 0)),
                  pl.BlockSpec((1, heads, S // tm, LANES, tm), lambda b, j: (b, 0, 0, 0, 0)),
                  _seq_spec(tq, width), _seq_spec(tq, D), _const_spec((width, D)), _const_spec((1, D))],
        out_specs=_seq_spec(tq, D),
        out_shape=jax.ShapeDtypeStruct((B, S, D), F32),
        scratch_shapes=[pltpu.VMEM((heads, 1, tq), F32), pltpu.VMEM((heads, LANES, tq), F32),
                        pltpu.VMEM((tq, width), F32)],
        compiler_params=pltpu.CompilerParams(dimension_semantics=("arbitrary", "arbitrary"),
                                             vmem_limit_bytes=VMEM_LIMIT),
        name="mla_attn",
    )(q, k, vt, g, h, w_out.astype(BF16), row(fg))


def kernel(x, positions, norm_g, final_g, mla_w_in, mla_g_q, mla_w_uq, mla_g_kv, mla_w_ukv, mla_w_out, gla_w_in, gla_w_gk2, gla_b_gk, gla_g_o, gla_w_out, lru_w_in, lru_conv_w, lru_conv_b, lru_w_a, lru_b_a, lru_w_x, lru_b_x, lru_lam, lru_w_out, ssd_w_in, ssd_conv_w, ssd_conv_b, ssd_dt_bias, ssd_a_log, ssd_d, ssd_g_norm, ssd_w_out):
    depth = norm_g.shape[0]
    h = x
    for i in range(depth):
        m, j = i % N_MIXERS, i // N_MIXERS
        common = dict(final=(i == depth - 1))
        if m == 0:
            h = _mla_layer(h, positions, norm_g[i], final_g, mla_w_in[j], mla_g_q[j], mla_w_uq[j],
                           mla_g_kv[j], mla_w_ukv[j], mla_w_out[j], **common)
        elif m == 1:
            h = _gla_layer(h, norm_g[i], final_g, gla_w_in[j], gla_w_gk2[j], gla_b_gk[j], gla_g_o[j],
                           gla_w_out[j], **common)
        elif m == 2:
            h = _lru_layer(h, norm_g[i], final_g, lru_w_in[j], lru_conv_w[j], lru_conv_b[j], lru_w_a[j],
                           lru_b_a[j], lru_w_x[j], lru_b_x[j], lru_lam[j], lru_w_out[j], **common)
        else:
            h = _ssd_layer(h, norm_g[i], final_g, ssd_w_in[j], ssd_conv_w[j], ssd_conv_b[j],
                           ssd_dt_bias[j], ssd_a_log[j], ssd_d[j], ssd_g_norm[j], ssd_w_out[j], **common)
    return h
```

```python
import functools

import jax
import jax.numpy as jnp
from jax import lax
from jax.experimental import pallas as pl
from jax.experimental.pallas import tpu as pltpu

F32 = jnp.float32
BF16 = jnp.bfloat16

NORM_EPS = 1e-6
N_MIXERS = 4
MLA_HEADS = 16
MLA_Q_RANK = 384
MLA_KV_RANK = 256
MLA_NOPE = 64
MLA_ROPE = 32
MLA_V = 64
MLA_LOOKAHEAD = 4
ROPE_THETA = 10000.0
GLA_HEADS = 4
GLA_GATE_RANK = 16
GLA_TAU = 16.0
GLA_CHUNK = 64
LRU_BLOCKS = 10
LRU_C = 8.0
CONV_W = 4
SSD_HEADDIM = 64
SSD_GROUPS = 8
SSD_STATE = 128
SSD_CHUNK = 64
SSD_CONV_BLOCK = 1024

LOG2_E = 1.4426950408889634
LANES = 128
SUBLANES = 8
VMEM_LIMIT = 56 * 1024 * 1024


def _dot(a, b):
    return jnp.dot(a, b, preferred_element_type=F32)


def _dot_nt(a, b):
    return lax.dot_general(a, b, (((1,), (1,)), ((), ())), preferred_element_type=F32)


def _dot_tn(a, b):
    return lax.dot_general(a, b, (((0,), (0,)), ((), ())), preferred_element_type=F32)


def _rms(x, g):
    ms = jnp.mean(x * x, axis=-1, keepdims=True)
    return x * lax.rsqrt(ms + NORM_EPS) * g


def _softplus(x):
    return jnp.maximum(x, 0.0) + jnp.log1p(jnp.exp(-jnp.abs(x)))


def _sigmoid(x):
    return 0.5 * jnp.tanh(0.5 * x) + 0.5


def _silu(x):
    hx = 0.5 * x
    return hx * jnp.tanh(hx) + hx


def _residual(x, y, fg_ref, final):
    r = x + y
    if final:
        r = _rms(r, fg_ref[...])
    return r


def _chunk_tri(n, chunk):
    r = lax.broadcasted_iota(jnp.int32, (n, n), 0)
    c = lax.broadcasted_iota(jnp.int32, (n, n), 1)
    keep = jnp.logical_and(r // chunk == c // chunk, c <= r)
    return jnp.where(keep, 1.0, 0.0).astype(BF16)


def _split_bf16(x, parts):
    out = []
    for _ in range(parts - 1):
        hi = x.astype(BF16)
        out.append(hi)
        x = x - hi.astype(F32)
    out.append(x.astype(BF16))
    return out


def _const_spec(shape):
    n = len(shape)
    return pl.BlockSpec(shape, lambda *_: (0,) * n, pipeline_mode=pl.Buffered(1))


def _seq_spec(tm, d):
    return pl.BlockSpec((None, tm, d), lambda b, j: (b, j, 0))


def _h_spec(tm, d, layout):
    if layout == "bsd":
        return _seq_spec(tm, d)
    return pl.BlockSpec((tm, d), lambda b, j: (j, b))


def _h_struct(B, S, D, layout):
    return jax.ShapeDtypeStruct((B, S, D) if layout == "bsd" else (S, B * D), F32)


def _lru_kernel(h_ref, ng_ref, win_ref, cw_ref, cb_ref, wax_ref, ba_ref, bx_ref, lam_ref, wout_ref,
                fg_ref, o_ref, ubuf, abuf, bbuf, gbuf, hst, *, tm, nb, width, final):
    hist = (CONV_W - 1) * nb

    @pl.when(pl.program_id(0) == 0)
    def _():
        ubuf[0:hist, :] = jnp.zeros((hist, width), F32)
        hst[...] = jnp.zeros_like(hst)

    x = h_ref[...]
    un = _rms(x, ng_ref[...]).astype(BF16)
    ubuf[hist:hist + tm, :] = _dot(un, win_ref[:, width:])
    gbuf[...] = _silu(_dot(un, win_ref[:, :width]))
    cw = cw_ref[...]
    conv = cb_ref[...] + cw[CONV_W - 1:CONV_W] * ubuf[hist:hist + tm, :]
    for k in range(CONV_W - 1):
        d = (CONV_W - 1 - k) * nb
        conv = conv + cw[k:k + 1] * ubuf[hist - d:hist - d + tm, :]
    ubuf[0:hist, :] = ubuf[tm:tm + hist, :]

    sp = _softplus(-lam_ref[...])
    for n in range(width // LANES):
        sl = slice(n * LANES, (n + 1) * LANES)
        cn = conv[:, sl]
        ra = _dot(cn.astype(BF16), wax_ref[n])
        r = _sigmoid(ra[:, :LANES] + ba_ref[:, sl])
        i = _sigmoid(ra[:, LANES:] + bx_ref[:, sl])
        log_a = -LRU_C * r * sp[:, sl]
        th = jnp.tanh(log_a)
        u = -2.0 * th
        mult = jnp.where(u > 0.0, u * lax.rsqrt(u * (1.0 - th)), 0.0)
        abuf[:, sl] = jnp.exp(log_a)
        bbuf[:, sl] = mult * (i * cn)

    hs = hst[...]
    for t in range(tm // nb):
        rs = slice(t * nb, (t + 1) * nb)
        hs = abuf[rs, :] * hs + bbuf[rs, :]
        bbuf[rs, :] = hs
    hst[...] = hs
    y = (bbuf[...] * gbuf[...]).astype(BF16)
    o_ref[...] = _residual(x, _dot(y, wout_ref[...]), fg_ref, final)


def _lru_layer(h, ng, fg, w_in, conv_w, conv_b, w_a, b_a, w_x, b_x, lam, w_out, *, final, shape, layouts,
               tm=256):
    B, S, D = shape
    assert B == SUBLANES, "time-major RG-LRU tiling needs one sublane tile per time step"
    if layouts[0] == "bsd":
        h = h.transpose(1, 0, 2)
    h = h.reshape(S * B, D)
    width = w_out.shape[0]
    wax = jnp.concatenate([w_a, w_x], axis=-1).astype(BF16)
    row = lambda v: v.reshape(1, -1).astype(F32)
    kern = functools.partial(_lru_kernel, tm=tm, nb=B, width=width, final=final)
    hist = (CONV_W - 1) * B
    tile = pl.BlockSpec((tm, D), lambda j: (j, 0))
    out = pl.pallas_call(
        kern,
        grid=(S * B // tm,),
        in_specs=[tile, _const_spec((1, D)), _const_spec((D, 2 * width)),
                  _const_spec((CONV_W, width)), _const_spec((1, width)), _const_spec(wax.shape),
                  _const_spec((1, width)), _const_spec((1, width)), _const_spec((1, width)),
                  _const_spec((width, D)), _const_spec((1, D))],
        out_specs=tile,
        out_shape=jax.ShapeDtypeStruct((S * B, D), F32),
        scratch_shapes=[pltpu.VMEM((tm + hist, width), F32), pltpu.VMEM((tm, width), F32),
                        pltpu.VMEM((tm, width), F32), pltpu.VMEM((tm, width), F32),
                        pltpu.VMEM((B, width), F32)],
        compiler_params=pltpu.CompilerParams(dimension_semantics=("arbitrary",),
                                             vmem_limit_bytes=VMEM_LIMIT),
        name="lru_layer",
    )(h, row(ng), w_in.astype(BF16), conv_w.astype(F32), row(conv_b), wax, row(b_a), row(b_x),
      row(lam), w_out.astype(BF16), row(fg))
    if layouts[1] == "bsd":
        return out.reshape(S, B, D).transpose(1, 0, 2)
    return out.reshape(S, B * D)


def _gla_kernel(h_ref, ng_ref, wq_ref, wgk_ref, wgk2_ref, bgk_ref, go_ref, wout_ref, fg_ref, o_ref,
                sst, obuf, gbuf, *, tm, key, val, final):
    heads, chunk = GLA_HEADS, GLA_CHUNK
    dk, dv = key // heads, val // heads

    @pl.when(pl.program_id(1) == 0)
    def _():
        sst[...] = jnp.zeros_like(sst)

    x = h_ref[...]
    un = _rms(x, ng_ref[...]).astype(BF16)
    q = _dot(un, wq_ref[:, :key]) * dk ** -0.5
    k = _dot(un, wq_ref[:, key:2 * key])
    gk = _dot(un, wgk_ref[...]).astype(BF16)
    log_a = -_softplus(-(_dot(gk, wgk2_ref[...]) + bgk_ref[...])) * (1.0 / GLA_TAU)
    v = _dot(un, wq_ref[:, 2 * key:2 * key + val]).astype(BF16)
    gbuf[...] = _silu(_dot(un, wq_ref[:, 2 * key + val:]))
    tri = _chunk_tri(tm, chunk)
    b = sum(_dot(tri, part) for part in _split_bf16(log_a, 2))

    rr = lax.broadcasted_iota(jnp.int32, (chunk, chunk), 0)
    cc = lax.broadcasted_iota(jnp.int32, (chunk, chunk), 1)
    causal = cc <= rr
    for ci in range(tm // chunk):
        rs = slice(ci * chunk, (ci + 1) * chunk)
        bc = b[rs]
        bl = bc[chunk - 1:chunk, :]
        qt = (q[rs] * jnp.exp(bc)).astype(BF16)
        kt = (k[rs] * jnp.exp(-bc)).astype(BF16)
        ke = (k[rs] * jnp.exp(bl - bc)).astype(BF16)
        dec = jnp.exp(bl)
        ksl = [slice(hh * dk, (hh + 1) * dk) for hh in range(heads)]
        vsl = [slice(hh * dv, (hh + 1) * dv) for hh in range(heads)]
        s_prev = [sst[hh] for hh in range(heads)]
        att = [_dot_nt(qt[:, ks], kt[:, ks]) for ks in ksl]
        inter = [_dot_nt(qt[:, ks], s_prev[hh].astype(BF16)) for hh, ks in enumerate(ksl)]
        for hh in range(heads):
            p = jnp.where(causal, att[hh], 0.0).astype(BF16)
            obuf[rs, vsl[hh]] = _dot(p, v[rs, vsl[hh]]) + inter[hh]
        for hh in range(heads):
            sst[hh] = dec[:, ksl[hh]] * s_prev[hh] + _dot_tn(v[rs, vsl[hh]], ke[:, ksl[hh]])

    parts = []
    for hh in range(heads):
        oh = obuf[:, hh * dv:(hh + 1) * dv]
        parts.append(oh * lax.rsqrt(jnp.mean(oh * oh, axis=-1, keepdims=True) + NORM_EPS))
    on = jnp.concatenate(parts, axis=1) * go_ref[...]
    y = (on * gbuf[...]).astype(BF16)
    o_ref[...] = _residual(x, _dot(y, wout_ref[...]), fg_ref, final)


def _gla_layer(h, ng, fg, w_in, w_gk2, b_gk, g_o, w_out, *, final, shape, layouts, tm=256):
    B, S, D = shape
    val = w_out.shape[0]
    key = (w_in.shape[1] - 2 * val - GLA_GATE_RANK) // 2
    main = 2 * key + 2 * val
    pad = LANES - GLA_GATE_RANK
    wq = w_in[:, :main].astype(BF16)
    wgk = jnp.pad(w_in[:, main:], ((0, 0), (0, pad))).astype(BF16)
    wgk2 = jnp.pad(w_gk2, ((0, pad), (0, 0))).astype(BF16)
    row = lambda v: v.reshape(1, -1).astype(F32)
    kern = functools.partial(_gla_kernel, tm=tm, key=key, val=val, final=final)
    return pl.pallas_call(
        kern,
        grid=(B, S // tm),
        in_specs=[_h_spec(tm, D, layouts[0]), _const_spec((1, D)), _const_spec((D, main)), _const_spec((D, LANES)),
                  _const_spec((LANES, key)), _const_spec((1, key)), _const_spec((1, val)),
                  _const_spec((val, D)), _const_spec((1, D))],
        out_specs=_h_spec(tm, D, layouts[1]),
        out_shape=_h_struct(B, S, D, layouts[1]),
        scratch_shapes=[pltpu.VMEM((GLA_HEADS, val // GLA_HEADS, key // GLA_HEADS), F32),
                        pltpu.VMEM((tm, val), F32), pltpu.VMEM((tm, val), F32)],
        compiler_params=pltpu.CompilerParams(dimension_semantics=("arbitrary", "arbitrary"),
                                             vmem_limit_bytes=VMEM_LIMIT),
        name="gla_layer",
    )(h, row(ng), wq, wgk, wgk2, row(b_gk), row(jnp.tile(g_o, GLA_HEADS)), w_out.astype(BF16), row(fg))


def _ssd_kernel(h_ref, ng_ref, win_ref, cw_ref, cb_ref, dtb_ref, alog_ref, dexp_ref, gn_ref, e_ref,
                wout_ref, fg_ref, o_ref, xbuf, cbuf, csx, dtx, cst, dtt, zbuf, ybuf, sst, *, tm, inner, final):
    groups, ns, chunk = SSD_GROUPS, SSD_STATE, SSD_CHUNK
    gw = inner // groups
    conv_dim = inner + 2 * groups * ns
    half = SSD_HEADDIM

    @pl.when(pl.program_id(1) == 0)
    def _():
        xbuf[0:SUBLANES, :] = jnp.zeros((SUBLANES, conv_dim), F32)
        sst[...] = jnp.zeros_like(sst)

    x = h_ref[...]
    un = _rms(x, ng_ref[...]).astype(BF16)

    dt_raw = _dot(un, win_ref[:, inner + conv_dim:])
    dt = _softplus(dt_raw + dtb_ref[...])
    da = dt * (-jnp.exp(alog_ref[...]))
    tri = _chunk_tri(tm, chunk)
    cs = sum(_dot(tri, part) for part in _split_bf16(da, 3))
    e = e_ref[...]
    csx[...] = sum(_dot(part, e) for part in _split_bf16(cs, 2))
    dtx[...] = _dot(dt.astype(BF16), e)
    cst[...] = cs.T
    dtt[...] = dt.T

    for c0 in range(0, conv_dim, SSD_CONV_BLOCK):
        cols = slice(c0, c0 + SSD_CONV_BLOCK)
        xbuf[SUBLANES:SUBLANES + tm, cols] = _dot(un, win_ref[:, inner + c0:inner + c0 + SSD_CONV_BLOCK])
        conv = cb_ref[:, cols] + cw_ref[CONV_W - 1:CONV_W, cols] * xbuf[SUBLANES:SUBLANES + tm, cols]
        for k in range(CONV_W - 1):
            d = CONV_W - 1 - k
            conv = conv + cw_ref[k:k + 1, cols] * xbuf[SUBLANES - d:SUBLANES - d + tm, cols]
        xbuf[0:SUBLANES, cols] = xbuf[tm:tm + SUBLANES, cols]
        cbuf[:, cols] = _silu(conv)
    zbuf[...] = _silu(_dot(un, win_ref[:, 0:inner]))

    ii = lax.broadcasted_iota(jnp.int32, (chunk, LANES), 0)
    ll = lax.broadcasted_iota(jnp.int32, (chunk, LANES), 1)
    causal2 = ll % half <= ii
    low = ll < half
    low_row = lax.broadcasted_iota(jnp.int32, (1, LANES), 1) < half
    dexp = dexp_ref[...]

    def lane_pair(ref, h0, ci):
        t0 = (ci * chunk // LANES) * LANES
        a = ref[h0:h0 + 1, t0:t0 + LANES]
        b = ref[h0 + 1:h0 + 2, t0:t0 + LANES]
        if (ci * chunk) % LANES == 0:
            return jnp.where(low_row, a, pltpu.roll(b, half, 1))
        return jnp.where(low_row, pltpu.roll(a, half, 1), b)

    for ci in range(tm // chunk):
        rs = slice(ci * chunk, (ci + 1) * chunk)
        cs_last = csx[(ci + 1) * chunk - 1:(ci + 1) * chunk, :]
        bgs, cgs, cb2s, y_offs, s_prevs = [], [], [], [], []
        for g in range(groups):
            bg = cbuf[rs, inner + g * ns:inner + (g + 1) * ns].astype(BF16)
            cg = cbuf[rs, inner + (groups + g) * ns:inner + (groups + g + 1) * ns].astype(BF16)
            s_prev = sst[g]
            bgs.append(bg)
            cgs.append(cg)
            s_prevs.append(s_prev)
            cb2s.append(_dot_nt(cg, jnp.concatenate([bg, bg], axis=0)))
            y_offs.append(_dot(cg, s_prev.astype(BF16)))
        for g in range(groups):
            gl = slice(g * gw, (g + 1) * gw)
            xg = cbuf[rs, gl]
            csg = csx[rs, gl]
            yd = []
            for pp in range(gw // LANES):
                ps = slice(pp * LANES, (pp + 1) * LANES)
                h0 = (g * gw + pp * LANES) // SSD_HEADDIM
                row = lane_pair(cst, h0, ci)
                dtrow = lane_pair(dtt, h0, ci)
                lm = jnp.where(causal2, jnp.exp(jnp.minimum(csg[:, ps] - row, 0.0)), 0.0)
                w = (cb2s[g] * lm * dtrow).astype(BF16)
                xp = xg[:, ps]
                bd = jnp.concatenate([jnp.where(low, xp, 0.0), jnp.where(low, 0.0, xp)],
                                     axis=0).astype(BF16)
                yd.append(_dot(w, bd))
            ybuf[rs, gl] = jnp.concatenate(yd, axis=1) + y_offs[g] * jnp.exp(csg) + dexp[:, gl] * xg
        for g in range(groups):
            gl = slice(g * gw, (g + 1) * gw)
            cl = cs_last[:, gl]
            w1 = jnp.exp(cl - csx[rs, gl]) * dtx[rs, gl]
            sst[g] = s_prevs[g] * jnp.exp(cl) + _dot_tn(bgs[g], (cbuf[rs, gl] * w1).astype(BF16))

    acc = x
    for g in range(groups):
        gl = slice(g * gw, (g + 1) * gw)
        yg = ybuf[:, gl] * zbuf[:, gl]
        yn = yg * lax.rsqrt(jnp.mean(yg * yg, axis=-1, keepdims=True) + NORM_EPS) * gn_ref[:, gl]
        acc = acc + _dot(yn.astype(BF16), wout_ref[gl, :])
    if final:
        acc = _rms(acc, fg_ref[...])
    o_ref[...] = acc


def _ssd_layer(h, ng, fg, w_in, conv_w, conv_b, dt_bias, a_log, d_skip, g_norm, w_out, *, final, shape, layouts,
               tm=256):
    B, S, D = shape
    inner = w_out.shape[0]
    heads = inner // SSD_HEADDIM
    conv_dim = inner + 2 * SSD_GROUPS * SSD_STATE
    pad = LANES - heads
    win = jnp.pad(w_in, ((0, 0), (0, pad))).astype(BF16)
    row = lambda v: v.reshape(1, -1).astype(F32)
    padrow = lambda v: jnp.pad(v.astype(F32), (0, pad)).reshape(1, LANES)
    expand = (lax.broadcasted_iota(jnp.int32, (LANES, inner), 1) // SSD_HEADDIM
              == lax.broadcasted_iota(jnp.int32, (LANES, inner), 0)).astype(BF16)
    kern = functools.partial(_ssd_kernel, tm=tm, inner=inner, final=final)
    gw = inner // SSD_GROUPS
    return pl.pallas_call(
        kern,
        grid=(B, S // tm),
        in_specs=[_h_spec(tm, D, layouts[0]), _const_spec((1, D)), _const_spec(win.shape),
                  _const_spec((CONV_W, conv_dim)), _const_spec((1, conv_dim)), _const_spec((1, LANES)),
                  _const_spec((1, LANES)), _const_spec((1, inner)), _const_spec((1, inner)),
                  _const_spec((LANES, inner)), _const_spec((inner, D)), _const_spec((1, D))],
        out_specs=_h_spec(tm, D, layouts[1]),
        out_shape=_h_struct(B, S, D, layouts[1]),
        scratch_shapes=[pltpu.VMEM((tm + SUBLANES, conv_dim), F32), pltpu.VMEM((tm, conv_dim), F32),
                        pltpu.VMEM((tm, inner), F32), pltpu.VMEM((tm, inner), F32),
                        pltpu.VMEM((LANES, tm), F32), pltpu.VMEM((LANES, tm), F32),
                        pltpu.VMEM((tm, inner), F32), pltpu.VMEM((tm, inner), F32),
                        pltpu.VMEM((SSD_GROUPS, SSD_STATE, gw), F32)],
        compiler_params=pltpu.CompilerParams(dimension_semantics=("arbitrary", "arbitrary"),
                                             vmem_limit_bytes=VMEM_LIMIT),
        name="ssd_layer",
    )(h, row(ng), win, conv_w.astype(F32), row(conv_b), padrow(dt_bias), padrow(a_log),
      row(jnp.repeat(d_skip, SSD_HEADDIM)), row(g_norm), expand, w_out.astype(BF16), row(fg))


def _mla_proj_kernel(h_ref, pos_ref, ng_ref, win_ref, gq_ref, wuq_ref, gkv_ref, wukv_ref, invf_ref,
                     sgn_ref, q_ref, k_ref, v_ref, g_ref, *, tm):
    heads = MLA_HEADS
    kw = heads * LANES
    x = h_ref[...]
    un = _rms(x, ng_ref[...]).astype(BF16)
    t = _dot(un, win_ref[...])
    c_q = t[:, :MLA_Q_RANK]
    c_kv = t[:, MLA_Q_RANK:MLA_Q_RANK + MLA_KV_RANK]
    g0 = MLA_Q_RANK + MLA_KV_RANK
    gate = t[:, g0:g0 + heads * MLA_V]
    k_r = t[:, g0 + heads * MLA_V:]
    g_ref[...] = _silu(gate).astype(BF16)

    ang = pos_ref[...].astype(F32) * invf_ref[...]
    cos = jnp.cos(ang)
    sin = jnp.sin(ang) * sgn_ref[...]
    lane = lax.broadcasted_iota(jnp.int32, (tm, LANES), 1)
    first_half = lane < MLA_NOPE + MLA_ROPE // 2

    def rope(a):
        swapped = jnp.where(first_half, pltpu.roll(a, LANES - MLA_ROPE // 2, 1),
                            pltpu.roll(a, MLA_ROPE // 2, 1))
        return a * cos + swapped * sin

    k_rope = rope(k_r)
    qn = _dot(_rms(c_q, gq_ref[...]).astype(BF16), wuq_ref[...])
    kvn = _dot(_rms(c_kv, gkv_ref[...]).astype(BF16), wukv_ref[...])
    scale = (MLA_NOPE + MLA_ROPE) ** -0.5 * LOG2_E
    for hh in range(heads):
        hs = slice(hh * LANES, (hh + 1) * LANES)
        q_ref[0, hh] = (rope(qn[:, hs]) * scale).astype(BF16)
        k_ref[0, hh] = (kvn[:, hs] + k_rope).astype(BF16)
    ones_row = jnp.where(lax.broadcasted_iota(jnp.int32, (LANES - MLA_V, tm), 0) == 0, 1.0, 0.0)
    for hp in range(heads // 2):
        vt = kvn[:, kw + hp * LANES:kw + (hp + 1) * LANES].T
        for e in range(2):
            v_ref[0, 2 * hp + e, 0] = jnp.concatenate(
                [vt[e * MLA_V:(e + 1) * MLA_V], ones_row], axis=0).astype(BF16)


def _mla_attn_kernel(q_ref, k_ref, vt_ref, g_ref, h_ref, wout_ref, fg_ref, o_ref, m_sc, acc_sc, obuf,
                     *, tq, final):
    heads = MLA_HEADS
    j = pl.program_id(1)
    key_idx = lax.broadcasted_iota(jnp.int32, (tq, tq), 0)
    qry_idx = lax.broadcasted_iota(jnp.int32, (tq, tq), 1)
    diag = key_idx <= qry_idx
    m_sc[...] = jnp.full(m_sc.shape, -jnp.inf, F32)
    acc_sc[...] = jnp.zeros(acc_sc.shape, F32)

    def kv_tile(kt, masked):
        k0 = pl.multiple_of(kt * tq, tq)

        def scores(hh):
            return _dot_nt(k_ref[0, hh, pl.ds(k0, tq), :], q_ref[0, hh])

        ready = [scores(hh) for hh in range(MLA_LOOKAHEAD)]
        for hh in range(heads):
            if hh % MLA_LOOKAHEAD == 0:
                nxt = range(hh + MLA_LOOKAHEAD, min(hh + 2 * MLA_LOOKAHEAD, heads))
                ready.extend(scores(h2) for h2 in nxt)
            st = ready.pop(0)
            if masked:
                st = jnp.where(diag, st, -jnp.inf)
            m_old = m_sc[hh]
            m_new = jnp.maximum(m_old, jnp.max(st, axis=0, keepdims=True))
            pt = jnp.exp2(st - m_new).astype(BF16)
            acc_sc[hh] = jnp.exp2(m_old - m_new) * acc_sc[hh] + _dot(vt_ref[0, hh, kt], pt)
            m_sc[hh] = m_new

    def full_tile(kt, carry):
        kv_tile(kt, False)
        return carry

    lax.fori_loop(0, j, full_tile, 0)
    kv_tile(j, True)

    for hp in range(heads // 2):
        halves = []
        for e in range(2):
            acc = acc_sc[2 * hp + e]
            halves.append(acc[:MLA_V] * (1.0 / acc[MLA_V:MLA_V + 1]))
        obuf[:, hp * LANES:(hp + 1) * LANES] = jnp.concatenate(halves, axis=0).T
    y = (obuf[...] * g_ref[...].astype(F32)).astype(BF16)
    o_ref[...] = _residual(h_ref[...], _dot(y, wout_ref[...]), fg_ref, final)


def _mla_layer(h, positions, ng, fg, w_in, g_q, w_uq, g_kv, w_ukv, w_out, *, final, shape, layouts, tq=256):
    B, S, D = shape
    heads, half = MLA_HEADS, MLA_ROPE // 2
    qk = MLA_NOPE + MLA_ROPE
    width = heads * MLA_V
    c0 = MLA_Q_RANK + MLA_KV_RANK
    w_kr = jnp.pad(w_in[:, c0:c0 + MLA_ROPE], ((0, 0), (MLA_NOPE, LANES - qk)))
    win = jnp.concatenate([w_in[:, :c0], w_in[:, c0 + MLA_ROPE:], w_kr], axis=1).astype(BF16)
    wuq = jnp.pad(w_uq.reshape(MLA_Q_RANK, heads, qk), ((0, 0), (0, 0), (0, LANES - qk)))
    wuq = wuq.reshape(MLA_Q_RANK, heads * LANES).astype(BF16)
    wkv = w_ukv.reshape(MLA_KV_RANK, heads, MLA_NOPE + MLA_V)
    wk = jnp.pad(wkv[:, :, :MLA_NOPE], ((0, 0), (0, 0), (0, LANES - MLA_NOPE)))
    wukv = jnp.concatenate([wk.reshape(MLA_KV_RANK, heads * LANES),
                            wkv[:, :, MLA_NOPE:].reshape(MLA_KV_RANK, width)], axis=1).astype(BF16)
    inv_freq = ROPE_THETA ** (-jnp.arange(0, MLA_ROPE, 2, dtype=F32) / MLA_ROPE)
    zeros = lambda n: jnp.zeros((n,), F32)
    invf = jnp.concatenate([zeros(MLA_NOPE), inv_freq, inv_freq, zeros(LANES - qk)]).reshape(1, LANES)
    sgn = jnp.concatenate([jnp.ones((MLA_NOPE,), F32), -jnp.ones((half,), F32),
                           jnp.ones((LANES - MLA_NOPE - half,), F32)]).reshape(1, LANES)
    row = lambda v: v.reshape(1, -1).astype(F32)

    tm = tq
    head_spec = pl.BlockSpec((1, heads, tm, LANES), lambda b, j: (b, 0, j, 0))
    vt_spec = pl.BlockSpec((1, heads, 1, LANES, tm), lambda b, j: (b, 0, j, 0, 0))
    q, k, vt, g = pl.pallas_call(
        functools.partial(_mla_proj_kernel, tm=tm),
        grid=(B, S // tm),
        in_specs=[_h_spec(tm, D, layouts[0]), _seq_spec(tm, 1), _const_spec((1, D)), _const_spec(win.shape),
                  _const_spec((1, MLA_Q_RANK)), _const_spec(wuq.shape), _const_spec((1, MLA_KV_RANK)),
                  _const_spec(wukv.shape), _const_spec((1, LANES)), _const_spec((1, LANES))],
        out_specs=[head_spec, head_spec, vt_spec, _seq_spec(tm, width)],
        out_shape=[jax.ShapeDtypeStruct((B, heads, S, LANES), BF16),
                   jax.ShapeDtypeStruct((B, heads, S, LANES), BF16),
                   jax.ShapeDtypeStruct((B, heads, S // tm, LANES, tm), BF16),
                   jax.ShapeDtypeStruct((B, S, width), BF16)],
        compiler_params=pltpu.CompilerParams(dimension_semantics=("arbitrary", "arbitrary"),
                                             vmem_limit_bytes=VMEM_LIMIT),
        name="mla_proj",
    )(h, positions.reshape(B, S, 1), row(ng), win, row(g_q), wuq, row(g_kv), wukv, invf, sgn)

    return pl.pallas_call(
        functools.partial(_mla_attn_kernel, tq=tq, final=final),
        grid=(B, S // tq),
        in_specs=[head_spec, pl.BlockSpec((1, heads, S, LANES), lambda b, j: (b, 0, 0, 0)),
                  pl.BlockSpec((1, heads, S // tm, LANES, tm), lambda b, j: (b, 0, 0, 0, 0)),
                  _seq_spec(tq, width), _h_spec(tq, D, layouts[0]), _const_spec((width, D)), _const_spec((1, D))],
        out_specs=_h_spec(tq, D, layouts[1]),
        out_shape=_h_struct(B, S, D, layouts[1]),
        scratch_shapes=[pltpu.VMEM((heads, 1, tq), F32), pltpu.VMEM((heads, LANES, tq), F32),
                        pltpu.VMEM((tq, width), F32)],
        compiler_params=pltpu.CompilerParams(dimension_semantics=("arbitrary", "arbitrary"),
                                             vmem_limit_bytes=VMEM_LIMIT),
        name="mla_attn",
    )(q, k, vt, g, h, w_out.astype(BF16), row(fg))


def kernel(x, positions, norm_g, final_g, mla_w_in, mla_g_q, mla_w_uq, mla_g_kv, mla_w_ukv, mla_w_out, gla_w_in, gla_w_gk2, gla_b_gk, gla_g_o, gla_w_out, lru_w_in, lru_conv_w, lru_conv_b, lru_w_a, lru_b_a, lru_w_x, lru_b_x, lru_lam, lru_w_out, ssd_w_in, ssd_conv_w, ssd_conv_b, ssd_dt_bias, ssd_a_log, ssd_d, ssd_g_norm, ssd_w_out):
    depth = norm_g.shape[0]
    mixers = [i % N_MIXERS for i in range(depth)]
    lay_in = ["sbd" if m == 2 or (i > 0 and mixers[i - 1] == 2) else "bsd" for i, m in enumerate(mixers)]
    lay_in[0] = "bsd"
    h = x
    for i in range(depth):
        m, j = mixers[i], i // N_MIXERS
        common = dict(final=(i == depth - 1), shape=x.shape,
                      layouts=(lay_in[i], lay_in[i + 1] if i + 1 < depth else "bsd"))
        if m == 0:
            h = _mla_layer(h, positions, norm_g[i], final_g, mla_w_in[j], mla_g_q[j], mla_w_uq[j],
                           mla_g_kv[j], mla_w_ukv[j], mla_w_out[j], **common)
        elif m == 1:
            h = _gla_layer(h, norm_g[i], final_g, gla_w_in[j], gla_w_gk2[j], gla_b_gk[j], gla_g_o[j],
                           gla_w_out[j], **common)
        elif m == 2:
            h = _lru_layer(h, norm_g[i], final_g, lru_w_in[j], lru_conv_w[j], lru_conv_b[j], lru_w_a[j],
                           lru_b_a[j], lru_w_x[j], lru_b_x[j], lru_lam[j], lru_w_out[j], **common)
        else:
            h = _ssd_layer(h, norm_g[i], final_g, ssd_w_in[j], ssd_conv_w[j], ssd_conv_b[j],
                           ssd_dt_bias[j], ssd_a_log[j], ssd_d[j], ssd_g_norm[j], ssd_w_out[j], **common)
    return h
```

```python
import functools

import jax
import jax.numpy as jnp
from jax import lax
from jax.experimental import pallas as pl
from jax.experimental.pallas import tpu as pltpu

F32 = jnp.float32
BF16 = jnp.bfloat16

NORM_EPS = 1e-6
N_MIXERS = 4
MLA_HEADS = 16
MLA_Q_RANK = 384
MLA_KV_RANK = 256
MLA_NOPE = 64
MLA_ROPE = 32
MLA_V = 64
MLA_LOOKAHEAD = 4
ROPE_THETA = 10000.0
GLA_HEADS = 4
GLA_GATE_RANK = 16
GLA_TAU = 16.0
GLA_CHUNK = 64
LRU_BLOCKS = 10
LRU_C = 8.0
CONV_W = 4
SSD_HEADDIM = 64
SSD_GROUPS = 8
SSD_STATE = 128
SSD_CHUNK = 64
SSD_CONV_BLOCK = 1024

LOG2_E = 1.4426950408889634
LANES = 128
SUBLANES = 8
VMEM_LIMIT = 56 * 1024 * 1024


def _dot(a, b):
    return jnp.dot(a, b, preferred_element_type=F32)


def _dot_nt(a, b):
    return lax.dot_general(a, b, (((1,), (1,)), ((), ())), preferred_element_type=F32)


def _dot_tn(a, b):
    return lax.dot_general(a, b, (((0,), (0,)), ((), ())), preferred_element_type=F32)


def _rms(x, g):
    ms = jnp.mean(x * x, axis=-1, keepdims=True)
    return x * lax.rsqrt(ms + NORM_EPS) * g


def _softplus(x):
    return jnp.maximum(x, 0.0) + jnp.log1p(jnp.exp(-jnp.abs(x)))


def _sigmoid(x):
    return 0.5 * jnp.tanh(0.5 * x) + 0.5


def _silu(x):
    hx = 0.5 * x
    return hx * jnp.tanh(hx) + hx


def _residual(x, y, fg_ref, final):
    r = x + y
    if final:
        r = _rms(r, fg_ref[...])
    return r


def _chunk_tri(n, chunk):
    r = lax.broadcasted_iota(jnp.int32, (n, n), 0)
    c = lax.broadcasted_iota(jnp.int32, (n, n), 1)
    keep = jnp.logical_and(r // chunk == c // chunk, c <= r)
    return jnp.where(keep, 1.0, 0.0).astype(BF16)


def _split_bf16(x, parts):
    out = []
    for _ in range(parts - 1):
        hi = x.astype(BF16)
        out.append(hi)
        x = x - hi.astype(F32)
    out.append(x.astype(BF16))
    return out


def _const_spec(shape):
    n = len(shape)
    return pl.BlockSpec(shape, lambda *_: (0,) * n, pipeline_mode=pl.Buffered(1))


def _seq_spec(tm, d):
    return pl.BlockSpec((None, tm, d), lambda b, j: (b, j, 0))


def _lru_kernel(h_ref, ng_ref, win_ref, cw_ref, cb_ref, wax_ref, ba_ref, bx_ref, lam_ref, wout_ref,
                fg_ref, o_ref, ubuf, abuf, bbuf, gbuf, hst, *, tm, nb, width, final):
    hist = (CONV_W - 1) * nb
    tt = tm // nb

    @pl.when(pl.program_id(0) == 0)
    def _():
        ubuf[0:hist, :] = jnp.zeros((hist, width), F32)
        hst[...] = jnp.zeros_like(hst)

    x = pltpu.einshape("btd->tbd", h_ref[...]).reshape(tm, h_ref.shape[-1])
    un = _rms(x, ng_ref[...]).astype(BF16)
    ubuf[hist:hist + tm, :] = _dot(un, win_ref[:, width:])
    gbuf[...] = _silu(_dot(un, win_ref[:, :width]))
    cw = cw_ref[...]
    conv = cb_ref[...] + cw[CONV_W - 1:CONV_W] * ubuf[hist:hist + tm, :]
    for k in range(CONV_W - 1):
        d = (CONV_W - 1 - k) * nb
        conv = conv + cw[k:k + 1] * ubuf[hist - d:hist - d + tm, :]
    ubuf[0:hist, :] = ubuf[tm:tm + hist, :]

    sp = _softplus(-lam_ref[...])
    for n in range(width // LANES):
        sl = slice(n * LANES, (n + 1) * LANES)
        cn = conv[:, sl]
        ra = _dot(cn.astype(BF16), wax_ref[n])
        r = _sigmoid(ra[:, :LANES] + ba_ref[:, sl])
        i = _sigmoid(ra[:, LANES:] + bx_ref[:, sl])
        log_a = -LRU_C * r * sp[:, sl]
        th = jnp.tanh(log_a)
        u = -2.0 * th
        mult = jnp.where(u > 0.0, u * lax.rsqrt(u * (1.0 - th)), 0.0)
        abuf[:, sl] = jnp.exp(log_a)
        bbuf[:, sl] = mult * (i * cn)

    hs = hst[...]
    for t in range(tm // nb):
        rs = slice(t * nb, (t + 1) * nb)
        hs = abuf[rs, :] * hs + bbuf[rs, :]
        bbuf[rs, :] = hs
    hst[...] = hs
    y = (bbuf[...] * gbuf[...]).astype(BF16)
    out = _residual(x, _dot(y, wout_ref[...]), fg_ref, final)
    o_ref[...] = pltpu.einshape("tbd->btd", out.reshape(tt, nb, out.shape[-1]))


def _lru_layer(h, ng, fg, w_in, conv_w, conv_b, w_a, b_a, w_x, b_x, lam, w_out, *, final, tm=256):
    B, S, D = h.shape
    assert B == SUBLANES, "time-major RG-LRU tiling needs one sublane tile per time step"
    width = w_out.shape[0]
    wax = jnp.concatenate([w_a, w_x], axis=-1).astype(BF16)
    row = lambda v: v.reshape(1, -1).astype(F32)
    kern = functools.partial(_lru_kernel, tm=tm, nb=B, width=width, final=final)
    hist = (CONV_W - 1) * B
    tile = pl.BlockSpec((B, tm // B, D), lambda j: (0, j, 0))
    return pl.pallas_call(
        kern,
        grid=(S * B // tm,),
        in_specs=[tile, _const_spec((1, D)), _const_spec((D, 2 * width)),
                  _const_spec((CONV_W, width)), _const_spec((1, width)), _const_spec(wax.shape),
                  _const_spec((1, width)), _const_spec((1, width)), _const_spec((1, width)),
                  _const_spec((width, D)), _const_spec((1, D))],
        out_specs=tile,
        out_shape=jax.ShapeDtypeStruct((B, S, D), F32),
        scratch_shapes=[pltpu.VMEM((tm + hist, width), F32), pltpu.VMEM((tm, width), F32),
                        pltpu.VMEM((tm, width), F32), pltpu.VMEM((tm, width), F32),
                        pltpu.VMEM((B, width), F32)],
        compiler_params=pltpu.CompilerParams(dimension_semantics=("arbitrary",),
                                             vmem_limit_bytes=VMEM_LIMIT),
        name="lru_layer",
    )(h, row(ng), w_in.astype(BF16), conv_w.astype(F32), row(conv_b), wax, row(b_a), row(b_x),
      row(lam), w_out.astype(BF16), row(fg))


def _gla_kernel(h_ref, ng_ref, wq_ref, wgk_ref, wgk2_ref, bgk_ref, go_ref, wout_ref, fg_ref, o_ref,
                sst, obuf, gbuf, *, tm, key, val, final):
    heads, chunk = GLA_HEADS, GLA_CHUNK
    dk, dv = key // heads, val // heads

    @pl.when(pl.program_id(1) == 0)
    def _():
        sst[...] = jnp.zeros_like(sst)

    x = h_ref[...]
    un = _rms(x, ng_ref[...]).astype(BF16)
    q = _dot(un, wq_ref[:, :key]) * dk ** -0.5
    k = _dot(un, wq_ref[:, key:2 * key])
    gk = _dot(un, wgk_ref[...]).astype(BF16)
    log_a = -_softplus(-(_dot(gk, wgk2_ref[...]) + bgk_ref[...])) * (1.0 / GLA_TAU)
    v = _dot(un, wq_ref[:, 2 * key:2 * key + val]).astype(BF16)
    gbuf[...] = _silu(_dot(un, wq_ref[:, 2 * key + val:]))
    tri = _chunk_tri(tm, chunk)
    b = sum(_dot(tri, part) for part in _split_bf16(log_a, 2))

    rr = lax.broadcasted_iota(jnp.int32, (chunk, chunk), 0)
    cc = lax.broadcasted_iota(jnp.int32, (chunk, chunk), 1)
    causal = cc <= rr
    for ci in range(tm // chunk):
        rs = slice(ci * chunk, (ci + 1) * chunk)
        bc = b[rs]
        bl = bc[chunk - 1:chunk, :]
        qt = (q[rs] * jnp.exp(bc)).astype(BF16)
        kt = (k[rs] * jnp.exp(-bc)).astype(BF16)
        ke = (k[rs] * jnp.exp(bl - bc)).astype(BF16)
        dec = jnp.exp(bl)
        ksl = [slice(hh * dk, (hh + 1) * dk) for hh in range(heads)]
        vsl = [slice(hh * dv, (hh + 1) * dv) for hh in range(heads)]
        s_prev = [sst[hh] for hh in range(heads)]
        att = [_dot_nt(qt[:, ks], kt[:, ks]) for ks in ksl]
        inter = [_dot_nt(qt[:, ks], s_prev[hh].astype(BF16)) for hh, ks in enumerate(ksl)]
        for hh in range(heads):
            p = jnp.where(causal, att[hh], 0.0).astype(BF16)
            obuf[rs, vsl[hh]] = _dot(p, v[rs, vsl[hh]]) + inter[hh]
        for hh in range(heads):
            sst[hh] = dec[:, ksl[hh]] * s_prev[hh] + _dot_tn(v[rs, vsl[hh]], ke[:, ksl[hh]])

    parts = []
    for hh in range(heads):
        oh = obuf[:, hh * dv:(hh + 1) * dv]
        parts.append(oh * lax.rsqrt(jnp.mean(oh * oh, axis=-1, keepdims=True) + NORM_EPS))
    on = jnp.concatenate(parts, axis=1) * go_ref[...]
    y = (on * gbuf[...]).astype(BF16)
    o_ref[...] = _residual(x, _dot(y, wout_ref[...]), fg_ref, final)


def _gla_layer(h, ng, fg, w_in, w_gk2, b_gk, g_o, w_out, *, final, tm=256):
    B, S, D = h.shape
    val = w_out.shape[0]
    key = (w_in.shape[1] - 2 * val - GLA_GATE_RANK) // 2
    main = 2 * key + 2 * val
    pad = LANES - GLA_GATE_RANK
    wq = w_in[:, :main].astype(BF16)
    wgk = jnp.pad(w_in[:, main:], ((0, 0), (0, pad))).astype(BF16)
    wgk2 = jnp.pad(w_gk2, ((0, pad), (0, 0))).astype(BF16)
    row = lambda v: v.reshape(1, -1).astype(F32)
    kern = functools.partial(_gla_kernel, tm=tm, key=key, val=val, final=final)
    return pl.pallas_call(
        kern,
        grid=(B, S // tm),
        in_specs=[_seq_spec(tm, D), _const_spec((1, D)), _const_spec((D, main)), _const_spec((D, LANES)),
                  _const_spec((LANES, key)), _const_spec((1, key)), _const_spec((1, val)),
                  _const_spec((val, D)), _const_spec((1, D))],
        out_specs=_seq_spec(tm, D),
        out_shape=jax.ShapeDtypeStruct((B, S, D), F32),
        scratch_shapes=[pltpu.VMEM((GLA_HEADS, val // GLA_HEADS, key // GLA_HEADS), F32),
                        pltpu.VMEM((tm, val), F32), pltpu.VMEM((tm, val), F32)],
        compiler_params=pltpu.CompilerParams(dimension_semantics=("arbitrary", "arbitrary"),
                                             vmem_limit_bytes=VMEM_LIMIT),
        name="gla_layer",
    )(h, row(ng), wq, wgk, wgk2, row(b_gk), row(jnp.tile(g_o, GLA_HEADS)), w_out.astype(BF16), row(fg))


def _ssd_kernel(h_ref, ng_ref, win_ref, cw_ref, cb_ref, dtb_ref, alog_ref, dexp_ref, gn_ref, e_ref,
                wout_ref, fg_ref, o_ref, xbuf, cbuf, csx, dtx, cst, dtt, zbuf, ybuf, sst, *, tm, inner, final):
    groups, ns, chunk = SSD_GROUPS, SSD_STATE, SSD_CHUNK
    gw = inner // groups
    conv_dim = inner + 2 * groups * ns
    half = SSD_HEADDIM

    @pl.when(pl.program_id(1) == 0)
    def _():
        xbuf[0:SUBLANES, :] = jnp.zeros((SUBLANES, conv_dim), F32)
        sst[...] = jnp.zeros_like(sst)

    x = h_ref[...]
    un = _rms(x, ng_ref[...]).astype(BF16)

    dt_raw = _dot(un, win_ref[:, inner + conv_dim:])
    dt = _softplus(dt_raw + dtb_ref[...])
    da = dt * (-jnp.exp(alog_ref[...]))
    tri = _chunk_tri(tm, chunk)
    cs = sum(_dot(tri, part) for part in _split_bf16(da, 3))
    e = e_ref[...]
    csx[...] = sum(_dot(part, e) for part in _split_bf16(cs, 2))
    dtx[...] = _dot(dt.astype(BF16), e)
    cst[...] = cs.T
    dtt[...] = dt.T

    for c0 in range(0, conv_dim, SSD_CONV_BLOCK):
        cols = slice(c0, c0 + SSD_CONV_BLOCK)
        xbuf[SUBLANES:SUBLANES + tm, cols] = _dot(un, win_ref[:, inner + c0:inner + c0 + SSD_CONV_BLOCK])
        conv = cb_ref[:, cols] + cw_ref[CONV_W - 1:CONV_W, cols] * xbuf[SUBLANES:SUBLANES + tm, cols]
        for k in range(CONV_W - 1):
            d = CONV_W - 1 - k
            conv = conv + cw_ref[k:k + 1, cols] * xbuf[SUBLANES - d:SUBLANES - d + tm, cols]
        xbuf[0:SUBLANES, cols] = xbuf[tm:tm + SUBLANES, cols]
        cbuf[:, cols] = _silu(conv)
    zbuf[...] = _silu(_dot(un, win_ref[:, 0:inner]))

    ii = lax.broadcasted_iota(jnp.int32, (chunk, LANES), 0)
    ll = lax.broadcasted_iota(jnp.int32, (chunk, LANES), 1)
    causal2 = ll % half <= ii
    low = ll < half
    low_row = lax.broadcasted_iota(jnp.int32, (1, LANES), 1) < half
    dexp = dexp_ref[...]

    def lane_pair(ref, h0, ci):
        t0 = (ci * chunk // LANES) * LANES
        a = ref[h0:h0 + 1, t0:t0 + LANES]
        b = ref[h0 + 1:h0 + 2, t0:t0 + LANES]
        if (ci * chunk) % LANES == 0:
            return jnp.where(low_row, a, pltpu.roll(b, half, 1))
        return jnp.where(low_row, pltpu.roll(a, half, 1), b)

    for ci in range(tm // chunk):
        rs = slice(ci * chunk, (ci + 1) * chunk)
        cs_last = csx[(ci + 1) * chunk - 1:(ci + 1) * chunk, :]
        bgs, cgs, cb2s, y_offs, s_prevs = [], [], [], [], []
        for g in range(groups):
            bg = cbuf[rs, inner + g * ns:inner + (g + 1) * ns].astype(BF16)
            cg = cbuf[rs, inner + (groups + g) * ns:inner + (groups + g + 1) * ns].astype(BF16)
            s_prev = sst[g]
            bgs.append(bg)
            cgs.append(cg)
            s_prevs.append(s_prev)
            cb2s.append(_dot_nt(cg, jnp.concatenate([bg, bg], axis=0)))
            y_offs.append(_dot(cg, s_prev.astype(BF16)))
        for g in range(groups):
            gl = slice(g * gw, (g + 1) * gw)
            xg = cbuf[rs, gl]
            csg = csx[rs, gl]
            yd = []
            for pp in range(gw // LANES):
                ps = slice(pp * LANES, (pp + 1) * LANES)
                h0 = (g * gw + pp * LANES) // SSD_HEADDIM
                row = lane_pair(cst, h0, ci)
                dtrow = lane_pair(dtt, h0, ci)
                lm = jnp.where(causal2, jnp.exp(jnp.minimum(csg[:, ps] - row, 0.0)), 0.0)
                w = (cb2s[g] * lm * dtrow).astype(BF16)
                xp = xg[:, ps]
                bd = jnp.concatenate([jnp.where(low, xp, 0.0), jnp.where(low, 0.0, xp)],
                                     axis=0).astype(BF16)
                yd.append(_dot(w, bd))
            ybuf[rs, gl] = jnp.concatenate(yd, axis=1) + y_offs[g] * jnp.exp(csg) + dexp[:, gl] * xg
        for g in range(groups):
            gl = slice(g * gw, (g + 1) * gw)
            cl = cs_last[:, gl]
            w1 = jnp.exp(cl - csx[rs, gl]) * dtx[rs, gl]
            sst[g] = s_prevs[g] * jnp.exp(cl) + _dot_tn(bgs[g], (cbuf[rs, gl] * w1).astype(BF16))

    acc = x
    for g in range(groups):
        gl = slice(g * gw, (g + 1) * gw)
        yg = ybuf[:, gl] * zbuf[:, gl]
        yn = yg * lax.rsqrt(jnp.mean(yg * yg, axis=-1, keepdims=True) + NORM_EPS) * gn_ref[:, gl]
        acc = acc + _dot(yn.astype(BF16), wout_ref[gl, :])
    if final:
        acc = _rms(acc, fg_ref[...])
    o_ref[...] = acc


def _ssd_layer(h, ng, fg, w_in, conv_w, conv_b, dt_bias, a_log, d_skip, g_norm, w_out, *, final, tm=256):
    B, S, D = h.shape
    inner = w_out.shape[0]
    heads = inner // SSD_HEADDIM
    conv_dim = inner + 2 * SSD_GROUPS * SSD_STATE
    pad = LANES - heads
    win = jnp.pad(w_in, ((0, 0), (0, pad))).astype(BF16)
    row = lambda v: v.reshape(1, -1).astype(F32)
    padrow = lambda v: jnp.pad(v.astype(F32), (0, pad)).reshape(1, LANES)
    expand = (lax.broadcasted_iota(jnp.int32, (LANES, inner), 1) // SSD_HEADDIM
              == lax.broadcasted_iota(jnp.int32, (LANES, inner), 0)).astype(BF16)
    kern = functools.partial(_ssd_kernel, tm=tm, inner=inner, final=final)
    gw = inner // SSD_GROUPS
    return pl.pallas_call(
        kern,
        grid=(B, S // tm),
        in_specs=[_seq_spec(tm, D), _const_spec((1, D)), _const_spec(win.shape),
                  _const_spec((CONV_W, conv_dim)), _const_spec((1, conv_dim)), _const_spec((1, LANES)),
                  _const_spec((1, LANES)), _const_spec((1, inner)), _const_spec((1, inner)),
                  _const_spec((LANES, inner)), _const_spec((inner, D)), _const_spec((1, D))],
        out_specs=_seq_spec(tm, D),
        out_shape=jax.ShapeDtypeStruct((B, S, D), F32),
        scratch_shapes=[pltpu.VMEM((tm + SUBLANES, conv_dim), F32), pltpu.VMEM((tm, conv_dim), F32),
                        pltpu.VMEM((tm, inner), F32), pltpu.VMEM((tm, inner), F32),
                        pltpu.VMEM((LANES, tm), F32), pltpu.VMEM((LANES, tm), F32),
                        pltpu.VMEM((tm, inner), F32), pltpu.VMEM((tm, inner), F32),
                        pltpu.VMEM((SSD_GROUPS, SSD_STATE, gw), F32)],
        compiler_params=pltpu.CompilerParams(dimension_semantics=("arbitrary", "arbitrary"),
                                             vmem_limit_bytes=VMEM_LIMIT),
        name="ssd_layer",
    )(h, row(ng), win, conv_w.astype(F32), row(conv_b), padrow(dt_bias), padrow(a_log),
      row(jnp.repeat(d_skip, SSD_HEADDIM)), row(g_norm), expand, w_out.astype(BF16), row(fg))


def _mla_proj_kernel(h_ref, pos_ref, ng_ref, win_ref, gq_ref, wuq_ref, gkv_ref, wukv_ref, invf_ref,
                     sgn_ref, q_ref, k_ref, v_ref, g_ref, *, tm):
    heads = MLA_HEADS
    kw = heads * LANES
    x = h_ref[...]
    un = _rms(x, ng_ref[...]).astype(BF16)
    t = _dot(un, win_ref[...])
    c_q = t[:, :MLA_Q_RANK]
    c_kv = t[:, MLA_Q_RANK:MLA_Q_RANK + MLA_KV_RANK]
    g0 = MLA_Q_RANK + MLA_KV_RANK
    gate = t[:, g0:g0 + heads * MLA_V]
    k_r = t[:, g0 + heads * MLA_V:]
    g_ref[...] = _silu(gate).astype(BF16)

    ang = pos_ref[...].astype(F32) * invf_ref[...]
    cos = jnp.cos(ang)
    sin = jnp.sin(ang) * sgn_ref[...]
    lane = lax.broadcasted_iota(jnp.int32, (tm, LANES), 1)
    first_half = lane < MLA_NOPE + MLA_ROPE // 2

    def rope(a):
        swapped = jnp.where(first_half, pltpu.roll(a, LANES - MLA_ROPE // 2, 1),
                            pltpu.roll(a, MLA_ROPE // 2, 1))
        return a * cos + swapped * sin

    k_rope = rope(k_r)
    qn = _dot(_rms(c_q, gq_ref[...]).astype(BF16), wuq_ref[...])
    kvn = _dot(_rms(c_kv, gkv_ref[...]).astype(BF16), wukv_ref[...])
    scale = (MLA_NOPE + MLA_ROPE) ** -0.5 * LOG2_E
    for hh in range(heads):
        hs = slice(hh * LANES, (hh + 1) * LANES)
        q_ref[0, hh] = (rope(qn[:, hs]) * scale).astype(BF16)
        k_ref[0, hh] = (kvn[:, hs] + k_rope).astype(BF16)
    ones_row = jnp.where(lax.broadcasted_iota(jnp.int32, (LANES - MLA_V, tm), 0) == 0, 1.0, 0.0)
    for hp in range(heads // 2):
        vt = kvn[:, kw + hp * LANES:kw + (hp + 1) * LANES].T
        for e in range(2):
            v_ref[0, 2 * hp + e, 0] = jnp.concatenate(
                [vt[e * MLA_V:(e + 1) * MLA_V], ones_row], axis=0).astype(BF16)


def _mla_attn_kernel(q_ref, k_ref, vt_ref, g_ref, h_ref, wout_ref, fg_ref, o_ref, m_sc, acc_sc, obuf,
                     *, tq, final):
    heads = MLA_HEADS
    j = pl.program_id(1)
    key_idx = lax.broadcasted_iota(jnp.int32, (tq, tq), 0)
    qry_idx = lax.broadcasted_iota(jnp.int32, (tq, tq), 1)
    diag = key_idx <= qry_idx
    m_sc[...] = jnp.full(m_sc.shape, -jnp.inf, F32)
    acc_sc[...] = jnp.zeros(acc_sc.shape, F32)

    def kv_tile(kt, masked):
        k0 = pl.multiple_of(kt * tq, tq)

        def scores(hh):
            return _dot_nt(k_ref[0, hh, pl.ds(k0, tq), :], q_ref[0, hh])

        ready = [scores(hh) for hh in range(MLA_LOOKAHEAD)]
        for hh in range(heads):
            if hh % MLA_LOOKAHEAD == 0:
                nxt = range(hh + MLA_LOOKAHEAD, min(hh + 2 * MLA_LOOKAHEAD, heads))
                ready.extend(scores(h2) for h2 in nxt)
            st = ready.pop(0)
            if masked:
                st = jnp.where(diag, st, -jnp.inf)
            m_old = m_sc[hh]
            m_new = jnp.maximum(m_old, jnp.max(st, axis=0, keepdims=True))
            pt = jnp.exp2(st - m_new).astype(BF16)
            acc_sc[hh] = jnp.exp2(m_old - m_new) * acc_sc[hh] + _dot(vt_ref[0, hh, kt], pt)
            m_sc[hh] = m_new

    def full_tile(kt, carry):
        kv_tile(kt, False)
        return carry

    lax.fori_loop(0, j, full_tile, 0)
    kv_tile(j, True)

    for hp in range(heads // 2):
        halves = []
        for e in range(2):
            acc = acc_sc[2 * hp + e]
            halves.append(acc[:MLA_V] * (1.0 / acc[MLA_V:MLA_V + 1]))
        obuf[:, hp * LANES:(hp + 1) * LANES] = jnp.concatenate(halves, axis=0).T
    y = (obuf[...] * g_ref[...].astype(F32)).astype(BF16)
    o_ref[...] = _residual(h_ref[...], _dot(y, wout_ref[...]), fg_ref, final)


def _mla_layer(h, positions, ng, fg, w_in, g_q, w_uq, g_kv, w_ukv, w_out, *, final, tq=256):
    B, S, D = h.shape
    heads, half = MLA_HEADS, MLA_ROPE // 2
    qk = MLA_NOPE + MLA_ROPE
    width = heads * MLA_V
    c0 = MLA_Q_RANK + MLA_KV_RANK
    w_kr = jnp.pad(w_in[:, c0:c0 + MLA_ROPE], ((0, 0), (MLA_NOPE, LANES - qk)))
    win = jnp.concatenate([w_in[:, :c0], w_in[:, c0 + MLA_ROPE:], w_kr], axis=1).astype(BF16)
    wuq = jnp.pad(w_uq.reshape(MLA_Q_RANK, heads, qk), ((0, 0), (0, 0), (0, LANES - qk)))
    wuq = wuq.reshape(MLA_Q_RANK, heads * LANES).astype(BF16)
    wkv = w_ukv.reshape(MLA_KV_RANK, heads, MLA_NOPE + MLA_V)
    wk = jnp.pad(wkv[:, :, :MLA_NOPE], ((0, 0), (0, 0), (0, LANES - MLA_NOPE)))
    wukv = jnp.concatenate([wk.reshape(MLA_KV_RANK, heads * LANES),
                            wkv[:, :, MLA_NOPE:].reshape(MLA_KV_RANK, width)], axis=1).astype(BF16)
    inv_freq = ROPE_THETA ** (-jnp.arange(0, MLA_ROPE, 2, dtype=F32) / MLA_ROPE)
    zeros = lambda n: jnp.zeros((n,), F32)
    invf = jnp.concatenate([zeros(MLA_NOPE), inv_freq, inv_freq, zeros(LANES - qk)]).reshape(1, LANES)
    sgn = jnp.concatenate([jnp.ones((MLA_NOPE,), F32), -jnp.ones((half,), F32),
                           jnp.ones((LANES - MLA_NOPE - half,), F32)]).reshape(1, LANES)
    row = lambda v: v.reshape(1, -1).astype(F32)

    tm = tq
    head_spec = pl.BlockSpec((1, heads, tm, LANES), lambda b, j: (b, 0, j, 0))
    vt_spec = pl.BlockSpec((1, heads, 1, LANES, tm), lambda b, j: (b, 0, j, 0, 0))
    q, k, vt, g = pl.pallas_call(
        functools.partial(_mla_proj_kernel, tm=tm),
        grid=(B, S // tm),
        in_specs=[_seq_spec(tm, D), _seq_spec(tm, 1), _const_spec((1, D)), _const_spec(win.shape),
                  _const_spec((1, MLA_Q_RANK)), _const_spec(wuq.shape), _const_spec((1, MLA_KV_RANK)),
                  _const_spec(wukv.shape), _const_spec((1, LANES)), _const_spec((1, LANES))],
        out_specs=[head_spec, head_spec, vt_spec, _seq_spec(tm, width)],
        out_shape=[jax.ShapeDtypeStruct((B, heads, S, LANES), BF16),
                   jax.ShapeDtypeStruct((B, heads, S, LANES), BF16),
                   jax.ShapeDtypeStruct((B, heads, S // tm, LANES, tm), BF16),
                   jax.ShapeDtypeStruct((B, S, width), BF16)],
        compiler_params=pltpu.CompilerParams(dimension_semantics=("arbitrary", "arbitrary"),
                                             vmem_limit_bytes=VMEM_LIMIT),
        name="mla_proj",
    )(h, positions.reshape(B, S, 1), row(ng), win, row(g_q), wuq, row(g_kv), wukv, invf, sgn)

    return pl.pallas_call(
        functools.partial(_mla_attn_kernel, tq=tq, final=final),
        grid=(B, S // tq),
        in_specs=[head_spec, pl.BlockSpec((1, heads, S, LANES), lambda b, j: (b, 0, 0, 0)),
                  pl.BlockSpec((1, heads, S // tm, LANES, tm), lambda b, j: (b, 0, 0, 0, 0)),
                  _seq_spec(tq, width), _seq_spec(tq, D), _const_spec((width, D)), _const_spec((1, D))],
        out_specs=_seq_spec(tq, D),
        out_shape=jax.ShapeDtypeStruct((B, S, D), F32),
        scratch_shapes=[pltpu.VMEM((heads, 1, tq), F32), pltpu.VMEM((heads, LANES, tq), F32),
                        pltpu.VMEM((tq, width), F32)],
        compiler_params=pltpu.CompilerParams(dimension_semantics=("arbitrary", "arbitrary"),
                                             vmem_limit_bytes=VMEM_LIMIT),
        name="mla_attn",
    )(q, k, vt, g, h, w_out.astype(BF16), row(fg))


def kernel(x, positions, norm_g, final_g, mla_w_in, mla_g_q, mla_w_uq, mla_g_kv, mla_w_ukv, mla_w_out, gla_w_in, gla_w_gk2, gla_b_gk, gla_g_o, gla_w_out, lru_w_in, lru_conv_w, lru_conv_b, lru_w_a, lru_b_a, lru_w_x, lru_b_x, lru_lam, lru_w_out, ssd_w_in, ssd_conv_w, ssd_conv_b, ssd_dt_bias, ssd_a_log, ssd_d, ssd_g_norm, ssd_w_out):
    depth = norm_g.shape[0]
    h = x
    for i in range(depth):
        m, j = i % N_MIXERS, i // N_MIXERS
        common = dict(final=(i == depth - 1))
        if m == 0:
            h = _mla_layer(h, positions, norm_g[i], final_g, mla_w_in[j], mla_g_q[j], mla_w_uq[j],
                           mla_g_kv[j], mla_w_ukv[j], mla_w_out[j], **common)
        elif m == 1:
            h = _gla_layer(h, norm_g[i], final_g, gla_w_in[j], gla_w_gk2[j], gla_b_gk[j], gla_g_o[j],
                           gla_w_out[j], **common)
        elif m == 2:
            h = _lru_layer(h, norm_g[i], final_g, lru_w_in[j], lru_conv_w[j], lru_conv_b[j], lru_w_a[j],
                           lru_b_a[j], lru_w_x[j], lru_b_x[j], lru_lam[j], lru_w_out[j], **common)
        else:
            h = _ssd_layer(h, norm_g[i], final_g, ssd_w_in[j], ssd_conv_w[j], ssd_conv_b[j],
                           ssd_dt_bias[j], ssd_a_log[j], ssd_d[j], ssd_g_norm[j], ssd_w_out[j], **common)
    return h
```

```python
import functools

import jax
import jax.numpy as jnp
from jax import lax
from jax.experimental import pallas as pl
from jax.experimental.pallas import tpu as pltpu

F32 = jnp.float32
BF16 = jnp.bfloat16

NORM_EPS = 1e-6
N_MIXERS = 4
MLA_HEADS = 16
MLA_Q_RANK = 384
MLA_KV_RANK = 256
MLA_NOPE = 64
MLA_ROPE = 32
MLA_V = 64
MLA_LOOKAHEAD = 4
ROPE_THETA = 10000.0
GLA_HEADS = 4
GLA_GATE_RANK = 16
GLA_TAU = 16.0
GLA_CHUNK = 64
LRU_BLOCKS = 10
LRU_C = 8.0
CONV_W = 4
SSD_HEADDIM = 64
SSD_GROUPS = 8
SSD_STATE = 128
SSD_CHUNK = 64
SSD_CONV_BLOCK = 1024

LOG2_E = 1.4426950408889634
LANES = 128
SUBLANES = 8
VMEM_LIMIT = 56 * 1024 * 1024


def _dot(a, b):
    return jnp.dot(a, b, preferred_element_type=F32)


def _dot_nt(a, b):
    return lax.dot_general(a, b, (((1,), (1,)), ((), ())), preferred_element_type=F32)


def _dot_tn(a, b):
    return lax.dot_general(a, b, (((0,), (0,)), ((), ())), preferred_element_type=F32)


def _rms(x, g):
    ms = jnp.mean(x * x, axis=-1, keepdims=True)
    return x * lax.rsqrt(ms + NORM_EPS) * g


def _softplus(x):
    return jnp.maximum(x, 0.0) + jnp.log1p(jnp.exp(-jnp.abs(x)))


def _sigmoid(x):
    return 0.5 * jnp.tanh(0.5 * x) + 0.5


def _silu(x):
    hx = 0.5 * x
    return hx * jnp.tanh(hx) + hx


def _residual(x, y, fg_ref, final):
    r = x + y
    if final:
        r = _rms(r, fg_ref[...])
    return r


def _chunk_tri(n, chunk):
    r = lax.broadcasted_iota(jnp.int32, (n, n), 0)
    c = lax.broadcasted_iota(jnp.int32, (n, n), 1)
    keep = jnp.logical_and(r // chunk == c // chunk, c <= r)
    return jnp.where(keep, 1.0, 0.0).astype(BF16)


def _split_bf16(x, parts):
    out = []
    for _ in range(parts - 1):
        hi = x.astype(BF16)
        out.append(hi)
        x = x - hi.astype(F32)
    out.append(x.astype(BF16))
    return out


def _const_spec(shape):
    n = len(shape)
    return pl.BlockSpec(shape, lambda *_: (0,) * n, pipeline_mode=pl.Buffered(1))


def _seq_spec(tm, d):
    return pl.BlockSpec((None, tm, d), lambda b, j: (b, j, 0))


def _lru_kernel(h_ref, ng_ref, win_ref, cw_ref, cb_ref, wax_ref, ba_ref, bx_ref, lam_ref, wout_ref,
                fg_ref, o_ref, ubuf, abuf, bbuf, gbuf, hst, *, tm, nb, width, final):
    hist = (CONV_W - 1) * nb
    tt = tm // nb

    @pl.when(pl.program_id(0) == 0)
    def _():
        ubuf[0:hist, :] = jnp.zeros((hist, width), F32)
        hst[...] = jnp.zeros_like(hst)

    x = pltpu.einshape("btd->tbd", h_ref[...]).reshape(tm, h_ref.shape[-1])
    un = _rms(x, ng_ref[...]).astype(BF16)
    ubuf[hist:hist + tm, :] = _dot(un, win_ref[:, width:])
    gbuf[...] = _silu(_dot(un, win_ref[:, :width]))
    cw = cw_ref[...]
    conv = cb_ref[...] + cw[CONV_W - 1:CONV_W] * ubuf[hist:hist + tm, :]
    for k in range(CONV_W - 1):
        d = (CONV_W - 1 - k) * nb
        conv = conv + cw[k:k + 1] * ubuf[hist - d:hist - d + tm, :]
    ubuf[0:hist, :] = ubuf[tm:tm + hist, :]

    sp = _softplus(-lam_ref[...])
    for n in range(width // LANES):
        sl = slice(n * LANES, (n + 1) * LANES)
        cn = conv[:, sl]
        ra = _dot(cn.astype(BF16), wax_ref[n])
        r = _sigmoid(ra[:, :LANES] + ba_ref[:, sl])
        i = _sigmoid(ra[:, LANES:] + bx_ref[:, sl])
        log_a = -LRU_C * r * sp[:, sl]
        th = jnp.tanh(log_a)
        u = -2.0 * th
        mult = jnp.where(u > 0.0, u * lax.rsqrt(u * (1.0 - th)), 0.0)
        abuf[:, sl] = jnp.exp(log_a)
        bbuf[:, sl] = mult * (i * cn)

    hs = hst[...]
    for t in range(tm // nb):
        rs = slice(t * nb, (t + 1) * nb)
        hs = abuf[rs, :] * hs + bbuf[rs, :]
        bbuf[rs, :] = hs
    hst[...] = hs
    y = (bbuf[...] * gbuf[...]).astype(BF16)
    out = _residual(x, _dot(y, wout_ref[...]), fg_ref, final)
    o_ref[...] = pltpu.einshape("tbd->btd", out.reshape(tt, nb, out.shape[-1]))


def _lru_layer(h, ng, fg, w_in, conv_w, conv_b, w_a, b_a, w_x, b_x, lam, w_out, *, final, tm=512):
    B, S, D = h.shape
    assert B == SUBLANES, "time-major RG-LRU tiling needs one sublane tile per time step"
    width = w_out.shape[0]
    wax = jnp.concatenate([w_a, w_x], axis=-1).astype(BF16)
    row = lambda v: v.reshape(1, -1).astype(F32)
    kern = functools.partial(_lru_kernel, tm=tm, nb=B, width=width, final=final)
    hist = (CONV_W - 1) * B
    tile = pl.BlockSpec((B, tm // B, D), lambda j: (0, j, 0))
    return pl.pallas_call(
        kern,
        grid=(S * B // tm,),
        in_specs=[tile, _const_spec((1, D)), _const_spec((D, 2 * width)),
                  _const_spec((CONV_W, width)), _const_spec((1, width)), _const_spec(wax.shape),
                  _const_spec((1, width)), _const_spec((1, width)), _const_spec((1, width)),
                  _const_spec((width, D)), _const_spec((1, D))],
        out_specs=tile,
        out_shape=jax.ShapeDtypeStruct((B, S, D), F32),
        scratch_shapes=[pltpu.VMEM((tm + hist, width), F32), pltpu.VMEM((tm, width), F32),
                        pltpu.VMEM((tm, width), F32), pltpu.VMEM((tm, width), F32),
                        pltpu.VMEM((B, width), F32)],
        compiler_params=pltpu.CompilerParams(dimension_semantics=("arbitrary",),
                                             vmem_limit_bytes=VMEM_LIMIT),
        name="lru_layer",
    )(h, row(ng), w_in.astype(BF16), conv_w.astype(F32), row(conv_b), wax, row(b_a), row(b_x),
      row(lam), w_out.astype(BF16), row(fg))


def _gla_kernel(h_ref, ng_ref, wq_ref, wgk_ref, wgk2_ref, bgk_ref, go_ref, wout_ref, fg_ref, o_ref,
                sst, obuf, gbuf, *, tm, key, val, final):
    heads, chunk = GLA_HEADS, GLA_CHUNK
    dk, dv = key // heads, val // heads

    @pl.when(pl.program_id(1) == 0)
    def _():
        sst[...] = jnp.zeros_like(sst)

    x = h_ref[...]
    un = _rms(x, ng_ref[...]).astype(BF16)
    q = _dot(un, wq_ref[:, :key]) * dk ** -0.5
    k = _dot(un, wq_ref[:, key:2 * key])
    gk = _dot(un, wgk_ref[...]).astype(BF16)
    log_a = -_softplus(-(_dot(gk, wgk2_ref[...]) + bgk_ref[...])) * (1.0 / GLA_TAU)
    v = _dot(un, wq_ref[:, 2 * key:2 * key + val]).astype(BF16)
    gbuf[...] = _silu(_dot(un, wq_ref[:, 2 * key + val:]))
    tri = _chunk_tri(tm, chunk)
    b = sum(_dot(tri, part) for part in _split_bf16(log_a, 2))

    rr = lax.broadcasted_iota(jnp.int32, (chunk, chunk), 0)
    cc = lax.broadcasted_iota(jnp.int32, (chunk, chunk), 1)
    causal = cc <= rr
    nch = tm // chunk
    ksl = [slice(hh * dk, (hh + 1) * dk) for hh in range(heads)]
    vsl = [slice(hh * dv, (hh + 1) * dv) for hh in range(heads)]
    rows = [slice(ci * chunk, (ci + 1) * chunk) for ci in range(nch)]
    qt, dec, att, ds = [], [], [], []
    for rs in rows:
        bc = b[rs]
        bl = bc[chunk - 1:chunk, :]
        qt.append((q[rs] * jnp.exp(bc)).astype(BF16))
        kt = (k[rs] * jnp.exp(-bc)).astype(BF16)
        ke = (k[rs] * jnp.exp(bl - bc)).astype(BF16)
        dec.append(jnp.exp(bl))
        att.append([_dot_nt(qt[-1][:, ks], kt[:, ks]) for ks in ksl])
        ds.append([_dot_tn(v[rs, vsl[hh]], ke[:, ksl[hh]]) for hh in range(heads)])
    for hh in range(heads):
        s_in = sst[hh]
        for ci, rs in enumerate(rows):
            p = jnp.where(causal, att[ci][hh], 0.0).astype(BF16)
            obuf[rs, vsl[hh]] = _dot(p, v[rs, vsl[hh]]) + _dot_nt(qt[ci][:, ksl[hh]], s_in.astype(BF16))
            s_in = dec[ci][:, ksl[hh]] * s_in + ds[ci][hh]
        sst[hh] = s_in

    parts = []
    for hh in range(heads):
        oh = obuf[:, hh * dv:(hh + 1) * dv]
        parts.append(oh * lax.rsqrt(jnp.mean(oh * oh, axis=-1, keepdims=True) + NORM_EPS))
    on = jnp.concatenate(parts, axis=1) * go_ref[...]
    y = (on * gbuf[...]).astype(BF16)
    o_ref[...] = _residual(x, _dot(y, wout_ref[...]), fg_ref, final)


def _gla_layer(h, ng, fg, w_in, w_gk2, b_gk, g_o, w_out, *, final, tm=256):
    B, S, D = h.shape
    val = w_out.shape[0]
    key = (w_in.shape[1] - 2 * val - GLA_GATE_RANK) // 2
    main = 2 * key + 2 * val
    pad = LANES - GLA_GATE_RANK
    wq = w_in[:, :main].astype(BF16)
    wgk = jnp.pad(w_in[:, main:], ((0, 0), (0, pad))).astype(BF16)
    wgk2 = jnp.pad(w_gk2, ((0, pad), (0, 0))).astype(BF16)
    row = lambda v: v.reshape(1, -1).astype(F32)
    kern = functools.partial(_gla_kernel, tm=tm, key=key, val=val, final=final)
    return pl.pallas_call(
        kern,
        grid=(B, S // tm),
        in_specs=[_seq_spec(tm, D), _const_spec((1, D)), _const_spec((D, main)), _const_spec((D, LANES)),
                  _const_spec((LANES, key)), _const_spec((1, key)), _const_spec((1, val)),
                  _const_spec((val, D)), _const_spec((1, D))],
        out_specs=_seq_spec(tm, D),
        out_shape=jax.ShapeDtypeStruct((B, S, D), F32),
        scratch_shapes=[pltpu.VMEM((GLA_HEADS, val // GLA_HEADS, key // GLA_HEADS), F32),
                        pltpu.VMEM((tm, val), F32), pltpu.VMEM((tm, val), F32)],
        compiler_params=pltpu.CompilerParams(dimension_semantics=("arbitrary", "arbitrary"),
                                             vmem_limit_bytes=VMEM_LIMIT),
        name="gla_layer",
    )(h, row(ng), wq, wgk, wgk2, row(b_gk), row(jnp.tile(g_o, GLA_HEADS)), w_out.astype(BF16), row(fg))


def _ssd_kernel(h_ref, ng_ref, win_ref, cw_ref, cb_ref, dtb_ref, alog_ref, dexp_ref, gn_ref, e_ref,
                wout_ref, fg_ref, o_ref, xbuf, cbuf, csx, dtx, cst, dtt, zbuf, ybuf, sst, *, tm, inner, final):
    groups, ns, chunk = SSD_GROUPS, SSD_STATE, SSD_CHUNK
    gw = inner // groups
    conv_dim = inner + 2 * groups * ns
    half = SSD_HEADDIM

    @pl.when(pl.program_id(1) == 0)
    def _():
        xbuf[:, 0:SUBLANES, :] = jnp.zeros((conv_dim // LANES, SUBLANES, LANES), F32)
        sst[...] = jnp.zeros_like(sst)

    x = h_ref[...]
    un = _rms(x, ng_ref[...]).astype(BF16)

    dt_raw = _dot(un, win_ref[:, inner + conv_dim:])
    dt = _softplus(dt_raw + dtb_ref[...])
    da = dt * (-jnp.exp(alog_ref[...]))
    tri = _chunk_tri(tm, chunk)
    cs = sum(_dot(tri, part) for part in _split_bf16(da, 3))
    e = e_ref[...]
    csx[...] = sum(_dot(part, e) for part in _split_bf16(cs, 2))
    dtx[...] = _dot(dt.astype(BF16), e)
    cst[...] = cs.T
    dtt[...] = dt.T

    for c0 in range(0, conv_dim, SSD_CONV_BLOCK):
        proj = _dot(un, win_ref[:, inner + c0:inner + c0 + SSD_CONV_BLOCK])
        for c in range(SSD_CONV_BLOCK // LANES):
            xbuf[c0 // LANES + c, SUBLANES:SUBLANES + tm, :] = proj[:, c * LANES:(c + 1) * LANES]
        for c in range(SSD_CONV_BLOCK // LANES):
            t = c0 // LANES + c
            cl = slice(c0 + c * LANES, c0 + (c + 1) * LANES)
            conv = cb_ref[:, cl] + cw_ref[CONV_W - 1:CONV_W, cl] * xbuf[t, SUBLANES:SUBLANES + tm, :]
            for k in range(CONV_W - 1):
                d = CONV_W - 1 - k
                conv = conv + cw_ref[k:k + 1, cl] * xbuf[t, pl.ds(SUBLANES - d, tm, stride=1), :]
            xbuf[t, 0:SUBLANES, :] = xbuf[t, tm:tm + SUBLANES, :]
            cbuf[:, cl] = _silu(conv)
    zbuf[...] = _silu(_dot(un, win_ref[:, 0:inner]))

    ii = lax.broadcasted_iota(jnp.int32, (chunk, LANES), 0)
    ll = lax.broadcasted_iota(jnp.int32, (chunk, LANES), 1)
    causal2 = ll % half <= ii
    low = ll < half
    low_row = lax.broadcasted_iota(jnp.int32, (1, LANES), 1) < half
    dexp = dexp_ref[...]

    def lane_pair(ref, h0, ci):
        t0 = (ci * chunk // LANES) * LANES
        a = ref[h0:h0 + 1, t0:t0 + LANES]
        b = ref[h0 + 1:h0 + 2, t0:t0 + LANES]
        if (ci * chunk) % LANES == 0:
            return jnp.where(low_row, a, pltpu.roll(b, half, 1))
        return jnp.where(low_row, pltpu.roll(a, half, 1), b)

    for ci in range(tm // chunk):
        rs = slice(ci * chunk, (ci + 1) * chunk)
        cs_last = csx[(ci + 1) * chunk - 1:(ci + 1) * chunk, :]
        bgs, cgs, cb2s, y_offs, s_prevs = [], [], [], [], []
        for g in range(groups):
            bg = cbuf[rs, inner + g * ns:inner + (g + 1) * ns].astype(BF16)
            cg = cbuf[rs, inner + (groups + g) * ns:inner + (groups + g + 1) * ns].astype(BF16)
            s_prev = sst[g]
            bgs.append(bg)
            cgs.append(cg)
            s_prevs.append(s_prev)
            cb2s.append(_dot_nt(cg, jnp.concatenate([bg, bg], axis=0)))
            y_offs.append(_dot(cg, s_prev.astype(BF16)))
        for g in range(groups):
            gl = slice(g * gw, (g + 1) * gw)
            xg = cbuf[rs, gl]
            csg = csx[rs, gl]
            yd = []
            for pp in range(gw // LANES):
                ps = slice(pp * LANES, (pp + 1) * LANES)
                h0 = (g * gw + pp * LANES) // SSD_HEADDIM
                row = lane_pair(cst, h0, ci)
                dtrow = lane_pair(dtt, h0, ci)
                lm = jnp.where(causal2, jnp.exp(jnp.minimum(csg[:, ps] - row, 0.0)), 0.0)
                w = (cb2s[g] * lm * dtrow).astype(BF16)
                xp = xg[:, ps]
                bd = jnp.concatenate([jnp.where(low, xp, 0.0), jnp.where(low, 0.0, xp)],
                                     axis=0).astype(BF16)
                yd.append(_dot(w, bd))
            ybuf[rs, gl] = jnp.concatenate(yd, axis=1) + y_offs[g] * jnp.exp(csg) + dexp[:, gl] * xg
        for g in range(groups):
            gl = slice(g * gw, (g + 1) * gw)
            cl = cs_last[:, gl]
            w1 = jnp.exp(cl - csx[rs, gl]) * dtx[rs, gl]
            sst[g] = s_prevs[g] * jnp.exp(cl) + _dot_tn(bgs[g], (cbuf[rs, gl] * w1).astype(BF16))

    acc = x
    for g in range(groups):
        gl = slice(g * gw, (g + 1) * gw)
        yg = ybuf[:, gl] * zbuf[:, gl]
        yn = yg * lax.rsqrt(jnp.mean(yg * yg, axis=-1, keepdims=True) + NORM_EPS) * gn_ref[:, gl]
        acc = acc + _dot(yn.astype(BF16), wout_ref[gl, :])
    if final:
        acc = _rms(acc, fg_ref[...])
    o_ref[...] = acc


def _ssd_layer(h, ng, fg, w_in, conv_w, conv_b, dt_bias, a_log, d_skip, g_norm, w_out, *, final, tm=256):
    B, S, D = h.shape
    inner = w_out.shape[0]
    heads = inner // SSD_HEADDIM
    conv_dim = inner + 2 * SSD_GROUPS * SSD_STATE
    pad = LANES - heads
    win = jnp.pad(w_in, ((0, 0), (0, pad))).astype(BF16)
    row = lambda v: v.reshape(1, -1).astype(F32)
    padrow = lambda v: jnp.pad(v.astype(F32), (0, pad)).reshape(1, LANES)
    expand = (lax.broadcasted_iota(jnp.int32, (LANES, inner), 1) // SSD_HEADDIM
              == lax.broadcasted_iota(jnp.int32, (LANES, inner), 0)).astype(BF16)
    kern = functools.partial(_ssd_kernel, tm=tm, inner=inner, final=final)
    gw = inner // SSD_GROUPS
    return pl.pallas_call(
        kern,
        grid=(B, S // tm),
        in_specs=[_seq_spec(tm, D), _const_spec((1, D)), _const_spec(win.shape),
                  _const_spec((CONV_W, conv_dim)), _const_spec((1, conv_dim)), _const_spec((1, LANES)),
                  _const_spec((1, LANES)), _const_spec((1, inner)), _const_spec((1, inner)),
                  _const_spec((LANES, inner)), _const_spec((inner, D)), _const_spec((1, D))],
        out_specs=_seq_spec(tm, D),
        out_shape=jax.ShapeDtypeStruct((B, S, D), F32),
        scratch_shapes=[pltpu.VMEM((conv_dim // LANES, tm + SUBLANES, LANES), F32), pltpu.VMEM((tm, conv_dim), F32),
                        pltpu.VMEM((tm, inner), F32), pltpu.VMEM((tm, inner), F32),
                        pltpu.VMEM((LANES, tm), F32), pltpu.VMEM((LANES, tm), F32),
                        pltpu.VMEM((tm, inner), F32), pltpu.VMEM((tm, inner), F32),
                        pltpu.VMEM((SSD_GROUPS, SSD_STATE, gw), F32)],
        compiler_params=pltpu.CompilerParams(dimension_semantics=("arbitrary", "arbitrary"),
                                             vmem_limit_bytes=VMEM_LIMIT),
        name="ssd_layer",
    )(h, row(ng), win, conv_w.astype(F32), row(conv_b), padrow(dt_bias), padrow(a_log),
      row(jnp.repeat(d_skip, SSD_HEADDIM)), row(g_norm), expand, w_out.astype(BF16), row(fg))


def _mla_proj_kernel(h_ref, pos_ref, ng_ref, win_ref, gq_ref, wuq_ref, gkv_ref, wukv_ref, invf_ref,
                     sgn_ref, q_ref, k_ref, v_ref, g_ref, *, tm):
    heads = MLA_HEADS
    kw = heads * LANES
    x = h_ref[...]
    un = _rms(x, ng_ref[...]).astype(BF16)
    t = _dot(un, win_ref[...])
    c_q = t[:, :MLA_Q_RANK]
    c_kv = t[:, MLA_Q_RANK:MLA_Q_RANK + MLA_KV_RANK]
    g0 = MLA_Q_RANK + MLA_KV_RANK
    gate = t[:, g0:g0 + heads * MLA_V]
    k_r = t[:, g0 + heads * MLA_V:]
    g_ref[...] = _silu(gate).astype(BF16)

    ang = pos_ref[...].astype(F32) * invf_ref[...]
    cos = jnp.cos(ang)
    sin = jnp.sin(ang) * sgn_ref[...]
    lane = lax.broadcasted_iota(jnp.int32, (tm, LANES), 1)
    first_half = lane < MLA_NOPE + MLA_ROPE // 2

    def rope(a):
        swapped = jnp.where(first_half, pltpu.roll(a, LANES - MLA_ROPE // 2, 1),
                            pltpu.roll(a, MLA_ROPE // 2, 1))
        return a * cos + swapped * sin

    k_rope = rope(k_r)
    qn = _dot(_rms(c_q, gq_ref[...]).astype(BF16), wuq_ref[...])
    kvn = _dot(_rms(c_kv, gkv_ref[...]).astype(BF16), wukv_ref[...])
    scale = (MLA_NOPE + MLA_ROPE) ** -0.5 * LOG2_E
    for hh in range(heads):
        hs = slice(hh * LANES, (hh + 1) * LANES)
        q_ref[0, hh] = (rope(qn[:, hs]) * scale).astype(BF16)
        k_ref[0, hh] = (kvn[:, hs] + k_rope).astype(BF16)
    ones_row = jnp.where(lax.broadcasted_iota(jnp.int32, (LANES - MLA_V, tm), 0) == 0, 1.0, 0.0)
    for hp in range(heads // 2):
        vt = kvn[:, kw + hp * LANES:kw + (hp + 1) * LANES].T
        for e in range(2):
            v_ref[0, 2 * hp + e, 0] = jnp.concatenate(
                [vt[e * MLA_V:(e + 1) * MLA_V], ones_row], axis=0).astype(BF16)


def _mla_attn_kernel(q_ref, k_ref, vt_ref, g_ref, h_ref, wout_ref, fg_ref, o_ref, m_sc, acc_sc, obuf,
                     *, tq, final):
    heads = MLA_HEADS
    j = pl.program_id(1)
    key_idx = lax.broadcasted_iota(jnp.int32, (tq, tq), 0)
    qry_idx = lax.broadcasted_iota(jnp.int32, (tq, tq), 1)
    diag = key_idx <= qry_idx
    m_sc[...] = jnp.full(m_sc.shape, -jnp.inf, F32)
    acc_sc[...] = jnp.zeros(acc_sc.shape, F32)

    def kv_tile(kt, masked):
        k0 = pl.multiple_of(kt * tq, tq)

        def scores(hh):
            return _dot_nt(k_ref[0, hh, pl.ds(k0, tq), :], q_ref[0, hh])

        ready = [scores(hh) for hh in range(MLA_LOOKAHEAD)]
        for hh in range(heads):
            if hh % MLA_LOOKAHEAD == 0:
                nxt = range(hh + MLA_LOOKAHEAD, min(hh + 2 * MLA_LOOKAHEAD, heads))
                ready.extend(scores(h2) for h2 in nxt)
            st = ready.pop(0)
            if masked:
                st = jnp.where(diag, st, -jnp.inf)
            m_old = m_sc[hh]
            m_new = jnp.maximum(m_old, jnp.max(st, axis=0, keepdims=True))
            pt = jnp.exp2(st - m_new).astype(BF16)
            acc_sc[hh] = jnp.exp2(m_old - m_new) * acc_sc[hh] + _dot(vt_ref[0, hh, kt], pt)
            m_sc[hh] = m_new

    def full_tile(kt, carry):
        kv_tile(kt, False)
        return carry

    lax.fori_loop(0, j, full_tile, 0)
    kv_tile(j, True)

    for hp in range(heads // 2):
        halves = []
        for e in range(2):
            acc = acc_sc[2 * hp + e]
            halves.append(acc[:MLA_V] * (1.0 / acc[MLA_V:MLA_V + 1]))
        obuf[:, hp * LANES:(hp + 1) * LANES] = jnp.concatenate(halves, axis=0).T
    y = (obuf[...] * g_ref[...].astype(F32)).astype(BF16)
    o_ref[...] = _residual(h_ref[...], _dot(y, wout_ref[...]), fg_ref, final)


def _mla_layer(h, positions, ng, fg, w_in, g_q, w_uq, g_kv, w_ukv, w_out, *, final, tq=256):
    B, S, D = h.shape
    heads, half = MLA_HEADS, MLA_ROPE // 2
    qk = MLA_NOPE + MLA_ROPE
    width = heads * MLA_V
    c0 = MLA_Q_RANK + MLA_KV_RANK
    w_kr = jnp.pad(w_in[:, c0:c0 + MLA_ROPE], ((0, 0), (MLA_NOPE, LANES - qk)))
    win = jnp.concatenate([w_in[:, :c0], w_in[:, c0 + MLA_ROPE:], w_kr], axis=1).astype(BF16)
    wuq = jnp.pad(w_uq.reshape(MLA_Q_RANK, heads, qk), ((0, 0), (0, 0), (0, LANES - qk)))
    wuq = wuq.reshape(MLA_Q_RANK, heads * LANES).astype(BF16)
    wkv = w_ukv.reshape(MLA_KV_RANK, heads, MLA_NOPE + MLA_V)
    wk = jnp.pad(wkv[:, :, :MLA_NOPE], ((0, 0), (0, 0), (0, LANES - MLA_NOPE)))
    wukv = jnp.concatenate([wk.reshape(MLA_KV_RANK, heads * LANES),
                            wkv[:, :, MLA_NOPE:].reshape(MLA_KV_RANK, width)], axis=1).astype(BF16)
    inv_freq = ROPE_THETA ** (-jnp.arange(0, MLA_ROPE, 2, dtype=F32) / MLA_ROPE)
    zeros = lambda n: jnp.zeros((n,), F32)
    invf = jnp.concatenate([zeros(MLA_NOPE), inv_freq, inv_freq, zeros(LANES - qk)]).reshape(1, LANES)
    sgn = jnp.concatenate([jnp.ones((MLA_NOPE,), F32), -jnp.ones((half,), F32),
                           jnp.ones((LANES - MLA_NOPE - half,), F32)]).reshape(1, LANES)
    row = lambda v: v.reshape(1, -1).astype(F32)

    tm = tq
    head_spec = pl.BlockSpec((1, heads, tm, LANES), lambda b, j: (b, 0, j, 0))
    vt_spec = pl.BlockSpec((1, heads, 1, LANES, tm), lambda b, j: (b, 0, j, 0, 0))
    q, k, vt, g = pl.pallas_call(
        functools.partial(_mla_proj_kernel, tm=tm),
        grid=(B, S // tm),
        in_specs=[_seq_spec(tm, D), _seq_spec(tm, 1), _const_spec((1, D)), _const_spec(win.shape),
                  _const_spec((1, MLA_Q_RANK)), _const_spec(wuq.shape), _const_spec((1, MLA_KV_RANK)),
                  _const_spec(wukv.shape), _const_spec((1, LANES)), _const_spec((1, LANES))],
        out_specs=[head_spec, head_spec, vt_spec, _seq_spec(tm, width)],
        out_shape=[jax.ShapeDtypeStruct((B, heads, S, LANES), BF16),
                   jax.ShapeDtypeStruct((B, heads, S, LANES), BF16),
                   jax.ShapeDtypeStruct((B, heads, S // tm, LANES, tm), BF16),
                   jax.ShapeDtypeStruct((B, S, width), BF16)],
        compiler_params=pltpu.CompilerParams(dimension_semantics=("arbitrary", "arbitrary"),
                                             vmem_limit_bytes=VMEM_LIMIT),
        name="mla_proj",
    )(h, positions.reshape(B, S, 1), row(ng), win, row(g_q), wuq, row(g_kv), wukv, invf, sgn)

    return pl.pallas_call(
        functools.partial(_mla_attn_kernel, tq=tq, final=final),
        grid=(B, S // tq),
        in_specs=[head_spec, pl.BlockSpec((1, heads, S, LANES), lambda b, j: (b, 0, 0, 0)),
                  pl.BlockSpec((1, heads, S // tm, LANES, tm), lambda b, j: (b, 0, 0, 0, 0)),
                  _seq_spec(tq, width), _seq_spec(tq, D), _const_spec((width, D)), _const_spec((1, D))],
        out_specs=_seq_spec(tq, D),
        out_shape=jax.ShapeDtypeStruct((B, S, D), F32),
        scratch_shapes=[pltpu.VMEM((heads, 1, tq), F32), pltpu.VMEM((heads, LANES, tq), F32),
                        pltpu.VMEM((tq, width), F32)],
        compiler_params=pltpu.CompilerParams(dimension_semantics=("arbitrary", "arbitrary"),
                                             vmem_limit_bytes=VMEM_LIMIT),
        name="mla_attn",
    )(q, k, vt, g, h, w_out.astype(BF16), row(fg))


def kernel(x, positions, norm_g, final_g, mla_w_in, mla_g_q, mla_w_uq, mla_g_kv, mla_w_ukv, mla_w_out, gla_w_in, gla_w_gk2, gla_b_gk, gla_g_o, gla_w_out, lru_w_in, lru_conv_w, lru_conv_b, lru_w_a, lru_b_a, lru_w_x, lru_b_x, lru_lam, lru_w_out, ssd_w_in, ssd_conv_w, ssd_conv_b, ssd_dt_bias, ssd_a_log, ssd_d, ssd_g_norm, ssd_w_out):
    depth = norm_g.shape[0]
    h = x
    for i in range(depth):
        m, j = i % N_MIXERS, i // N_MIXERS
        common = dict(final=(i == depth - 1))
        if m == 0:
            h = _mla_layer(h, positions, norm_g[i], final_g, mla_w_in[j], mla_g_q[j], mla_w_uq[j],
                           mla_g_kv[j], mla_w_ukv[j], mla_w_out[j], **common)
        elif m == 1:
            h = _gla_layer(h, norm_g[i], final_g, gla_w_in[j], gla_w_gk2[j], gla_b_gk[j], gla_g_o[j],
                           gla_w_out[j], **common)
        elif m == 2:
            h = _lru_layer(h, norm_g[i], final_g, lru_w_in[j], lru_conv_w[j], lru_conv_b[j], lru_w_a[j],
                           lru_b_a[j], lru_w_x[j], lru_b_x[j], lru_lam[j], lru_w_out[j], **common)
        else:
            h = _ssd_layer(h, norm_g[i], final_g, ssd_w_in[j], ssd_conv_w[j], ssd_conv_b[j],
                           ssd_dt_bias[j], ssd_a_log[j], ssd_d[j], ssd_g_norm[j], ssd_w_out[j], **common)
    return h
```

```python
import functools

import jax
import jax.numpy as jnp
from jax import lax
from jax.experimental import pallas as pl
from jax.experimental.pallas import tpu as pltpu

F32 = jnp.float32
BF16 = jnp.bfloat16

NORM_EPS = 1e-6
N_MIXERS = 4
MLA_HEADS = 16
MLA_Q_RANK = 384
MLA_KV_RANK = 256
MLA_NOPE = 64
MLA_ROPE = 32
MLA_V = 64
MLA_VROWS = 80
MLA_LOOKAHEAD = 4
ROPE_THETA = 10000.0
GLA_HEADS = 4
GLA_GATE_RANK = 16
GLA_TAU = 16.0
GLA_CHUNK = 64
LRU_BLOCKS = 10
LRU_C = 8.0
CONV_W = 4
SSD_HEADDIM = 64
SSD_GROUPS = 8
SSD_STATE = 128
SSD_CHUNK = 64
SSD_CONV_BLOCK = 1024

LOG2_E = 1.4426950408889634
LANES = 128
SUBLANES = 8
VMEM_LIMIT = 56 * 1024 * 1024


def _dot(a, b):
    return jnp.dot(a, b, preferred_element_type=F32)


def _dot_nt(a, b):
    return lax.dot_general(a, b, (((1,), (1,)), ((), ())), preferred_element_type=F32)


def _dot_tn(a, b):
    return lax.dot_general(a, b, (((0,), (0,)), ((), ())), preferred_element_type=F32)


def _rms(x, g):
    ms = jnp.mean(x * x, axis=-1, keepdims=True)
    return x * lax.rsqrt(ms + NORM_EPS) * g


def _softplus(x):
    return jnp.maximum(x, 0.0) + jnp.log1p(jnp.exp(-jnp.abs(x)))


def _sigmoid(x):
    return 0.5 * jnp.tanh(0.5 * x) + 0.5


def _silu(x):
    hx = 0.5 * x
    return hx * jnp.tanh(hx) + hx


def _residual(x, y, fg_ref, final):
    r = x + y
    if final:
        r = _rms(r, fg_ref[...])
    return r


def _chunk_tri(n, chunk):
    r = lax.broadcasted_iota(jnp.int32, (n, n), 0)
    c = lax.broadcasted_iota(jnp.int32, (n, n), 1)
    keep = jnp.logical_and(r // chunk == c // chunk, c <= r)
    return jnp.where(keep, 1.0, 0.0).astype(BF16)


def _split_bf16(x, parts):
    out = []
    for _ in range(parts - 1):
        hi = x.astype(BF16)
        out.append(hi)
        x = x - hi.astype(F32)
    out.append(x.astype(BF16))
    return out


def _const_spec(shape):
    n = len(shape)
    return pl.BlockSpec(shape, lambda *_: (0,) * n, pipeline_mode=pl.Buffered(1))


def _seq_spec(tm, d):
    return pl.BlockSpec((None, tm, d), lambda b, j: (b, j, 0))


def _lru_kernel(h_ref, ng_ref, win_ref, cw_ref, cb_ref, wax_ref, ba_ref, bx_ref, lam_ref, wout_ref,
                fg_ref, o_ref, ubuf, abuf, bbuf, gbuf, hst, *, tm, nb, width, final):
    hist = (CONV_W - 1) * nb
    tt = tm // nb

    @pl.when(pl.program_id(0) == 0)
    def _():
        ubuf[0:hist, :] = jnp.zeros((hist, width), F32)
        hst[...] = jnp.zeros_like(hst)

    x = pltpu.einshape("btd->tbd", h_ref[...]).reshape(tm, h_ref.shape[-1])
    un = _rms(x, ng_ref[...]).astype(BF16)
    ubuf[hist:hist + tm, :] = _dot(un, win_ref[:, width:])
    gbuf[...] = _silu(_dot(un, win_ref[:, :width]))
    cw = cw_ref[...]
    conv = cb_ref[...] + cw[CONV_W - 1:CONV_W] * ubuf[hist:hist + tm, :]
    for k in range(CONV_W - 1):
        d = (CONV_W - 1 - k) * nb
        conv = conv + cw[k:k + 1] * ubuf[hist - d:hist - d + tm, :]
    ubuf[0:hist, :] = ubuf[tm:tm + hist, :]

    sp = _softplus(-lam_ref[...])
    for n in range(width // LANES):
        sl = slice(n * LANES, (n + 1) * LANES)
        cn = conv[:, sl]
        ra = _dot(cn.astype(BF16), wax_ref[n])
        r = _sigmoid(ra[:, :LANES] + ba_ref[:, sl])
        i = _sigmoid(ra[:, LANES:] + bx_ref[:, sl])
        log_a = -LRU_C * r * sp[:, sl]
        th = jnp.tanh(log_a)
        u = -2.0 * th
        mult = jnp.where(u > 0.0, u * lax.rsqrt(u * (1.0 - th)), 0.0)
        abuf[:, sl] = jnp.exp(log_a)
        bbuf[:, sl] = mult * (i * cn)

    hs = hst[...]
    for t in range(tm // nb):
        rs = slice(t * nb, (t + 1) * nb)
        hs = abuf[rs, :] * hs + bbuf[rs, :]
        bbuf[rs, :] = hs
    hst[...] = hs
    y = (bbuf[...] * gbuf[...]).astype(BF16)
    out = _residual(x, _dot(y, wout_ref[...]), fg_ref, final)
    o_ref[...] = pltpu.einshape("tbd->btd", out.reshape(tt, nb, out.shape[-1]))


def _lru_layer(h, ng, fg, w_in, conv_w, conv_b, w_a, b_a, w_x, b_x, lam, w_out, *, final, tm=512):
    B, S, D = h.shape
    assert B == SUBLANES, "time-major RG-LRU tiling needs one sublane tile per time step"
    width = w_out.shape[0]
    wax = jnp.concatenate([w_a, w_x], axis=-1).astype(BF16)
    row = lambda v: v.reshape(1, -1).astype(F32)
    kern = functools.partial(_lru_kernel, tm=tm, nb=B, width=width, final=final)
    hist = (CONV_W - 1) * B
    tile = pl.BlockSpec((B, tm // B, D), lambda j: (0, j, 0))
    return pl.pallas_call(
        kern,
        grid=(S * B // tm,),
        in_specs=[tile, _const_spec((1, D)), _const_spec((D, 2 * width)),
                  _const_spec((CONV_W, width)), _const_spec((1, width)), _const_spec(wax.shape),
                  _const_spec((1, width)), _const_spec((1, width)), _const_spec((1, width)),
                  _const_spec((width, D)), _const_spec((1, D))],
        out_specs=tile,
        out_shape=jax.ShapeDtypeStruct((B, S, D), F32),
        scratch_shapes=[pltpu.VMEM((tm + hist, width), F32), pltpu.VMEM((tm, width), F32),
                        pltpu.VMEM((tm, width), F32), pltpu.VMEM((tm, width), F32),
                        pltpu.VMEM((B, width), F32)],
        compiler_params=pltpu.CompilerParams(dimension_semantics=("arbitrary",),
                                             vmem_limit_bytes=VMEM_LIMIT),
        name="lru_layer",
    )(h, row(ng), w_in.astype(BF16), conv_w.astype(F32), row(conv_b), wax, row(b_a), row(b_x),
      row(lam), w_out.astype(BF16), row(fg))


def _gla_kernel(h_ref, ng_ref, wq_ref, wgk_ref, wgk2_ref, bgk_ref, go_ref, wout_ref, fg_ref, o_ref,
                sst, obuf, gbuf, *, tm, key, val, final):
    heads, chunk = GLA_HEADS, GLA_CHUNK
    dk, dv = key // heads, val // heads

    @pl.when(pl.program_id(1) == 0)
    def _():
        sst[...] = jnp.zeros_like(sst)

    x = h_ref[...]
    un = _rms(x, ng_ref[...]).astype(BF16)
    q = _dot(un, wq_ref[:, :key]) * dk ** -0.5
    k = _dot(un, wq_ref[:, key:2 * key])
    gk = _dot(un, wgk_ref[...]).astype(BF16)
    log_a = -_softplus(-(_dot(gk, wgk2_ref[...]) + bgk_ref[...])) * (1.0 / GLA_TAU)
    v = _dot(un, wq_ref[:, 2 * key:2 * key + val]).astype(BF16)
    gbuf[...] = _silu(_dot(un, wq_ref[:, 2 * key + val:]))
    tri = _chunk_tri(tm, chunk)
    b = sum(_dot(tri, part) for part in _split_bf16(log_a, 2))

    rr = lax.broadcasted_iota(jnp.int32, (chunk, chunk), 0)
    cc = lax.broadcasted_iota(jnp.int32, (chunk, chunk), 1)
    causal = cc <= rr
    nch = tm // chunk
    ksl = [slice(hh * dk, (hh + 1) * dk) for hh in range(heads)]
    vsl = [slice(hh * dv, (hh + 1) * dv) for hh in range(heads)]
    rows = [slice(ci * chunk, (ci + 1) * chunk) for ci in range(nch)]
    qt, dec, att, ds = [], [], [], []
    for rs in rows:
        bc = b[rs]
        bl = bc[chunk - 1:chunk, :]
        qt.append((q[rs] * jnp.exp(bc)).astype(BF16))
        kt = (k[rs] * jnp.exp(-bc)).astype(BF16)
        ke = (k[rs] * jnp.exp(bl - bc)).astype(BF16)
        dec.append(jnp.exp(bl))
        att.append([_dot_nt(qt[-1][:, ks], kt[:, ks]) for ks in ksl])
        ds.append([_dot_tn(v[rs, vsl[hh]], ke[:, ksl[hh]]) for hh in range(heads)])
    for hh in range(heads):
        s_in = sst[hh]
        for ci, rs in enumerate(rows):
            p = jnp.where(causal, att[ci][hh], 0.0).astype(BF16)
            obuf[rs, vsl[hh]] = _dot(p, v[rs, vsl[hh]]) + _dot_nt(qt[ci][:, ksl[hh]], s_in.astype(BF16))
            s_in = dec[ci][:, ksl[hh]] * s_in + ds[ci][hh]
        sst[hh] = s_in

    parts = []
    for hh in range(heads):
        oh = obuf[:, hh * dv:(hh + 1) * dv]
        parts.append(oh * lax.rsqrt(jnp.mean(oh * oh, axis=-1, keepdims=True) + NORM_EPS))
    on = jnp.concatenate(parts, axis=1) * go_ref[...]
    y = (on * gbuf[...]).astype(BF16)
    o_ref[...] = _residual(x, _dot(y, wout_ref[...]), fg_ref, final)


def _gla_layer(h, ng, fg, w_in, w_gk2, b_gk, g_o, w_out, *, final, tm=256):
    B, S, D = h.shape
    val = w_out.shape[0]
    key = (w_in.shape[1] - 2 * val - GLA_GATE_RANK) // 2
    main = 2 * key + 2 * val
    pad = LANES - GLA_GATE_RANK
    w_in = w_in.astype(BF16)
    wq = w_in[:, :main]
    wgk = jnp.pad(w_in[:, main:], ((0, 0), (0, pad)))
    wgk2 = jnp.pad(w_gk2.astype(BF16), ((0, pad), (0, 0)))
    row = lambda v: v.reshape(1, -1).astype(F32)
    kern = functools.partial(_gla_kernel, tm=tm, key=key, val=val, final=final)
    return pl.pallas_call(
        kern,
        grid=(B, S // tm),
        in_specs=[_seq_spec(tm, D), _const_spec((1, D)), _const_spec((D, main)), _const_spec((D, LANES)),
                  _const_spec((LANES, key)), _const_spec((1, key)), _const_spec((1, val)),
                  _const_spec((val, D)), _const_spec((1, D))],
        out_specs=_seq_spec(tm, D),
        out_shape=jax.ShapeDtypeStruct((B, S, D), F32),
        scratch_shapes=[pltpu.VMEM((GLA_HEADS, val // GLA_HEADS, key // GLA_HEADS), F32),
                        pltpu.VMEM((tm, val), F32), pltpu.VMEM((tm, val), F32)],
        compiler_params=pltpu.CompilerParams(dimension_semantics=("arbitrary", "arbitrary"),
                                             vmem_limit_bytes=VMEM_LIMIT),
        name="gla_layer",
    )(h, row(ng), wq, wgk, wgk2, row(b_gk), row(jnp.tile(g_o, GLA_HEADS)), w_out.astype(BF16), row(fg))


def _ssd_kernel(h_ref, ng_ref, win_ref, cw_ref, cb_ref, dtb_ref, alog_ref, dexp_ref, gn_ref, e_ref,
                wout_ref, fg_ref, o_ref, xbuf, cbuf, csx, dtx, cst, dtt, zbuf, ybuf, sst, *, tm, inner, final):
    groups, ns, chunk = SSD_GROUPS, SSD_STATE, SSD_CHUNK
    gw = inner // groups
    conv_dim = inner + 2 * groups * ns
    half = SSD_HEADDIM

    @pl.when(pl.program_id(1) == 0)
    def _():
        xbuf[:, 0:SUBLANES, :] = jnp.zeros((conv_dim // LANES, SUBLANES, LANES), F32)
        sst[...] = jnp.zeros_like(sst)

    x = h_ref[...]
    un = _rms(x, ng_ref[...]).astype(BF16)

    dt_raw = _dot(un, win_ref[:, inner + conv_dim:])
    dt = _softplus(dt_raw + dtb_ref[...])
    da = dt * (-jnp.exp(alog_ref[...]))
    tri = _chunk_tri(tm, chunk)
    cs = sum(_dot(tri, part) for part in _split_bf16(da, 3))
    e = e_ref[...]
    csx[...] = sum(_dot(part, e) for part in _split_bf16(cs, 2))
    dtx[...] = _dot(dt.astype(BF16), e)
    cst[...] = cs.T
    dtt[...] = dt.T

    for c0 in range(0, conv_dim, SSD_CONV_BLOCK):
        proj = _dot(un, win_ref[:, inner + c0:inner + c0 + SSD_CONV_BLOCK])
        for c in range(SSD_CONV_BLOCK // LANES):
            xbuf[c0 // LANES + c, SUBLANES:SUBLANES + tm, :] = proj[:, c * LANES:(c + 1) * LANES]
        for c in range(SSD_CONV_BLOCK // LANES):
            t = c0 // LANES + c
            cl = slice(c0 + c * LANES, c0 + (c + 1) * LANES)
            conv = cb_ref[:, cl] + cw_ref[CONV_W - 1:CONV_W, cl] * xbuf[t, SUBLANES:SUBLANES + tm, :]
            for k in range(CONV_W - 1):
                d = CONV_W - 1 - k
                conv = conv + cw_ref[k:k + 1, cl] * xbuf[t, pl.ds(SUBLANES - d, tm, stride=1), :]
            xbuf[t, 0:SUBLANES, :] = xbuf[t, tm:tm + SUBLANES, :]
            cbuf[:, cl] = _silu(conv)
    zbuf[...] = _silu(_dot(un, win_ref[:, 0:inner]))

    ii = lax.broadcasted_iota(jnp.int32, (chunk, LANES), 0)
    ll = lax.broadcasted_iota(jnp.int32, (chunk, LANES), 1)
    causal2 = ll % half <= ii
    low = ll < half
    low_row = lax.broadcasted_iota(jnp.int32, (1, LANES), 1) < half
    dexp = dexp_ref[...]

    def lane_pair(ref, h0, ci):
        t0 = (ci * chunk // LANES) * LANES
        a = ref[h0:h0 + 1, t0:t0 + LANES]
        b = ref[h0 + 1:h0 + 2, t0:t0 + LANES]
        if (ci * chunk) % LANES == 0:
            return jnp.where(low_row, a, pltpu.roll(b, half, 1))
        return jnp.where(low_row, pltpu.roll(a, half, 1), b)

    for ci in range(tm // chunk):
        rs = slice(ci * chunk, (ci + 1) * chunk)
        cs_last = csx[(ci + 1) * chunk - 1:(ci + 1) * chunk, :]
        bgs, cgs, cb2s, y_offs, s_prevs = [], [], [], [], []
        for g in range(groups):
            bg = cbuf[rs, inner + g * ns:inner + (g + 1) * ns].astype(BF16)
            cg = cbuf[rs, inner + (groups + g) * ns:inner + (groups + g + 1) * ns].astype(BF16)
            s_prev = sst[g]
            bgs.append(bg)
            cgs.append(cg)
            s_prevs.append(s_prev)
            cb2s.append(_dot_nt(cg, jnp.concatenate([bg, bg], axis=0)))
            y_offs.append(_dot(cg, s_prev.astype(BF16)))
        for g in range(groups):
            gl = slice(g * gw, (g + 1) * gw)
            xg = cbuf[rs, gl]
            csg = csx[rs, gl]
            yd = []
            for pp in range(gw // LANES):
                ps = slice(pp * LANES, (pp + 1) * LANES)
                h0 = (g * gw + pp * LANES) // SSD_HEADDIM
                row = lane_pair(cst, h0, ci)
                dtrow = lane_pair(dtt, h0, ci)
                lm = jnp.where(causal2, jnp.exp(jnp.minimum(csg[:, ps] - row, 0.0)), 0.0)
                w = (cb2s[g] * lm * dtrow).astype(BF16)
                xp = xg[:, ps]
                bd = jnp.concatenate([jnp.where(low, xp, 0.0), jnp.where(low, 0.0, xp)],
                                     axis=0).astype(BF16)
                yd.append(_dot(w, bd))
            ybuf[rs, gl] = jnp.concatenate(yd, axis=1) + y_offs[g] * jnp.exp(csg) + dexp[:, gl] * xg
        for g in range(groups):
            gl = slice(g * gw, (g + 1) * gw)
            cl = cs_last[:, gl]
            w1 = jnp.exp(cl - csx[rs, gl]) * dtx[rs, gl]
            sst[g] = s_prevs[g] * jnp.exp(cl) + _dot_tn(bgs[g], (cbuf[rs, gl] * w1).astype(BF16))

    acc = x
    for g in range(groups):
        gl = slice(g * gw, (g + 1) * gw)
        yg = ybuf[:, gl] * zbuf[:, gl]
        yn = yg * lax.rsqrt(jnp.mean(yg * yg, axis=-1, keepdims=True) + NORM_EPS) * gn_ref[:, gl]
        acc = acc + _dot(yn.astype(BF16), wout_ref[gl, :])
    if final:
        acc = _rms(acc, fg_ref[...])
    o_ref[...] = acc


def _ssd_layer(h, ng, fg, w_in, conv_w, conv_b, dt_bias, a_log, d_skip, g_norm, w_out, *, final, tm=256):
    B, S, D = h.shape
    inner = w_out.shape[0]
    heads = inner // SSD_HEADDIM
    conv_dim = inner + 2 * SSD_GROUPS * SSD_STATE
    pad = LANES - heads
    win = jnp.pad(w_in.astype(BF16), ((0, 0), (0, pad)))
    row = lambda v: v.reshape(1, -1).astype(F32)
    padrow = lambda v: jnp.pad(v.astype(F32), (0, pad)).reshape(1, LANES)
    expand = (lax.broadcasted_iota(jnp.int32, (LANES, inner), 1) // SSD_HEADDIM
              == lax.broadcasted_iota(jnp.int32, (LANES, inner), 0)).astype(BF16)
    kern = functools.partial(_ssd_kernel, tm=tm, inner=inner, final=final)
    gw = inner // SSD_GROUPS
    return pl.pallas_call(
        kern,
        grid=(B, S // tm),
        in_specs=[_seq_spec(tm, D), _const_spec((1, D)), _const_spec(win.shape),
                  _const_spec((CONV_W, conv_dim)), _const_spec((1, conv_dim)), _const_spec((1, LANES)),
                  _const_spec((1, LANES)), _const_spec((1, inner)), _const_spec((1, inner)),
                  _const_spec((LANES, inner)), _const_spec((inner, D)), _const_spec((1, D))],
        out_specs=_seq_spec(tm, D),
        out_shape=jax.ShapeDtypeStruct((B, S, D), F32),
        scratch_shapes=[pltpu.VMEM((conv_dim // LANES, tm + SUBLANES, LANES), F32), pltpu.VMEM((tm, conv_dim), F32),
                        pltpu.VMEM((tm, inner), F32), pltpu.VMEM((tm, inner), F32),
                        pltpu.VMEM((LANES, tm), F32), pltpu.VMEM((LANES, tm), F32),
                        pltpu.VMEM((tm, inner), F32), pltpu.VMEM((tm, inner), F32),
                        pltpu.VMEM((SSD_GROUPS, SSD_STATE, gw), F32)],
        compiler_params=pltpu.CompilerParams(dimension_semantics=("arbitrary", "arbitrary"),
                                             vmem_limit_bytes=VMEM_LIMIT),
        name="ssd_layer",
    )(h, row(ng), win, conv_w.astype(F32), row(conv_b), padrow(dt_bias), padrow(a_log),
      row(jnp.repeat(d_skip, SSD_HEADDIM)), row(g_norm), expand, w_out.astype(BF16), row(fg))


def _mla_proj_kernel(h_ref, pos_ref, ng_ref, win_ref, gq_ref, wuq_ref, gkv_ref, wukv_ref, invf_ref,
                     sgn_ref, q_ref, k_ref, v_ref, g_ref, *, tm):
    heads = MLA_HEADS
    kw = heads * LANES
    x = h_ref[...]
    un = _rms(x, ng_ref[...]).astype(BF16)
    t = _dot(un, win_ref[...])
    c_q = t[:, :MLA_Q_RANK]
    c_kv = t[:, MLA_Q_RANK:MLA_Q_RANK + MLA_KV_RANK]
    g0 = MLA_Q_RANK + MLA_KV_RANK
    gate = t[:, g0:g0 + heads * MLA_V]
    k_r = t[:, g0 + heads * MLA_V:]
    g_ref[...] = _silu(gate).astype(BF16)

    ang = pos_ref[...].astype(F32) * invf_ref[...]
    cos = jnp.cos(ang)
    sin = jnp.sin(ang) * sgn_ref[...]
    lane = lax.broadcasted_iota(jnp.int32, (tm, LANES), 1)
    first_half = lane < MLA_NOPE + MLA_ROPE // 2

    def rope(a):
        swapped = jnp.where(first_half, pltpu.roll(a, LANES - MLA_ROPE // 2, 1),
                            pltpu.roll(a, MLA_ROPE // 2, 1))
        return a * cos + swapped * sin

    k_rope = rope(k_r)
    qn = _dot(_rms(c_q, gq_ref[...]).astype(BF16), wuq_ref[...])
    kvn = _dot(_rms(c_kv, gkv_ref[...]).astype(BF16), wukv_ref[...])
    scale = (MLA_NOPE + MLA_ROPE) ** -0.5 * LOG2_E
    for hh in range(heads):
        hs = slice(hh * LANES, (hh + 1) * LANES)
        q_ref[0, hh] = (rope(qn[:, hs]) * scale).astype(BF16)
        k_ref[0, hh] = (kvn[:, hs] + k_rope).astype(BF16)
    ones_row = jnp.where(lax.broadcasted_iota(jnp.int32, (MLA_VROWS - MLA_V, tm), 0) == 0, 1.0, 0.0)
    for hp in range(heads // 2):
        vt = kvn[:, kw + hp * LANES:kw + (hp + 1) * LANES].T
        for e in range(2):
            v_ref[0, 2 * hp + e, 0] = jnp.concatenate(
                [vt[e * MLA_V:(e + 1) * MLA_V], ones_row], axis=0).astype(BF16)


def _mla_attn_kernel(q_ref, k_ref, vt_ref, g_ref, h_ref, wout_ref, fg_ref, o_ref, m_sc, acc_sc, obuf,
                     *, tq, final):
    heads = MLA_HEADS
    j = pl.program_id(1)
    key_idx = lax.broadcasted_iota(jnp.int32, (tq, tq), 0)
    qry_idx = lax.broadcasted_iota(jnp.int32, (tq, tq), 1)
    diag = key_idx <= qry_idx
    m_sc[...] = jnp.full(m_sc.shape, -jnp.inf, F32)
    acc_sc[...] = jnp.zeros(acc_sc.shape, F32)

    def kv_tile(kt, masked):
        k0 = pl.multiple_of(kt * tq, tq)

        def scores(hh):
            return _dot_nt(k_ref[0, hh, pl.ds(k0, tq), :], q_ref[0, hh])

        ready = [scores(hh) for hh in range(MLA_LOOKAHEAD)]
        for hh in range(heads):
            if hh % MLA_LOOKAHEAD == 0:
                nxt = range(hh + MLA_LOOKAHEAD, min(hh + 2 * MLA_LOOKAHEAD, heads))
                ready.extend(scores(h2) for h2 in nxt)
            st = ready.pop(0)
            if masked:
                st = jnp.where(diag, st, -jnp.inf)
            m_old = m_sc[hh]
            m_new = jnp.maximum(m_old, jnp.max(st, axis=0, keepdims=True))
            pt = jnp.exp2(st - m_new).astype(BF16)
            acc_sc[hh] = jnp.exp2(m_old - m_new) * acc_sc[hh] + _dot(vt_ref[0, hh, kt], pt)
            m_sc[hh] = m_new

    def full_tile(kt, carry):
        kv_tile(kt, False)
        return carry

    lax.fori_loop(0, j, full_tile, 0)
    kv_tile(j, True)

    for hp in range(heads // 2):
        halves = []
        for e in range(2):
            acc = acc_sc[2 * hp + e]
            halves.append(acc[:MLA_V] * (1.0 / acc[MLA_V:MLA_V + 1]))
        obuf[:, hp * LANES:(hp + 1) * LANES] = jnp.concatenate(halves, axis=0).T
    y = (obuf[...] * g_ref[...].astype(F32)).astype(BF16)
    o_ref[...] = _residual(h_ref[...], _dot(y, wout_ref[...]), fg_ref, final)


def _mla_layer(h, positions, ng, fg, w_in, g_q, w_uq, g_kv, w_ukv, w_out, *, final, tq=256):
    B, S, D = h.shape
    heads, half = MLA_HEADS, MLA_ROPE // 2
    qk = MLA_NOPE + MLA_ROPE
    width = heads * MLA_V
    c0 = MLA_Q_RANK + MLA_KV_RANK
    w_in = w_in.astype(BF16)
    w_kr = jnp.pad(w_in[:, c0:c0 + MLA_ROPE], ((0, 0), (MLA_NOPE, LANES - qk)))
    win = jnp.concatenate([w_in[:, :c0], w_in[:, c0 + MLA_ROPE:], w_kr], axis=1)
    wuq = jnp.pad(w_uq.reshape(MLA_Q_RANK, heads, qk), ((0, 0), (0, 0), (0, LANES - qk)))
    wuq = wuq.reshape(MLA_Q_RANK, heads * LANES).astype(BF16)
    wkv = w_ukv.reshape(MLA_KV_RANK, heads, MLA_NOPE + MLA_V)
    wk = jnp.pad(wkv[:, :, :MLA_NOPE], ((0, 0), (0, 0), (0, LANES - MLA_NOPE)))
    wukv = jnp.concatenate([wk.reshape(MLA_KV_RANK, heads * LANES),
                            wkv[:, :, MLA_NOPE:].reshape(MLA_KV_RANK, width)], axis=1).astype(BF16)
    inv_freq = ROPE_THETA ** (-jnp.arange(0, MLA_ROPE, 2, dtype=F32) / MLA_ROPE)
    zeros = lambda n: jnp.zeros((n,), F32)
    invf = jnp.concatenate([zeros(MLA_NOPE), inv_freq, inv_freq, zeros(LANES - qk)]).reshape(1, LANES)
    sgn = jnp.concatenate([jnp.ones((MLA_NOPE,), F32), -jnp.ones((half,), F32),
                           jnp.ones((LANES - MLA_NOPE - half,), F32)]).reshape(1, LANES)
    row = lambda v: v.reshape(1, -1).astype(F32)

    tm = tq
    head_spec = pl.BlockSpec((1, heads, tm, LANES), lambda b, j: (b, 0, j, 0))
    vt_spec = pl.BlockSpec((1, heads, 1, MLA_VROWS, tm), lambda b, j: (b, 0, j, 0, 0))
    q, k, vt, g = pl.pallas_call(
        functools.partial(_mla_proj_kernel, tm=tm),
        grid=(B, S // tm),
        in_specs=[_seq_spec(tm, D), _seq_spec(tm, 1), _const_spec((1, D)), _const_spec(win.shape),
                  _const_spec((1, MLA_Q_RANK)), _const_spec(wuq.shape), _const_spec((1, MLA_KV_RANK)),
                  _const_spec(wukv.shape), _const_spec((1, LANES)), _const_spec((1, LANES))],
        out_specs=[head_spec, head_spec, vt_spec, _seq_spec(tm, width)],
        out_shape=[jax.ShapeDtypeStruct((B, heads, S, LANES), BF16),
                   jax.ShapeDtypeStruct((B, heads, S, LANES), BF16),
                   jax.ShapeDtypeStruct((B, heads, S // tm, MLA_VROWS, tm), BF16),
                   jax.ShapeDtypeStruct((B, S, width), BF16)],
        compiler_params=pltpu.CompilerParams(dimension_semantics=("arbitrary", "arbitrary"),
                                             vmem_limit_bytes=VMEM_LIMIT),
        name="mla_proj",
    )(h, positions.reshape(B, S, 1), row(ng), win, row(g_q), wuq, row(g_kv), wukv, invf, sgn)

    return pl.pallas_call(
        functools.partial(_mla_attn_kernel, tq=tq, final=final),
        grid=(B, S // tq),
        in_specs=[head_spec, pl.BlockSpec((1, heads, S, LANES), lambda b, j: (b, 0, 0, 0)),
                  pl.BlockSpec((1, heads, S // tm, MLA_VROWS, tm), lambda b, j: (b, 0, 0, 0, 0)),
                  _seq_spec(tq, width), _seq_spec(tq, D), _const_spec((width, D)), _const_spec((1, D))],
        out_specs=_seq_spec(tq, D),
        out_shape=jax.ShapeDtypeStruct((B, S, D), F32),
        scratch_shapes=[pltpu.VMEM((heads, 1, tq), F32), pltpu.VMEM((heads, MLA_VROWS, tq), F32),
                        pltpu.VMEM((tq, width), F32)],
        compiler_params=pltpu.CompilerParams(dimension_semantics=("arbitrary", "arbitrary"),
                                             vmem_limit_bytes=VMEM_LIMIT),
        name="mla_attn",
    )(q, k, vt, g, h, w_out.astype(BF16), row(fg))


def kernel(x, positions, norm_g, final_g, mla_w_in, mla_g_q, mla_w_uq, mla_g_kv, mla_w_ukv, mla_w_out, gla_w_in, gla_w_gk2, gla_b_gk, gla_g_o, gla_w_out, lru_w_in, lru_conv_w, lru_conv_b, lru_w_a, lru_b_a, lru_w_x, lru_b_x, lru_lam, lru_w_out, ssd_w_in, ssd_conv_w, ssd_conv_b, ssd_dt_bias, ssd_a_log, ssd_d, ssd_g_norm, ssd_w_out):
    depth = norm_g.shape[0]
    h = x
    for i in range(depth):
        m, j = i % N_MIXERS, i // N_MIXERS
        common = dict(final=(i == depth - 1))
        if m == 0:
            h = _mla_layer(h, positions, norm_g[i], final_g, mla_w_in[j], mla_g_q[j], mla_w_uq[j],
                           mla_g_kv[j], mla_w_ukv[j], mla_w_out[j], **common)
        elif m == 1:
            h = _gla_layer(h, norm_g[i], final_g, gla_w_in[j], gla_w_gk2[j], gla_b_gk[j], gla_g_o[j],
                           gla_w_out[j], **common)
        elif m == 2:
            h = _lru_layer(h, norm_g[i], final_g, lru_w_in[j], lru_conv_w[j], lru_conv_b[j], lru_w_a[j],
                           lru_b_a[j], lru_w_x[j], lru_b_x[j], lru_lam[j], lru_w_out[j], **common)
        else:
            h = _ssd_layer(h, norm_g[i], final_g, ssd_w_in[j], ssd_conv_w[j], ssd_conv_b[j],
                           ssd_dt_bias[j], ssd_a_log[j], ssd_d[j], ssd_g_norm[j], ssd_w_out[j], **common)
    return h
```

```python
import functools

import jax
import jax.numpy as jnp
from jax import lax
from jax.experimental import pallas as pl
from jax.experimental.pallas import tpu as pltpu

F32 = jnp.float32
BF16 = jnp.bfloat16

NORM_EPS = 1e-6
N_MIXERS = 4
MLA_HEADS = 16
MLA_Q_RANK = 384
MLA_KV_RANK = 256
MLA_NOPE = 64
MLA_ROPE = 32
MLA_V = 64
MLA_VROWS = 80
MLA_LOOKAHEAD = 4
ROPE_THETA = 10000.0
GLA_HEADS = 4
GLA_GATE_RANK = 16
GLA_TAU = 16.0
GLA_CHUNK = 64
LRU_BLOCKS = 10
LRU_C = 8.0
CONV_W = 4
SSD_HEADDIM = 64
SSD_GROUPS = 8
SSD_STATE = 128
SSD_CHUNK = 64
SSD_CONV_BLOCK = 1024

LOG2_E = 1.4426950408889634
LANES = 128
SUBLANES = 8
VMEM_LIMIT = 56 * 1024 * 1024


def _dot(a, b):
    return jnp.dot(a, b, preferred_element_type=F32)


def _dot_nt(a, b):
    return lax.dot_general(a, b, (((1,), (1,)), ((), ())), preferred_element_type=F32)


def _dot_tn(a, b):
    return lax.dot_general(a, b, (((0,), (0,)), ((), ())), preferred_element_type=F32)


def _rms(x, g):
    ms = jnp.mean(x * x, axis=-1, keepdims=True)
    return x * lax.rsqrt(ms + NORM_EPS) * g


def _softplus(x):
    return jnp.maximum(x, 0.0) + jnp.log1p(jnp.exp(-jnp.abs(x)))


def _sigmoid(x):
    return 0.5 * jnp.tanh(0.5 * x) + 0.5


def _silu(x):
    hx = 0.5 * x
    return hx * jnp.tanh(hx) + hx


def _residual(x, y, fg_ref, final):
    r = x + y
    if final:
        r = _rms(r, fg_ref[...])
    return r


def _chunk_tri(n, chunk):
    r = lax.broadcasted_iota(jnp.int32, (n, n), 0)
    c = lax.broadcasted_iota(jnp.int32, (n, n), 1)
    keep = jnp.logical_and(r // chunk == c // chunk, c <= r)
    return jnp.where(keep, 1.0, 0.0).astype(BF16)


def _split_bf16(x, parts):
    out = []
    for _ in range(parts - 1):
        hi = x.astype(BF16)
        out.append(hi)
        x = x - hi.astype(F32)
    out.append(x.astype(BF16))
    return out


def _const_spec(shape):
    n = len(shape)
    return pl.BlockSpec(shape, lambda *_: (0,) * n, pipeline_mode=pl.Buffered(1))


def _seq_spec(tm, d):
    return pl.BlockSpec((None, tm, d), lambda b, j: (b, j, 0))


def _cast_cols_kernel(w_ref, *o_refs, splits):
    for o_ref, (c0, c1) in zip(o_refs, splits):
        n = c1 - c0
        o_ref[:, :n] = w_ref[:, c0:c1].astype(BF16)
        if o_ref.shape[1] > n:
            o_ref[:, n:] = jnp.zeros((o_ref.shape[0], o_ref.shape[1] - n), BF16)


def _cast_cols(w_stack, layer, splits, widths, rows=256):
    _, d, cols = w_stack.shape
    return pl.pallas_call(
        functools.partial(_cast_cols_kernel, splits=splits),
        grid=(d // rows,),
        in_specs=[pl.BlockSpec((None, rows, cols), lambda i: (layer, i, 0))],
        out_specs=[pl.BlockSpec((rows, w), lambda i: (i, 0)) for w in widths],
        out_shape=[jax.ShapeDtypeStruct((d, w), BF16) for w in widths],
        compiler_params=pltpu.CompilerParams(vmem_limit_bytes=VMEM_LIMIT),
        name="cast_cols",
    )(w_stack)


def _lru_kernel(h_ref, ng_ref, win_ref, cw_ref, cb_ref, wax_ref, ba_ref, bx_ref, lam_ref, wout_ref,
                fg_ref, o_ref, ubuf, abuf, bbuf, gbuf, hst, *, tm, nb, width, final):
    hist = (CONV_W - 1) * nb
    tt = tm // nb

    @pl.when(pl.program_id(0) == 0)
    def _():
        ubuf[0:hist, :] = jnp.zeros((hist, width), F32)
        hst[...] = jnp.zeros_like(hst)

    x = pltpu.einshape("btd->tbd", h_ref[...]).reshape(tm, h_ref.shape[-1])
    un = _rms(x, ng_ref[...]).astype(BF16)
    ubuf[hist:hist + tm, :] = _dot(un, win_ref[:, width:])
    gbuf[...] = _silu(_dot(un, win_ref[:, :width]))
    cw = cw_ref[...]
    conv = cb_ref[...] + cw[CONV_W - 1:CONV_W] * ubuf[hist:hist + tm, :]
    for k in range(CONV_W - 1):
        d = (CONV_W - 1 - k) * nb
        conv = conv + cw[k:k + 1] * ubuf[hist - d:hist - d + tm, :]
    ubuf[0:hist, :] = ubuf[tm:tm + hist, :]

    sp = _softplus(-lam_ref[...])
    for n in range(width // LANES):
        sl = slice(n * LANES, (n + 1) * LANES)
        cn = conv[:, sl]
        ra = _dot(cn.astype(BF16), wax_ref[n])
        r = _sigmoid(ra[:, :LANES] + ba_ref[:, sl])
        i = _sigmoid(ra[:, LANES:] + bx_ref[:, sl])
        log_a = -LRU_C * r * sp[:, sl]
        th = jnp.tanh(log_a)
        u = -2.0 * th
        mult = jnp.where(u > 0.0, u * lax.rsqrt(u * (1.0 - th)), 0.0)
        abuf[:, sl] = jnp.exp(log_a)
        bbuf[:, sl] = mult * (i * cn)

    hs = hst[...]
    for t in range(tm // nb):
        rs = slice(t * nb, (t + 1) * nb)
        hs = abuf[rs, :] * hs + bbuf[rs, :]
        bbuf[rs, :] = hs
    hst[...] = hs
    y = (bbuf[...] * gbuf[...]).astype(BF16)
    out = _residual(x, _dot(y, wout_ref[...]), fg_ref, final)
    o_ref[...] = pltpu.einshape("tbd->btd", out.reshape(tt, nb, out.shape[-1]))


def _lru_layer(h, ng, fg, w_in, conv_w, conv_b, w_a, b_a, w_x, b_x, lam, w_out, *, final, tm=512):
    B, S, D = h.shape
    assert B == SUBLANES, "time-major RG-LRU tiling needs one sublane tile per time step"
    width = w_out.shape[0]
    wax = jnp.concatenate([w_a, w_x], axis=-1).astype(BF16)
    row = lambda v: v.reshape(1, -1).astype(F32)
    kern = functools.partial(_lru_kernel, tm=tm, nb=B, width=width, final=final)
    hist = (CONV_W - 1) * B
    tile = pl.BlockSpec((B, tm // B, D), lambda j: (0, j, 0))
    return pl.pallas_call(
        kern,
        grid=(S * B // tm,),
        in_specs=[tile, _const_spec((1, D)), _const_spec((D, 2 * width)),
                  _const_spec((CONV_W, width)), _const_spec((1, width)), _const_spec(wax.shape),
                  _const_spec((1, width)), _const_spec((1, width)), _const_spec((1, width)),
                  _const_spec((width, D)), _const_spec((1, D))],
        out_specs=tile,
        out_shape=jax.ShapeDtypeStruct((B, S, D), F32),
        scratch_shapes=[pltpu.VMEM((tm + hist, width), F32), pltpu.VMEM((tm, width), F32),
                        pltpu.VMEM((tm, width), F32), pltpu.VMEM((tm, width), F32),
                        pltpu.VMEM((B, width), F32)],
        compiler_params=pltpu.CompilerParams(dimension_semantics=("arbitrary",),
                                             vmem_limit_bytes=VMEM_LIMIT),
        name="lru_layer",
    )(h, row(ng), w_in.astype(BF16), conv_w.astype(F32), row(conv_b), wax, row(b_a), row(b_x),
      row(lam), w_out.astype(BF16), row(fg))


def _gla_kernel(h_ref, ng_ref, wq_ref, wgk_ref, wgk2_ref, bgk_ref, go_ref, wout_ref, fg_ref, o_ref,
                sst, obuf, gbuf, *, tm, key, val, final):
    heads, chunk = GLA_HEADS, GLA_CHUNK
    dk, dv = key // heads, val // heads

    @pl.when(pl.program_id(1) == 0)
    def _():
        sst[...] = jnp.zeros_like(sst)

    x = h_ref[...]
    un = _rms(x, ng_ref[...]).astype(BF16)
    q = _dot(un, wq_ref[:, :key]) * dk ** -0.5
    k = _dot(un, wq_ref[:, key:2 * key])
    gk = _dot(un, wgk_ref[...]).astype(BF16)
    log_a = -_softplus(-(_dot(gk, wgk2_ref[...]) + bgk_ref[...])) * (1.0 / GLA_TAU)
    v = _dot(un, wq_ref[:, 2 * key:2 * key + val]).astype(BF16)
    gbuf[...] = _silu(_dot(un, wq_ref[:, 2 * key + val:]))
    tri = _chunk_tri(tm, chunk)
    b = sum(_dot(tri, part) for part in _split_bf16(log_a, 2))

    rr = lax.broadcasted_iota(jnp.int32, (chunk, chunk), 0)
    cc = lax.broadcasted_iota(jnp.int32, (chunk, chunk), 1)
    causal = cc <= rr
    nch = tm // chunk
    ksl = [slice(hh * dk, (hh + 1) * dk) for hh in range(heads)]
    vsl = [slice(hh * dv, (hh + 1) * dv) for hh in range(heads)]
    rows = [slice(ci * chunk, (ci + 1) * chunk) for ci in range(nch)]
    qt, dec, att, ds = [], [], [], []
    for rs in rows:
        bc = b[rs]
        bl = bc[chunk - 1:chunk, :]
        qt.append((q[rs] * jnp.exp(bc)).astype(BF16))
        kt = (k[rs] * jnp.exp(-bc)).astype(BF16)
        ke = (k[rs] * jnp.exp(bl - bc)).astype(BF16)
        dec.append(jnp.exp(bl))
        att.append([_dot_nt(qt[-1][:, ks], kt[:, ks]) for ks in ksl])
        ds.append([_dot_tn(v[rs, vsl[hh]], ke[:, ksl[hh]]) for hh in range(heads)])
    for hh in range(heads):
        s_in = sst[hh]
        for ci, rs in enumerate(rows):
            p = jnp.where(causal, att[ci][hh], 0.0).astype(BF16)
            obuf[rs, vsl[hh]] = _dot(p, v[rs, vsl[hh]]) + _dot_nt(qt[ci][:, ksl[hh]], s_in.astype(BF16))
            s_in = dec[ci][:, ksl[hh]] * s_in + ds[ci][hh]
        sst[hh] = s_in

    parts = []
    for hh in range(heads):
        oh = obuf[:, hh * dv:(hh + 1) * dv]
        parts.append(oh * lax.rsqrt(jnp.mean(oh * oh, axis=-1, keepdims=True) + NORM_EPS))
    on = jnp.concatenate(parts, axis=1) * go_ref[...]
    y = (on * gbuf[...]).astype(BF16)
    o_ref[...] = _residual(x, _dot(y, wout_ref[...]), fg_ref, final)


def _gla_layer(h, ng, fg, w_in_stack, layer, w_gk2, b_gk, g_o, w_out, *, final, tm=256):
    B, S, D = h.shape
    val = w_out.shape[0]
    key = (w_in_stack.shape[2] - 2 * val - GLA_GATE_RANK) // 2
    main = 2 * key + 2 * val
    pad = LANES - GLA_GATE_RANK
    wq, wgk = _cast_cols(w_in_stack, layer, ((0, main), (main, main + GLA_GATE_RANK)), (main, LANES))
    wgk2 = jnp.pad(w_gk2.astype(BF16), ((0, pad), (0, 0)))
    row = lambda v: v.reshape(1, -1).astype(F32)
    kern = functools.partial(_gla_kernel, tm=tm, key=key, val=val, final=final)
    return pl.pallas_call(
        kern,
        grid=(B, S // tm),
        in_specs=[_seq_spec(tm, D), _const_spec((1, D)), _const_spec((D, main)), _const_spec((D, LANES)),
                  _const_spec((LANES, key)), _const_spec((1, key)), _const_spec((1, val)),
                  _const_spec((val, D)), _const_spec((1, D))],
        out_specs=_seq_spec(tm, D),
        out_shape=jax.ShapeDtypeStruct((B, S, D), F32),
        scratch_shapes=[pltpu.VMEM((GLA_HEADS, val // GLA_HEADS, key // GLA_HEADS), F32),
                        pltpu.VMEM((tm, val), F32), pltpu.VMEM((tm, val), F32)],
        compiler_params=pltpu.CompilerParams(dimension_semantics=("arbitrary", "arbitrary"),
                                             vmem_limit_bytes=VMEM_LIMIT),
        name="gla_layer",
    )(h, row(ng), wq, wgk, wgk2, row(b_gk), row(jnp.tile(g_o, GLA_HEADS)), w_out.astype(BF16), row(fg))


def _ssd_kernel(h_ref, ng_ref, win_ref, cw_ref, cb_ref, dtb_ref, alog_ref, dexp_ref, gn_ref, e_ref,
                wout_ref, fg_ref, o_ref, xbuf, cbuf, csx, dtx, cst, dtt, zbuf, ybuf, sst, *, tm, inner, final):
    groups, ns, chunk = SSD_GROUPS, SSD_STATE, SSD_CHUNK
    gw = inner // groups
    conv_dim = inner + 2 * groups * ns
    half = SSD_HEADDIM

    @pl.when(pl.program_id(1) == 0)
    def _():
        xbuf[:, 0:SUBLANES, :] = jnp.zeros((conv_dim // LANES, SUBLANES, LANES), F32)
        sst[...] = jnp.zeros_like(sst)

    x = h_ref[...]
    un = _rms(x, ng_ref[...]).astype(BF16)

    dt_raw = _dot(un, win_ref[:, inner + conv_dim:])
    dt = _softplus(dt_raw + dtb_ref[...])
    da = dt * (-jnp.exp(alog_ref[...]))
    tri = _chunk_tri(tm, chunk)
    cs = sum(_dot(tri, part) for part in _split_bf16(da, 3))
    e = e_ref[...]
    csx[...] = sum(_dot(part, e) for part in _split_bf16(cs, 2))
    dtx[...] = _dot(dt.astype(BF16), e)
    cst[...] = cs.T
    dtt[...] = dt.T

    for c0 in range(0, conv_dim, SSD_CONV_BLOCK):
        proj = _dot(un, win_ref[:, inner + c0:inner + c0 + SSD_CONV_BLOCK])
        for c in range(SSD_CONV_BLOCK // LANES):
            xbuf[c0 // LANES + c, SUBLANES:SUBLANES + tm, :] = proj[:, c * LANES:(c + 1) * LANES]
        for c in range(SSD_CONV_BLOCK // LANES):
            t = c0 // LANES + c
            cl = slice(c0 + c * LANES, c0 + (c + 1) * LANES)
            conv = cb_ref[:, cl] + cw_ref[CONV_W - 1:CONV_W, cl] * xbuf[t, SUBLANES:SUBLANES + tm, :]
            for k in range(CONV_W - 1):
                d = CONV_W - 1 - k
                conv = conv + cw_ref[k:k + 1, cl] * xbuf[t, pl.ds(SUBLANES - d, tm, stride=1), :]
            xbuf[t, 0:SUBLANES, :] = xbuf[t, tm:tm + SUBLANES, :]
            cbuf[:, cl] = _silu(conv)
    zbuf[...] = _silu(_dot(un, win_ref[:, 0:inner]))

    ii = lax.broadcasted_iota(jnp.int32, (chunk, LANES), 0)
    ll = lax.broadcasted_iota(jnp.int32, (chunk, LANES), 1)
    causal2 = ll % half <= ii
    low = ll < half
    low_row = lax.broadcasted_iota(jnp.int32, (1, LANES), 1) < half
    dexp = dexp_ref[...]

    def lane_pair(ref, h0, ci):
        t0 = (ci * chunk // LANES) * LANES
        a = ref[h0:h0 + 1, t0:t0 + LANES]
        b = ref[h0 + 1:h0 + 2, t0:t0 + LANES]
        if (ci * chunk) % LANES == 0:
            return jnp.where(low_row, a, pltpu.roll(b, half, 1))
        return jnp.where(low_row, pltpu.roll(a, half, 1), b)

    for ci in range(tm // chunk):
        rs = slice(ci * chunk, (ci + 1) * chunk)
        cs_last = csx[(ci + 1) * chunk - 1:(ci + 1) * chunk, :]
        bgs, cgs, cb2s, y_offs, s_prevs = [], [], [], [], []
        for g in range(groups):
            bg = cbuf[rs, inner + g * ns:inner + (g + 1) * ns].astype(BF16)
            cg = cbuf[rs, inner + (groups + g) * ns:inner + (groups + g + 1) * ns].astype(BF16)
            s_prev = sst[g]
            bgs.append(bg)
            cgs.append(cg)
            s_prevs.append(s_prev)
            cb2s.append(_dot_nt(cg, jnp.concatenate([bg, bg], axis=0)))
            y_offs.append(_dot(cg, s_prev.astype(BF16)))
        for g in range(groups):
            gl = slice(g * gw, (g + 1) * gw)
            xg = cbuf[rs, gl]
            csg = csx[rs, gl]
            yd = []
            for pp in range(gw // LANES):
                ps = slice(pp * LANES, (pp + 1) * LANES)
                h0 = (g * gw + pp * LANES) // SSD_HEADDIM
                row = lane_pair(cst, h0, ci)
                dtrow = lane_pair(dtt, h0, ci)
                lm = jnp.where(causal2, jnp.exp(csg[:, ps] - row), 0.0)
                w = (cb2s[g] * lm * dtrow).astype(BF16)
                xp = xg[:, ps]
                bd = jnp.concatenate([jnp.where(low, xp, 0.0), jnp.where(low, 0.0, xp)],
                                     axis=0).astype(BF16)
                yd.append(_dot(w, bd))
            ybuf[rs, gl] = jnp.concatenate(yd, axis=1) + y_offs[g] * jnp.exp(csg) + dexp[:, gl] * xg
        for g in range(groups):
            gl = slice(g * gw, (g + 1) * gw)
            cl = cs_last[:, gl]
            w1 = jnp.exp(cl - csx[rs, gl]) * dtx[rs, gl]
            sst[g] = s_prevs[g] * jnp.exp(cl) + _dot_tn(bgs[g], (cbuf[rs, gl] * w1).astype(BF16))

    acc = x
    for g in range(groups):
        gl = slice(g * gw, (g + 1) * gw)
        yg = ybuf[:, gl] * zbuf[:, gl]
        yn = yg * lax.rsqrt(jnp.mean(yg * yg, axis=-1, keepdims=True) + NORM_EPS) * gn_ref[:, gl]
        acc = acc + _dot(yn.astype(BF16), wout_ref[gl, :])
    if final:
        acc = _rms(acc, fg_ref[...])
    o_ref[...] = acc


def _ssd_layer(h, ng, fg, w_in_stack, layer, conv_w, conv_b, dt_bias, a_log, d_skip, g_norm, w_out, *, final,
               tm=256):
    B, S, D = h.shape
    inner = w_out.shape[0]
    heads = inner // SSD_HEADDIM
    conv_dim = inner + 2 * SSD_GROUPS * SSD_STATE
    pad = LANES - heads
    cols = w_in_stack.shape[2]
    (win,) = _cast_cols(w_in_stack, layer, ((0, cols),), (cols + pad,))
    row = lambda v: v.reshape(1, -1).astype(F32)
    padrow = lambda v: jnp.pad(v.astype(F32), (0, pad)).reshape(1, LANES)
    expand = (lax.broadcasted_iota(jnp.int32, (LANES, inner), 1) // SSD_HEADDIM
              == lax.broadcasted_iota(jnp.int32, (LANES, inner), 0)).astype(BF16)
    kern = functools.partial(_ssd_kernel, tm=tm, inner=inner, final=final)
    gw = inner // SSD_GROUPS
    return pl.pallas_call(
        kern,
        grid=(B, S // tm),
        in_specs=[_seq_spec(tm, D), _const_spec((1, D)), _const_spec(win.shape),
                  _const_spec((CONV_W, conv_dim)), _const_spec((1, conv_dim)), _const_spec((1, LANES)),
                  _const_spec((1, LANES)), _const_spec((1, inner)), _const_spec((1, inner)),
                  _const_spec((LANES, inner)), _const_spec((inner, D)), _const_spec((1, D))],
        out_specs=_seq_spec(tm, D),
        out_shape=jax.ShapeDtypeStruct((B, S, D), F32),
        scratch_shapes=[pltpu.VMEM((conv_dim // LANES, tm + SUBLANES, LANES), F32), pltpu.VMEM((tm, conv_dim), F32),
                        pltpu.VMEM((tm, inner), F32), pltpu.VMEM((tm, inner), F32),
                        pltpu.VMEM((LANES, tm), F32), pltpu.VMEM((LANES, tm), F32),
                        pltpu.VMEM((tm, inner), F32), pltpu.VMEM((tm, inner), F32),
                        pltpu.VMEM((SSD_GROUPS, SSD_STATE, gw), F32)],
        compiler_params=pltpu.CompilerParams(dimension_semantics=("arbitrary", "arbitrary"),
                                             vmem_limit_bytes=VMEM_LIMIT),
        name="ssd_layer",
    )(h, row(ng), win, conv_w.astype(F32), row(conv_b), padrow(dt_bias), padrow(a_log),
      row(jnp.repeat(d_skip, SSD_HEADDIM)), row(g_norm), expand, w_out.astype(BF16), row(fg))


def _mla_proj_kernel(h_ref, pos_ref, ng_ref, win_ref, gq_ref, wuq_ref, gkv_ref, wukv_ref, invf_ref,
                     sgn_ref, q_ref, k_ref, v_ref, g_ref, *, tm):
    heads = MLA_HEADS
    kw = heads * LANES
    x = h_ref[...]
    un = _rms(x, ng_ref[...]).astype(BF16)
    t = _dot(un, win_ref[...])
    c_q = t[:, :MLA_Q_RANK]
    c_kv = t[:, MLA_Q_RANK:MLA_Q_RANK + MLA_KV_RANK]
    g0 = MLA_Q_RANK + MLA_KV_RANK
    gate = t[:, g0:g0 + heads * MLA_V]
    k_r = t[:, g0 + heads * MLA_V:]
    g_ref[...] = _silu(gate).astype(BF16)

    ang = pos_ref[...].astype(F32) * invf_ref[...]
    cos = jnp.cos(ang)
    sin = jnp.sin(ang) * sgn_ref[...]
    lane = lax.broadcasted_iota(jnp.int32, (tm, LANES), 1)
    first_half = lane < MLA_NOPE + MLA_ROPE // 2

    def rope(a):
        swapped = jnp.where(first_half, pltpu.roll(a, LANES - MLA_ROPE // 2, 1),
                            pltpu.roll(a, MLA_ROPE // 2, 1))
        return a * cos + swapped * sin

    k_rope = rope(k_r)
    qn = _dot(_rms(c_q, gq_ref[...]).astype(BF16), wuq_ref[...])
    kvn = _dot(_rms(c_kv, gkv_ref[...]).astype(BF16), wukv_ref[...])
    scale = (MLA_NOPE + MLA_ROPE) ** -0.5 * LOG2_E
    for hh in range(heads):
        hs = slice(hh * LANES, (hh + 1) * LANES)
        q_ref[0, hh] = (rope(qn[:, hs]) * scale).astype(BF16)
        k_ref[0, hh] = (kvn[:, hs] + k_rope).astype(BF16)
    ones_row = jnp.where(lax.broadcasted_iota(jnp.int32, (MLA_VROWS - MLA_V, tm), 0) == 0, 1.0, 0.0)
    for hp in range(heads // 2):
        vt = kvn[:, kw + hp * LANES:kw + (hp + 1) * LANES].T
        for e in range(2):
            v_ref[0, 2 * hp + e, 0] = jnp.concatenate(
                [vt[e * MLA_V:(e + 1) * MLA_V], ones_row], axis=0).astype(BF16)


def _mla_attn_kernel(q_ref, k_ref, vt_ref, g_ref, h_ref, wout_ref, fg_ref, o_ref, m_sc, acc_sc, obuf,
                     *, tq, final):
    heads = MLA_HEADS
    j = pl.program_id(1)
    key_idx = lax.broadcasted_iota(jnp.int32, (tq, tq), 0)
    qry_idx = lax.broadcasted_iota(jnp.int32, (tq, tq), 1)
    diag = key_idx <= qry_idx
    m_sc[...] = jnp.full(m_sc.shape, -jnp.inf, F32)
    acc_sc[...] = jnp.zeros(acc_sc.shape, F32)

    def kv_tile(kt, masked):
        k0 = pl.multiple_of(kt * tq, tq)

        def scores(hh):
            return _dot_nt(k_ref[0, hh, pl.ds(k0, tq), :], q_ref[0, hh])

        ready = [scores(hh) for hh in range(MLA_LOOKAHEAD)]
        for hh in range(heads):
            if hh % MLA_LOOKAHEAD == 0:
                nxt = range(hh + MLA_LOOKAHEAD, min(hh + 2 * MLA_LOOKAHEAD, heads))
                ready.extend(scores(h2) for h2 in nxt)
            st = ready.pop(0)
            if masked:
                st = jnp.where(diag, st, -jnp.inf)
            m_old = m_sc[hh]
            m_new = jnp.maximum(m_old, jnp.max(st, axis=0, keepdims=True))
            pt = jnp.exp2(st - m_new).astype(BF16)
            acc_sc[hh] = jnp.exp2(m_old - m_new) * acc_sc[hh] + _dot(vt_ref[0, hh, kt], pt)
            m_sc[hh] = m_new

    def full_tile(kt, carry):
        kv_tile(kt, False)
        return carry

    lax.fori_loop(0, j, full_tile, 0)
    kv_tile(j, True)

    for hp in range(heads // 2):
        halves = []
        for e in range(2):
            acc = acc_sc[2 * hp + e]
            halves.append(acc[:MLA_V] * (1.0 / acc[MLA_V:MLA_V + 1]))
        obuf[:, hp * LANES:(hp + 1) * LANES] = jnp.concatenate(halves, axis=0).T
    y = (obuf[...] * g_ref[...].astype(F32)).astype(BF16)
    o_ref[...] = _residual(h_ref[...], _dot(y, wout_ref[...]), fg_ref, final)


def _mla_layer(h, positions, ng, fg, w_in, g_q, w_uq, g_kv, w_ukv, w_out, *, final, tq=256):
    B, S, D = h.shape
    heads, half = MLA_HEADS, MLA_ROPE // 2
    qk = MLA_NOPE + MLA_ROPE
    width = heads * MLA_V
    c0 = MLA_Q_RANK + MLA_KV_RANK
    w_in = w_in.astype(BF16)
    w_kr = jnp.pad(w_in[:, c0:c0 + MLA_ROPE], ((0, 0), (MLA_NOPE, LANES - qk)))
    win = jnp.concatenate([w_in[:, :c0], w_in[:, c0 + MLA_ROPE:], w_kr], axis=1)
    wuq = jnp.pad(w_uq.reshape(MLA_Q_RANK, heads, qk), ((0, 0), (0, 0), (0, LANES - qk)))
    wuq = wuq.reshape(MLA_Q_RANK, heads * LANES).astype(BF16)
    wkv = w_ukv.reshape(MLA_KV_RANK, heads, MLA_NOPE + MLA_V)
    wk = jnp.pad(wkv[:, :, :MLA_NOPE], ((0, 0), (0, 0), (0, LANES - MLA_NOPE)))
    wukv = jnp.concatenate([wk.reshape(MLA_KV_RANK, heads * LANES),
                            wkv[:, :, MLA_NOPE:].reshape(MLA_KV_RANK, width)], axis=1).astype(BF16)
    inv_freq = ROPE_THETA ** (-jnp.arange(0, MLA_ROPE, 2, dtype=F32) / MLA_ROPE)
    zeros = lambda n: jnp.zeros((n,), F32)
    invf = jnp.concatenate([zeros(MLA_NOPE), inv_freq, inv_freq, zeros(LANES - qk)]).reshape(1, LANES)
    sgn = jnp.concatenate([jnp.ones((MLA_NOPE,), F32), -jnp.ones((half,), F32),
                           jnp.ones((LANES - MLA_NOPE - half,), F32)]).reshape(1, LANES)
    row = lambda v: v.reshape(1, -1).astype(F32)

    tm = tq
    head_spec = pl.BlockSpec((1, heads, tm, LANES), lambda b, j: (b, 0, j, 0))
    vt_spec = pl.BlockSpec((1, heads, 1, MLA_VROWS, tm), lambda b, j: (b, 0, j, 0, 0))
    q, k, vt, g = pl.pallas_call(
        functools.partial(_mla_proj_kernel, tm=tm),
        grid=(B, S // tm),
        in_specs=[_seq_spec(tm, D), _seq_spec(tm, 1), _const_spec((1, D)), _const_spec(win.shape),
                  _const_spec((1, MLA_Q_RANK)), _const_spec(wuq.shape), _const_spec((1, MLA_KV_RANK)),
                  _const_spec(wukv.shape), _const_spec((1, LANES)), _const_spec((1, LANES))],
        out_specs=[head_spec, head_spec, vt_spec, _seq_spec(tm, width)],
        out_shape=[jax.ShapeDtypeStruct((B, heads, S, LANES), BF16),
                   jax.ShapeDtypeStruct((B, heads, S, LANES), BF16),
                   jax.ShapeDtypeStruct((B, heads, S // tm, MLA_VROWS, tm), BF16),
                   jax.ShapeDtypeStruct((B, S, width), BF16)],
        compiler_params=pltpu.CompilerParams(dimension_semantics=("arbitrary", "arbitrary"),
                                             vmem_limit_bytes=VMEM_LIMIT),
        name="mla_proj",
    )(h, positions.reshape(B, S, 1), row(ng), win, row(g_q), wuq, row(g_kv), wukv, invf, sgn)

    return pl.pallas_call(
        functools.partial(_mla_attn_kernel, tq=tq, final=final),
        grid=(B, S // tq),
        in_specs=[head_spec, pl.BlockSpec((1, heads, S, LANES), lambda b, j: (b, 0, 0, 0)),
                  pl.BlockSpec((1, heads, S // tm, MLA_VROWS, tm), lambda b, j: (b, 0, 0, 0, 0)),
                  _seq_spec(tq, width), _seq_spec(tq, D), _const_spec((width, D)), _const_spec((1, D))],
        out_specs=_seq_spec(tq, D),
        out_shape=jax.ShapeDtypeStruct((B, S, D), F32),
        scratch_shapes=[pltpu.VMEM((heads, 1, tq), F32), pltpu.VMEM((heads, MLA_VROWS, tq), F32),
                        pltpu.VMEM((tq, width), F32)],
        compiler_params=pltpu.CompilerParams(dimension_semantics=("arbitrary", "arbitrary"),
                                             vmem_limit_bytes=VMEM_LIMIT),
        name="mla_attn",
    )(q, k, vt, g, h, w_out.astype(BF16), row(fg))


def kernel(x, positions, norm_g, final_g, mla_w_in, mla_g_q, mla_w_uq, mla_g_kv, mla_w_ukv, mla_w_out, gla_w_in, gla_w_gk2, gla_b_gk, gla_g_o, gla_w_out, lru_w_in, lru_conv_w, lru_conv_b, lru_w_a, lru_b_a, lru_w_x, lru_b_x, lru_lam, lru_w_out, ssd_w_in, ssd_conv_w, ssd_conv_b, ssd_dt_bias, ssd_a_log, ssd_d, ssd_g_norm, ssd_w_out):
    depth = norm_g.shape[0]
    h = x
    for i in range(depth):
        m, j = i % N_MIXERS, i // N_MIXERS
        common = dict(final=(i == depth - 1))
        if m == 0:
            h = _mla_layer(h, positions, norm_g[i], final_g, mla_w_in[j], mla_g_q[j], mla_w_uq[j],
                           mla_g_kv[j], mla_w_ukv[j], mla_w_out[j], **common)
        elif m == 1:
            h = _gla_layer(h, norm_g[i], final_g, gla_w_in, j, gla_w_gk2[j], gla_b_gk[j], gla_g_o[j],
                           gla_w_out[j], **common)
        elif m == 2:
            h = _lru_layer(h, norm_g[i], final_g, lru_w_in[j], lru_conv_w[j], lru_conv_b[j], lru_w_a[j],
                           lru_b_a[j], lru_w_x[j], lru_b_x[j], lru_lam[j], lru_w_out[j], **common)
        else:
            h = _ssd_layer(h, norm_g[i], final_g, ssd_w_in, j, ssd_conv_w[j], ssd_conv_b[j],
                           ssd_dt_bias[j], ssd_a_log[j], ssd_d[j], ssd_g_norm[j], ssd_w_out[j], **common)
    return h
```

```python
import functools

import jax
import jax.numpy as jnp
from jax import lax
from jax.experimental import pallas as pl
from jax.experimental.pallas import tpu as pltpu

F32 = jnp.float32
BF16 = jnp.bfloat16

NORM_EPS = 1e-6
N_MIXERS = 4
MLA_HEADS = 16
MLA_Q_RANK = 384
MLA_KV_RANK = 256
MLA_NOPE = 64
MLA_ROPE = 32
MLA_V = 64
MLA_VROWS = 80
MLA_LOOKAHEAD = 4
ROPE_THETA = 10000.0
GLA_HEADS = 4
GLA_GATE_RANK = 16
GLA_TAU = 16.0
GLA_CHUNK = 64
LRU_BLOCKS = 10
LRU_C = 8.0
CONV_W = 4
SSD_HEADDIM = 64
SSD_GROUPS = 8
SSD_STATE = 128
SSD_CHUNK = 64
SSD_CONV_BLOCK = 1024

LOG2_E = 1.4426950408889634
LANES = 128
SUBLANES = 8
VMEM_LIMIT = 56 * 1024 * 1024


def _dot(a, b):
    return jnp.dot(a, b, preferred_element_type=F32)


def _dot_nt(a, b):
    return lax.dot_general(a, b, (((1,), (1,)), ((), ())), preferred_element_type=F32)


def _dot_tn(a, b):
    return lax.dot_general(a, b, (((0,), (0,)), ((), ())), preferred_element_type=F32)


def _rms(x, g):
    ms = jnp.mean(x * x, axis=-1, keepdims=True)
    return x * lax.rsqrt(ms + NORM_EPS) * g


def _softplus(x):
    return jnp.maximum(x, 0.0) + jnp.log1p(jnp.exp(-jnp.abs(x)))


def _sigmoid(x):
    return 0.5 * jnp.tanh(0.5 * x) + 0.5


def _silu(x):
    hx = 0.5 * x
    return hx * jnp.tanh(hx) + hx


def _residual(x, y, fg_ref, final):
    r = x + y
    if final:
        r = _rms(r, fg_ref[...])
    return r


def _chunk_tri(n, chunk):
    r = lax.broadcasted_iota(jnp.int32, (n, n), 0)
    c = lax.broadcasted_iota(jnp.int32, (n, n), 1)
    keep = jnp.logical_and(r // chunk == c // chunk, c <= r)
    return jnp.where(keep, 1.0, 0.0).astype(BF16)


def _split_bf16(x, parts):
    out = []
    for _ in range(parts - 1):
        hi = x.astype(BF16)
        out.append(hi)
        x = x - hi.astype(F32)
    out.append(x.astype(BF16))
    return out


def _const_spec(shape):
    n = len(shape)
    return pl.BlockSpec(shape, lambda *_: (0,) * n, pipeline_mode=pl.Buffered(1))


def _seq_spec(tm, d):
    return pl.BlockSpec((None, tm, d), lambda b, j: (b, j, 0))


def _lru_kernel(h_ref, ng_ref, win_ref, cw_ref, cb_ref, wax_ref, ba_ref, bx_ref, lam_ref, wout_ref,
                fg_ref, o_ref, ubuf, abuf, bbuf, gbuf, hst, *, tm, nb, width, final):
    hist = (CONV_W - 1) * nb
    tt = tm // nb

    @pl.when(pl.program_id(0) == 0)
    def _():
        ubuf[0:hist, :] = jnp.zeros((hist, width), F32)
        hst[...] = jnp.zeros_like(hst)

    x = pltpu.einshape("btd->tbd", h_ref[...]).reshape(tm, h_ref.shape[-1])
    un = _rms(x, ng_ref[...]).astype(BF16)
    ubuf[hist:hist + tm, :] = _dot(un, win_ref[:, width:])
    gbuf[...] = _silu(_dot(un, win_ref[:, :width]))
    cw = cw_ref[...]
    conv = cb_ref[...] + cw[CONV_W - 1:CONV_W] * ubuf[hist:hist + tm, :]
    for k in range(CONV_W - 1):
        d = (CONV_W - 1 - k) * nb
        conv = conv + cw[k:k + 1] * ubuf[hist - d:hist - d + tm, :]
    ubuf[0:hist, :] = ubuf[tm:tm + hist, :]

    sp = _softplus(-lam_ref[...])
    for n in range(width // LANES):
        sl = slice(n * LANES, (n + 1) * LANES)
        cn = conv[:, sl]
        ra = _dot(cn.astype(BF16), wax_ref[n])
        r = _sigmoid(ra[:, :LANES] + ba_ref[:, sl])
        i = _sigmoid(ra[:, LANES:] + bx_ref[:, sl])
        log_a = -LRU_C * r * sp[:, sl]
        th = jnp.tanh(log_a)
        u = -2.0 * th
        mult = jnp.where(u > 0.0, u * lax.rsqrt(u * (1.0 - th)), 0.0)
        abuf[:, sl] = jnp.exp(log_a)
        bbuf[:, sl] = mult * (i * cn)

    hs = hst[...]
    for t in range(tm // nb):
        rs = slice(t * nb, (t + 1) * nb)
        hs = abuf[rs, :] * hs + bbuf[rs, :]
        bbuf[rs, :] = hs
    hst[...] = hs
    y = (bbuf[...] * gbuf[...]).astype(BF16)
    out = _residual(x, _dot(y, wout_ref[...]), fg_ref, final)
    o_ref[...] = pltpu.einshape("tbd->btd", out.reshape(tt, nb, out.shape[-1]))


def _lru_layer(h, ng, fg, w_in, conv_w, conv_b, w_a, b_a, w_x, b_x, lam, w_out, *, final, tm=512):
    B, S, D = h.shape
    assert B == SUBLANES, "time-major RG-LRU tiling needs one sublane tile per time step"
    width = w_out.shape[0]
    wax = jnp.concatenate([w_a, w_x], axis=-1).astype(BF16)
    row = lambda v: v.reshape(1, -1).astype(F32)
    kern = functools.partial(_lru_kernel, tm=tm, nb=B, width=width, final=final)
    hist = (CONV_W - 1) * B
    tile = pl.BlockSpec((B, tm // B, D), lambda j: (0, j, 0))
    return pl.pallas_call(
        kern,
        grid=(S * B // tm,),
        in_specs=[tile, _const_spec((1, D)), _const_spec((D, 2 * width)),
                  _const_spec((CONV_W, width)), _const_spec((1, width)), _const_spec(wax.shape),
                  _const_spec((1, width)), _const_spec((1, width)), _const_spec((1, width)),
                  _const_spec((width, D)), _const_spec((1, D))],
        out_specs=tile,
        out_shape=jax.ShapeDtypeStruct((B, S, D), F32),
        scratch_shapes=[pltpu.VMEM((tm + hist, width), F32), pltpu.VMEM((tm, width), F32),
                        pltpu.VMEM((tm, width), F32), pltpu.VMEM((tm, width), F32),
                        pltpu.VMEM((B, width), F32)],
        compiler_params=pltpu.CompilerParams(dimension_semantics=("arbitrary",),
                                             vmem_limit_bytes=VMEM_LIMIT),
        name="lru_layer",
    )(h, row(ng), w_in.astype(BF16), conv_w.astype(F32), row(conv_b), wax, row(b_a), row(b_x),
      row(lam), w_out.astype(BF16), row(fg))


def _gla_kernel(h_ref, ng_ref, wq_ref, wgk_ref, wgk2_ref, bgk_ref, go_ref, wout_ref, fg_ref, o_ref,
                sst, obuf, gbuf, *, tm, key, val, final):
    heads, chunk = GLA_HEADS, GLA_CHUNK
    dk, dv = key // heads, val // heads

    @pl.when(pl.program_id(1) == 0)
    def _():
        sst[...] = jnp.zeros_like(sst)

    x = h_ref[...]
    un = _rms(x, ng_ref[...]).astype(BF16)
    q = _dot(un, wq_ref[:, :key]) * dk ** -0.5
    k = _dot(un, wq_ref[:, key:2 * key])
    gk = _dot(un, wgk_ref[...]).astype(BF16)
    log_a = -_softplus(-(_dot(gk, wgk2_ref[...]) + bgk_ref[...])) * (1.0 / GLA_TAU)
    v = _dot(un, wq_ref[:, 2 * key:2 * key + val]).astype(BF16)
    gbuf[...] = _silu(_dot(un, wq_ref[:, 2 * key + val:]))
    tri = _chunk_tri(tm, chunk)
    b = sum(_dot(tri, part) for part in _split_bf16(log_a, 2))

    rr = lax.broadcasted_iota(jnp.int32, (chunk, chunk), 0)
    cc = lax.broadcasted_iota(jnp.int32, (chunk, chunk), 1)
    causal = cc <= rr
    nch = tm // chunk
    ksl = [slice(hh * dk, (hh + 1) * dk) for hh in range(heads)]
    vsl = [slice(hh * dv, (hh + 1) * dv) for hh in range(heads)]
    rows = [slice(ci * chunk, (ci + 1) * chunk) for ci in range(nch)]
    qt, dec, att, ds = [], [], [], []
    for rs in rows:
        bc = b[rs]
        bl = bc[chunk - 1:chunk, :]
        qt.append((q[rs] * jnp.exp(bc)).astype(BF16))
        kt = (k[rs] * jnp.exp(-bc)).astype(BF16)
        ke = (k[rs] * jnp.exp(bl - bc)).astype(BF16)
        dec.append(jnp.exp(bl))
        att.append([_dot_nt(qt[-1][:, ks], kt[:, ks]) for ks in ksl])
        ds.append([_dot_tn(v[rs, vsl[hh]], ke[:, ksl[hh]]) for hh in range(heads)])
    for hh in range(heads):
        s_in = sst[hh]
        for ci, rs in enumerate(rows):
            p = jnp.where(causal, att[ci][hh], 0.0).astype(BF16)
            obuf[rs, vsl[hh]] = _dot(p, v[rs, vsl[hh]]) + _dot_nt(qt[ci][:, ksl[hh]], s_in.astype(BF16))
            s_in = dec[ci][:, ksl[hh]] * s_in + ds[ci][hh]
        sst[hh] = s_in

    parts = []
    for hh in range(heads):
        oh = obuf[:, hh * dv:(hh + 1) * dv]
        parts.append(oh * lax.rsqrt(jnp.mean(oh * oh, axis=-1, keepdims=True) + NORM_EPS))
    on = jnp.concatenate(parts, axis=1) * go_ref[...]
    y = (on * gbuf[...]).astype(BF16)
    o_ref[...] = _residual(x, _dot(y, wout_ref[...]), fg_ref, final)


def _gla_layer(h, ng, fg, w_in, w_gk2, b_gk, g_o, w_out, *, final, tm=256):
    B, S, D = h.shape
    val = w_out.shape[0]
    key = (w_in.shape[1] - 2 * val - GLA_GATE_RANK) // 2
    main = 2 * key + 2 * val
    pad = LANES - GLA_GATE_RANK
    w_in = w_in.astype(BF16)
    wq = w_in[:, :main]
    wgk = jnp.pad(w_in[:, main:], ((0, 0), (0, pad)))
    wgk2 = jnp.pad(w_gk2.astype(BF16), ((0, pad), (0, 0)))
    row = lambda v: v.reshape(1, -1).astype(F32)
    kern = functools.partial(_gla_kernel, tm=tm, key=key, val=val, final=final)
    return pl.pallas_call(
        kern,
        grid=(B, S // tm),
        in_specs=[_seq_spec(tm, D), _const_spec((1, D)), _const_spec((D, main)), _const_spec((D, LANES)),
                  _const_spec((LANES, key)), _const_spec((1, key)), _const_spec((1, val)),
                  _const_spec((val, D)), _const_spec((1, D))],
        out_specs=_seq_spec(tm, D),
        out_shape=jax.ShapeDtypeStruct((B, S, D), F32),
        scratch_shapes=[pltpu.VMEM((GLA_HEADS, val // GLA_HEADS, key // GLA_HEADS), F32),
                        pltpu.VMEM((tm, val), F32), pltpu.VMEM((tm, val), F32)],
        compiler_params=pltpu.CompilerParams(dimension_semantics=("arbitrary", "arbitrary"),
                                             vmem_limit_bytes=VMEM_LIMIT),
        name="gla_layer",
    )(h, row(ng), wq, wgk, wgk2, row(b_gk), row(jnp.tile(g_o, GLA_HEADS)), w_out.astype(BF16), row(fg))


def _ssd_kernel(h_ref, ng_ref, win_ref, cw_ref, cb_ref, dtb_ref, alog_ref, dexp_ref, gn_ref, e_ref,
                wout_ref, fg_ref, o_ref, xbuf, cbuf, csx, dtx, cst, dtt, zbuf, ybuf, sst, *, tm, inner, final):
    groups, ns, chunk = SSD_GROUPS, SSD_STATE, SSD_CHUNK
    gw = inner // groups
    conv_dim = inner + 2 * groups * ns
    half = SSD_HEADDIM

    @pl.when(pl.program_id(1) == 0)
    def _():
        xbuf[:, 0:SUBLANES, :] = jnp.zeros((conv_dim // LANES, SUBLANES, LANES), F32)
        sst[...] = jnp.zeros_like(sst)

    x = h_ref[...]
    un = _rms(x, ng_ref[...]).astype(BF16)

    dt_raw = _dot(un, win_ref[:, inner + conv_dim:])
    dt = _softplus(dt_raw + dtb_ref[...])
    da = dt * (-jnp.exp(alog_ref[...]))
    tri = _chunk_tri(tm, chunk)
    cs = sum(_dot(tri, part) for part in _split_bf16(da, 3))
    e = e_ref[...]
    csx[...] = sum(_dot(part, e) for part in _split_bf16(cs, 2))
    dtx[...] = _dot(dt.astype(BF16), e)
    cst[...] = cs.T
    dtt[...] = dt.T

    for c0 in range(0, conv_dim, SSD_CONV_BLOCK):
        proj = _dot(un, win_ref[:, inner + c0:inner + c0 + SSD_CONV_BLOCK])
        for c in range(SSD_CONV_BLOCK // LANES):
            xbuf[c0 // LANES + c, SUBLANES:SUBLANES + tm, :] = proj[:, c * LANES:(c + 1) * LANES]
        for c in range(SSD_CONV_BLOCK // LANES):
            t = c0 // LANES + c
            cl = slice(c0 + c * LANES, c0 + (c + 1) * LANES)
            conv = cb_ref[:, cl] + cw_ref[CONV_W - 1:CONV_W, cl] * xbuf[t, SUBLANES:SUBLANES + tm, :]
            for k in range(CONV_W - 1):
                d = CONV_W - 1 - k
                conv = conv + cw_ref[k:k + 1, cl] * xbuf[t, pl.ds(SUBLANES - d, tm, stride=1), :]
            xbuf[t, 0:SUBLANES, :] = xbuf[t, tm:tm + SUBLANES, :]
            cbuf[:, cl] = _silu(conv)
    zbuf[...] = _silu(_dot(un, win_ref[:, 0:inner]))

    ii = lax.broadcasted_iota(jnp.int32, (chunk, LANES), 0)
    ll = lax.broadcasted_iota(jnp.int32, (chunk, LANES), 1)
    causal2 = ll % half <= ii
    low = ll < half
    low_row = lax.broadcasted_iota(jnp.int32, (1, LANES), 1) < half
    dexp = dexp_ref[...]

    def lane_pair(ref, h0, ci):
        t0 = (ci * chunk // LANES) * LANES
        a = ref[h0:h0 + 1, t0:t0 + LANES]
        b = ref[h0 + 1:h0 + 2, t0:t0 + LANES]
        if (ci * chunk) % LANES == 0:
            return jnp.where(low_row, a, pltpu.roll(b, half, 1))
        return jnp.where(low_row, pltpu.roll(a, half, 1), b)

    for ci in range(tm // chunk):
        rs = slice(ci * chunk, (ci + 1) * chunk)
        cs_last = csx[(ci + 1) * chunk - 1:(ci + 1) * chunk, :]
        bgs, cgs, cb2s, y_offs, s_prevs = [], [], [], [], []
        for g in range(groups):
            bg = cbuf[rs, inner + g * ns:inner + (g + 1) * ns].astype(BF16)
            cg = cbuf[rs, inner + (groups + g) * ns:inner + (groups + g + 1) * ns].astype(BF16)
            s_prev = sst[g]
            bgs.append(bg)
            cgs.append(cg)
            s_prevs.append(s_prev)
            cb2s.append(_dot_nt(cg, jnp.concatenate([bg, bg], axis=0)))
            y_offs.append(_dot(cg, s_prev.astype(BF16)))
        for g in range(groups):
            gl = slice(g * gw, (g + 1) * gw)
            xg = cbuf[rs, gl]
            csg = csx[rs, gl]
            yd = []
            for pp in range(gw // LANES):
                ps = slice(pp * LANES, (pp + 1) * LANES)
                h0 = (g * gw + pp * LANES) // SSD_HEADDIM
                row = lane_pair(cst, h0, ci)
                dtrow = lane_pair(dtt, h0, ci)
                lm = jnp.where(causal2, jnp.exp(csg[:, ps] - row), 0.0)
                w = (cb2s[g] * lm * dtrow).astype(BF16)
                xp = xg[:, ps]
                bd = jnp.concatenate([jnp.where(low, xp, 0.0), jnp.where(low, 0.0, xp)],
                                     axis=0).astype(BF16)
                yd.append(_dot(w, bd))
            ybuf[rs, gl] = jnp.concatenate(yd, axis=1) + y_offs[g] * jnp.exp(csg) + dexp[:, gl] * xg
        for g in range(groups):
            gl = slice(g * gw, (g + 1) * gw)
            cl = cs_last[:, gl]
            w1 = jnp.exp(cl - csx[rs, gl]) * dtx[rs, gl]
            sst[g] = s_prevs[g] * jnp.exp(cl) + _dot_tn(bgs[g], (cbuf[rs, gl] * w1).astype(BF16))

    acc = x
    for g in range(groups):
        gl = slice(g * gw, (g + 1) * gw)
        yg = ybuf[:, gl] * zbuf[:, gl]
        yn = yg * lax.rsqrt(jnp.mean(yg * yg, axis=-1, keepdims=True) + NORM_EPS) * gn_ref[:, gl]
        acc = acc + _dot(yn.astype(BF16), wout_ref[gl, :])
    if final:
        acc = _rms(acc, fg_ref[...])
    o_ref[...] = acc


def _ssd_layer(h, ng, fg, w_in, conv_w, conv_b, dt_bias, a_log, d_skip, g_norm, w_out, *, final, tm=256):
    B, S, D = h.shape
    inner = w_out.shape[0]
    heads = inner // SSD_HEADDIM
    conv_dim = inner + 2 * SSD_GROUPS * SSD_STATE
    pad = LANES - heads
    win = jnp.pad(w_in.astype(BF16), ((0, 0), (0, pad)))
    row = lambda v: v.reshape(1, -1).astype(F32)
    padrow = lambda v: jnp.pad(v.astype(F32), (0, pad)).reshape(1, LANES)
    expand = (lax.broadcasted_iota(jnp.int32, (LANES, inner), 1) // SSD_HEADDIM
              == lax.broadcasted_iota(jnp.int32, (LANES, inner), 0)).astype(BF16)
    kern = functools.partial(_ssd_kernel, tm=tm, inner=inner, final=final)
    gw = inner // SSD_GROUPS
    return pl.pallas_call(
        kern,
        grid=(B, S // tm),
        in_specs=[_seq_spec(tm, D), _const_spec((1, D)), _const_spec(win.shape),
                  _const_spec((CONV_W, conv_dim)), _const_spec((1, conv_dim)), _const_spec((1, LANES)),
                  _const_spec((1, LANES)), _const_spec((1, inner)), _const_spec((1, inner)),
                  _const_spec((LANES, inner)), _const_spec((inner, D)), _const_spec((1, D))],
        out_specs=_seq_spec(tm, D),
        out_shape=jax.ShapeDtypeStruct((B, S, D), F32),
        scratch_shapes=[pltpu.VMEM((conv_dim // LANES, tm + SUBLANES, LANES), F32), pltpu.VMEM((tm, conv_dim), F32),
                        pltpu.VMEM((tm, inner), F32), pltpu.VMEM((tm, inner), F32),
                        pltpu.VMEM((LANES, tm), F32), pltpu.VMEM((LANES, tm), F32),
                        pltpu.VMEM((tm, inner), F32), pltpu.VMEM((tm, inner), F32),
                        pltpu.VMEM((SSD_GROUPS, SSD_STATE, gw), F32)],
        compiler_params=pltpu.CompilerParams(dimension_semantics=("arbitrary", "arbitrary"),
                                             vmem_limit_bytes=VMEM_LIMIT),
        name="ssd_layer",
    )(h, row(ng), win, conv_w.astype(F32), row(conv_b), padrow(dt_bias), padrow(a_log),
      row(jnp.repeat(d_skip, SSD_HEADDIM)), row(g_norm), expand, w_out.astype(BF16), row(fg))


def _mla_proj_kernel(h_ref, pos_ref, ng_ref, win_ref, gq_ref, wuq_ref, gkv_ref, wukv_ref, invf_ref,
                     sgn_ref, q_ref, k_ref, v_ref, g_ref, *, tm):
    heads = MLA_HEADS
    kw = heads * LANES
    x = h_ref[...]
    un = _rms(x, ng_ref[...]).astype(BF16)
    t = _dot(un, win_ref[...])
    c_q = t[:, :MLA_Q_RANK]
    c_kv = t[:, MLA_Q_RANK:MLA_Q_RANK + MLA_KV_RANK]
    g0 = MLA_Q_RANK + MLA_KV_RANK
    gate = t[:, g0:g0 + heads * MLA_V]
    k_r = t[:, g0 + heads * MLA_V:]
    g_ref[...] = _silu(gate).astype(BF16)

    ang = pos_ref[...].astype(F32) * invf_ref[...]
    cos = jnp.cos(ang)
    sin = jnp.sin(ang) * sgn_ref[...]
    lane = lax.broadcasted_iota(jnp.int32, (tm, LANES), 1)
    first_half = lane < MLA_NOPE + MLA_ROPE // 2

    def rope(a):
        swapped = jnp.where(first_half, pltpu.roll(a, LANES - MLA_ROPE // 2, 1),
                            pltpu.roll(a, MLA_ROPE // 2, 1))
        return a * cos + swapped * sin

    k_rope = rope(k_r)
    qn = _dot(_rms(c_q, gq_ref[...]).astype(BF16), wuq_ref[...])
    kvn = _dot(_rms(c_kv, gkv_ref[...]).astype(BF16), wukv_ref[...])
    scale = (MLA_NOPE + MLA_ROPE) ** -0.5 * LOG2_E
    for hh in range(heads):
        hs = slice(hh * LANES, (hh + 1) * LANES)
        q_ref[0, hh] = (rope(qn[:, hs]) * scale).astype(BF16)
        k_ref[0, hh] = (kvn[:, hs] + k_rope).astype(BF16)
    ones_row = jnp.where(lax.broadcasted_iota(jnp.int32, (MLA_VROWS - MLA_V, tm), 0) == 0, 1.0, 0.0)
    for hp in range(heads // 2):
        vt = kvn[:, kw + hp * LANES:kw + (hp + 1) * LANES].T
        for e in range(2):
            v_ref[0, 2 * hp + e, 0] = jnp.concatenate(
                [vt[e * MLA_V:(e + 1) * MLA_V], ones_row], axis=0).astype(BF16)


def _mla_attn_kernel(q_ref, k_ref, vt_ref, g_ref, h_ref, wout_ref, fg_ref, o_ref, m_sc, acc_sc, obuf,
                     *, tq, final):
    heads = MLA_HEADS
    j = pl.program_id(1)
    key_idx = lax.broadcasted_iota(jnp.int32, (tq, tq), 0)
    qry_idx = lax.broadcasted_iota(jnp.int32, (tq, tq), 1)
    diag = key_idx <= qry_idx
    m_sc[...] = jnp.full(m_sc.shape, -jnp.inf, F32)
    acc_sc[...] = jnp.zeros(acc_sc.shape, F32)

    def kv_tiles(tiles):
        units = [(kt, masked, hh) for kt, masked in tiles for hh in range(heads)]

        def scores(unit):
            kt, _, hh = unit
            k0 = pl.multiple_of(kt * tq, tq)
            return _dot_nt(k_ref[0, hh, pl.ds(k0, tq), :], q_ref[0, hh])

        ready = [scores(u) for u in units[:MLA_LOOKAHEAD]]
        for n, (kt, masked, hh) in enumerate(units):
            if n % MLA_LOOKAHEAD == 0:
                ready.extend(scores(u) for u in units[n + MLA_LOOKAHEAD:n + 2 * MLA_LOOKAHEAD])
            st = ready.pop(0)
            if masked:
                st = jnp.where(diag, st, -jnp.inf)
            m_old = m_sc[hh]
            m_new = jnp.maximum(m_old, jnp.max(st, axis=0, keepdims=True))
            pt = jnp.exp2(st - m_new).astype(BF16)
            acc_sc[hh] = jnp.exp2(m_old - m_new) * acc_sc[hh] + _dot(vt_ref[0, hh, kt], pt)
            m_sc[hh] = m_new

    def full_pair(p, carry):
        kv_tiles([(2 * p, False), (2 * p + 1, False)])
        return carry

    lax.fori_loop(0, j // 2, full_pair, 0)

    @pl.when(j % 2 == 1)
    def _():
        kv_tiles([(j - 1, False), (j, True)])

    @pl.when(j % 2 == 0)
    def _():
        kv_tiles([(j, True)])

    for hp in range(heads // 2):
        halves = []
        for e in range(2):
            acc = acc_sc[2 * hp + e]
            halves.append(acc[:MLA_V] * (1.0 / acc[MLA_V:MLA_V + 1]))
        obuf[:, hp * LANES:(hp + 1) * LANES] = jnp.concatenate(halves, axis=0).T
    y = (obuf[...] * g_ref[...].astype(F32)).astype(BF16)
    o_ref[...] = _residual(h_ref[...], _dot(y, wout_ref[...]), fg_ref, final)


def _mla_layer(h, positions, ng, fg, w_in, g_q, w_uq, g_kv, w_ukv, w_out, *, final, tq=256):
    B, S, D = h.shape
    heads, half = MLA_HEADS, MLA_ROPE // 2
    qk = MLA_NOPE + MLA_ROPE
    width = heads * MLA_V
    c0 = MLA_Q_RANK + MLA_KV_RANK
    w_in = w_in.astype(BF16)
    w_kr = jnp.pad(w_in[:, c0:c0 + MLA_ROPE], ((0, 0), (MLA_NOPE, LANES - qk)))
    win = jnp.concatenate([w_in[:, :c0], w_in[:, c0 + MLA_ROPE:], w_kr], axis=1)
    wuq = jnp.pad(w_uq.reshape(MLA_Q_RANK, heads, qk), ((0, 0), (0, 0), (0, LANES - qk)))
    wuq = wuq.reshape(MLA_Q_RANK, heads * LANES).astype(BF16)
    wkv = w_ukv.reshape(MLA_KV_RANK, heads, MLA_NOPE + MLA_V)
    wk = jnp.pad(wkv[:, :, :MLA_NOPE], ((0, 0), (0, 0), (0, LANES - MLA_NOPE)))
    wukv = jnp.concatenate([wk.reshape(MLA_KV_RANK, heads * LANES),
                            wkv[:, :, MLA_NOPE:].reshape(MLA_KV_RANK, width)], axis=1).astype(BF16)
    inv_freq = ROPE_THETA ** (-jnp.arange(0, MLA_ROPE, 2, dtype=F32) / MLA_ROPE)
    zeros = lambda n: jnp.zeros((n,), F32)
    invf = jnp.concatenate([zeros(MLA_NOPE), inv_freq, inv_freq, zeros(LANES - qk)]).reshape(1, LANES)
    sgn = jnp.concatenate([jnp.ones((MLA_NOPE,), F32), -jnp.ones((half,), F32),
                           jnp.ones((LANES - MLA_NOPE - half,), F32)]).reshape(1, LANES)
    row = lambda v: v.reshape(1, -1).astype(F32)

    tm = tq
    head_spec = pl.BlockSpec((1, heads, tm, LANES), lambda b, j: (b, 0, j, 0))
    vt_spec = pl.BlockSpec((1, heads, 1, MLA_VROWS, tm), lambda b, j: (b, 0, j, 0, 0))
    q, k, vt, g = pl.pallas_call(
        functools.partial(_mla_proj_kernel, tm=tm),
        grid=(B, S // tm),
        in_specs=[_seq_spec(tm, D), _seq_spec(tm, 1), _const_spec((1, D)), _const_spec(win.shape),
                  _const_spec((1, MLA_Q_RANK)), _const_spec(wuq.shape), _const_spec((1, MLA_KV_RANK)),
                  _const_spec(wukv.shape), _const_spec((1, LANES)), _const_spec((1, LANES))],
        out_specs=[head_spec, head_spec, vt_spec, _seq_spec(tm, width)],
        out_shape=[jax.ShapeDtypeStruct((B, heads, S, LANES), BF16),
                   jax.ShapeDtypeStruct((B, heads, S, LANES), BF16),
                   jax.ShapeDtypeStruct((B, heads, S // tm, MLA_VROWS, tm), BF16),
                   jax.ShapeDtypeStruct((B, S, width), BF16)],
        compiler_params=pltpu.CompilerParams(dimension_semantics=("arbitrary", "arbitrary"),
                                             vmem_limit_bytes=VMEM_LIMIT),
        name="mla_proj",
    )(h, positions.reshape(B, S, 1), row(ng), win, row(g_q), wuq, row(g_kv), wukv, invf, sgn)

    return pl.pallas_call(
        functools.partial(_mla_attn_kernel, tq=tq, final=final),
        grid=(B, S // tq),
        in_specs=[head_spec, pl.BlockSpec((1, heads, S, LANES), lambda b, j: (b, 0, 0, 0)),
                  pl.BlockSpec((1, heads, S // tm, MLA_VROWS, tm), lambda b, j: (b, 0, 0, 0, 0)),
                  _seq_spec(tq, width), _seq_spec(tq, D), _const_spec((width, D)), _const_spec((1, D))],
        out_specs=_seq_spec(tq, D),
        out_shape=jax.ShapeDtypeStruct((B, S, D), F32),
        scratch_shapes=[pltpu.VMEM((heads, 1, tq), F32), pltpu.VMEM((heads, MLA_VROWS, tq), F32),
                        pltpu.VMEM((tq, width), F32)],
        compiler_params=pltpu.CompilerParams(dimension_semantics=("arbitrary", "arbitrary"),
                                             vmem_limit_bytes=VMEM_LIMIT),
        name="mla_attn",
    )(q, k, vt, g, h, w_out.astype(BF16), row(fg))


def kernel(x, positions, norm_g, final_g, mla_w_in, mla_g_q, mla_w_uq, mla_g_kv, mla_w_ukv, mla_w_out, gla_w_in, gla_w_gk2, gla_b_gk, gla_g_o, gla_w_out, lru_w_in, lru_conv_w, lru_conv_b, lru_w_a, lru_b_a, lru_w_x, lru_b_x, lru_lam, lru_w_out, ssd_w_in, ssd_conv_w, ssd_conv_b, ssd_dt_bias, ssd_a_log, ssd_d, ssd_g_norm, ssd_w_out):
    depth = norm_g.shape[0]
    h = x
    for i in range(depth):
        m, j = i % N_MIXERS, i // N_MIXERS
        common = dict(final=(i == depth - 1))
        if m == 0:
            h = _mla_layer(h, positions, norm_g[i], final_g, mla_w_in[j], mla_g_q[j], mla_w_uq[j],
                           mla_g_kv[j], mla_w_ukv[j], mla_w_out[j], **common)
        elif m == 1:
            h = _gla_layer(h, norm_g[i], final_g, gla_w_in[j], gla_w_gk2[j], gla_b_gk[j], gla_g_o[j],
                           gla_w_out[j], **common)
        elif m == 2:
            h = _lru_layer(h, norm_g[i], final_g, lru_w_in[j], lru_conv_w[j], lru_conv_b[j], lru_w_a[j],
                           lru_b_a[j], lru_w_x[j], lru_b_x[j], lru_lam[j], lru_w_out[j], **common)
        else:
            h = _ssd_layer(h, norm_g[i], final_g, ssd_w_in[j], ssd_conv_w[j], ssd_conv_b[j],
                           ssd_dt_bias[j], ssd_a_log[j], ssd_d[j], ssd_g_norm[j], ssd_w_out[j], **common)
    return h
```

```python
import functools

import jax
import jax.numpy as jnp
from jax import lax
from jax.experimental import pallas as pl
from jax.experimental.pallas import tpu as pltpu

F32 = jnp.float32
BF16 = jnp.bfloat16

NORM_EPS = 1e-6
N_MIXERS = 4
MLA_HEADS = 16
MLA_Q_RANK = 384
MLA_KV_RANK = 256
MLA_NOPE = 64
MLA_ROPE = 32
MLA_V = 64
MLA_VROWS = 80
MLA_PROJ_TILES = 2
MLA_LOOKAHEAD = 4
ROPE_THETA = 10000.0
GLA_HEADS = 4
GLA_GATE_RANK = 16
GLA_TAU = 16.0
GLA_CHUNK = 64
LRU_BLOCKS = 10
LRU_C = 8.0
CONV_W = 4
SSD_HEADDIM = 64
SSD_GROUPS = 8
SSD_STATE = 128
SSD_CHUNK = 64
SSD_CONV_BLOCK = 1024

LOG2_E = 1.4426950408889634
LANES = 128
SUBLANES = 8
VMEM_LIMIT = 56 * 1024 * 1024


def _dot(a, b):
    return jnp.dot(a, b, preferred_element_type=F32)


def _dot_nt(a, b):
    return lax.dot_general(a, b, (((1,), (1,)), ((), ())), preferred_element_type=F32)


def _dot_tn(a, b):
    return lax.dot_general(a, b, (((0,), (0,)), ((), ())), preferred_element_type=F32)


def _rms(x, g):
    ms = jnp.mean(x * x, axis=-1, keepdims=True)
    return x * lax.rsqrt(ms + NORM_EPS) * g


def _softplus(x):
    return jnp.maximum(x, 0.0) + jnp.log1p(jnp.exp(-jnp.abs(x)))


def _sigmoid(x):
    return 0.5 * jnp.tanh(0.5 * x) + 0.5


def _silu(x):
    hx = 0.5 * x
    return hx * jnp.tanh(hx) + hx


def _residual(x, y, fg_ref, final):
    r = x + y
    if final:
        r = _rms(r, fg_ref[...])
    return r


def _chunk_tri(n, chunk):
    r = lax.broadcasted_iota(jnp.int32, (n, n), 0)
    c = lax.broadcasted_iota(jnp.int32, (n, n), 1)
    keep = jnp.logical_and(r // chunk == c // chunk, c <= r)
    return jnp.where(keep, 1.0, 0.0).astype(BF16)


def _split_bf16(x, parts):
    out = []
    for _ in range(parts - 1):
        hi = x.astype(BF16)
        out.append(hi)
        x = x - hi.astype(F32)
    out.append(x.astype(BF16))
    return out


def _const_spec(shape):
    n = len(shape)
    return pl.BlockSpec(shape, lambda *_: (0,) * n, pipeline_mode=pl.Buffered(1))


def _seq_spec(tm, d):
    return pl.BlockSpec((None, tm, d), lambda b, j: (b, j, 0))


def _lru_kernel(h_ref, ng_ref, win_ref, cw_ref, cb_ref, wax_ref, ba_ref, bx_ref, lam_ref, wout_ref,
                fg_ref, o_ref, ubuf, abuf, bbuf, gbuf, hst, *, tm, nb, width, final):
    hist = (CONV_W - 1) * nb
    tt = tm // nb

    @pl.when(pl.program_id(0) == 0)
    def _():
        ubuf[0:hist, :] = jnp.zeros((hist, width), F32)
        hst[...] = jnp.zeros_like(hst)

    x = pltpu.einshape("btd->tbd", h_ref[...]).reshape(tm, h_ref.shape[-1])
    un = _rms(x, ng_ref[...]).astype(BF16)
    ubuf[hist:hist + tm, :] = _dot(un, win_ref[:, width:])
    gbuf[...] = _silu(_dot(un, win_ref[:, :width]))
    cw = cw_ref[...]
    conv = cb_ref[...] + cw[CONV_W - 1:CONV_W] * ubuf[hist:hist + tm, :]
    for k in range(CONV_W - 1):
        d = (CONV_W - 1 - k) * nb
        conv = conv + cw[k:k + 1] * ubuf[hist - d:hist - d + tm, :]
    ubuf[0:hist, :] = ubuf[tm:tm + hist, :]

    sp = _softplus(-lam_ref[...])
    for n in range(width // LANES):
        sl = slice(n * LANES, (n + 1) * LANES)
        cn = conv[:, sl]
        ra = _dot(cn.astype(BF16), wax_ref[n])
        r = _sigmoid(ra[:, :LANES] + ba_ref[:, sl])
        i = _sigmoid(ra[:, LANES:] + bx_ref[:, sl])
        log_a = -LRU_C * r * sp[:, sl]
        th = jnp.tanh(log_a)
        u = -2.0 * th
        mult = jnp.where(u > 0.0, u * lax.rsqrt(u * (1.0 - th)), 0.0)
        abuf[:, sl] = jnp.exp(log_a)
        bbuf[:, sl] = mult * (i * cn)

    hs = hst[...]
    for t in range(tm // nb):
        rs = slice(t * nb, (t + 1) * nb)
        hs = abuf[rs, :] * hs + bbuf[rs, :]
        bbuf[rs, :] = hs
    hst[...] = hs
    y = (bbuf[...] * gbuf[...]).astype(BF16)
    out = _residual(x, _dot(y, wout_ref[...]), fg_ref, final)
    o_ref[...] = pltpu.einshape("tbd->btd", out.reshape(tt, nb, out.shape[-1]))


def _lru_layer(h, ng, fg, w_in, conv_w, conv_b, w_a, b_a, w_x, b_x, lam, w_out, *, final, tm=512):
    B, S, D = h.shape
    assert B == SUBLANES, "time-major RG-LRU tiling needs one sublane tile per time step"
    width = w_out.shape[0]
    wax = jnp.concatenate([w_a, w_x], axis=-1).astype(BF16)
    row = lambda v: v.reshape(1, -1).astype(F32)
    kern = functools.partial(_lru_kernel, tm=tm, nb=B, width=width, final=final)
    hist = (CONV_W - 1) * B
    tile = pl.BlockSpec((B, tm // B, D), lambda j: (0, j, 0))
    return pl.pallas_call(
        kern,
        grid=(S * B // tm,),
        in_specs=[tile, _const_spec((1, D)), _const_spec((D, 2 * width)),
                  _const_spec((CONV_W, width)), _const_spec((1, width)), _const_spec(wax.shape),
                  _const_spec((1, width)), _const_spec((1, width)), _const_spec((1, width)),
                  _const_spec((width, D)), _const_spec((1, D))],
        out_specs=tile,
        out_shape=jax.ShapeDtypeStruct((B, S, D), F32),
        scratch_shapes=[pltpu.VMEM((tm + hist, width), F32), pltpu.VMEM((tm, width), F32),
                        pltpu.VMEM((tm, width), F32), pltpu.VMEM((tm, width), F32),
                        pltpu.VMEM((B, width), F32)],
        compiler_params=pltpu.CompilerParams(dimension_semantics=("arbitrary",),
                                             vmem_limit_bytes=VMEM_LIMIT),
        name="lru_layer",
    )(h, row(ng), w_in.astype(BF16), conv_w.astype(F32), row(conv_b), wax, row(b_a), row(b_x),
      row(lam), w_out.astype(BF16), row(fg))


def _gla_kernel(h_ref, ng_ref, wq_ref, wgk_ref, wgk2_ref, bgk_ref, go_ref, wout_ref, fg_ref, o_ref,
                sst, obuf, gbuf, *, tm, key, val, final):
    heads, chunk = GLA_HEADS, GLA_CHUNK
    dk, dv = key // heads, val // heads

    @pl.when(pl.program_id(1) == 0)
    def _():
        sst[...] = jnp.zeros_like(sst)

    x = h_ref[...]
    un = _rms(x, ng_ref[...]).astype(BF16)
    q = _dot(un, wq_ref[:, :key]) * dk ** -0.5
    k = _dot(un, wq_ref[:, key:2 * key])
    gk = _dot(un, wgk_ref[...]).astype(BF16)
    log_a = -_softplus(-(_dot(gk, wgk2_ref[...]) + bgk_ref[...])) * (1.0 / GLA_TAU)
    v = _dot(un, wq_ref[:, 2 * key:2 * key + val]).astype(BF16)
    gbuf[...] = _silu(_dot(un, wq_ref[:, 2 * key + val:]))
    tri = _chunk_tri(tm, chunk)
    b = sum(_dot(tri, part) for part in _split_bf16(log_a, 2))

    rr = lax.broadcasted_iota(jnp.int32, (chunk, chunk), 0)
    cc = lax.broadcasted_iota(jnp.int32, (chunk, chunk), 1)
    causal = cc <= rr
    nch = tm // chunk
    ksl = [slice(hh * dk, (hh + 1) * dk) for hh in range(heads)]
    vsl = [slice(hh * dv, (hh + 1) * dv) for hh in range(heads)]
    rows = [slice(ci * chunk, (ci + 1) * chunk) for ci in range(nch)]
    qt, dec, att, ds = [], [], [], []
    for rs in rows:
        bc = b[rs]
        bl = bc[chunk - 1:chunk, :]
        qt.append((q[rs] * jnp.exp(bc)).astype(BF16))
        kt = (k[rs] * jnp.exp(-bc)).astype(BF16)
        ke = (k[rs] * jnp.exp(bl - bc)).astype(BF16)
        dec.append(jnp.exp(bl))
        att.append([_dot_nt(qt[-1][:, ks], kt[:, ks]) for ks in ksl])
        ds.append([_dot_tn(v[rs, vsl[hh]], ke[:, ksl[hh]]) for hh in range(heads)])
    for hh in range(heads):
        s_in = sst[hh]
        for ci, rs in enumerate(rows):
            p = jnp.where(causal, att[ci][hh], 0.0).astype(BF16)
            obuf[rs, vsl[hh]] = _dot(p, v[rs, vsl[hh]]) + _dot_nt(qt[ci][:, ksl[hh]], s_in.astype(BF16))
            s_in = dec[ci][:, ksl[hh]] * s_in + ds[ci][hh]
        sst[hh] = s_in

    parts = []
    for hh in range(heads):
        oh = obuf[:, hh * dv:(hh + 1) * dv]
        parts.append(oh * lax.rsqrt(jnp.mean(oh * oh, axis=-1, keepdims=True) + NORM_EPS))
    on = jnp.concatenate(parts, axis=1) * go_ref[...]
    y = (on * gbuf[...]).astype(BF16)
    o_ref[...] = _residual(x, _dot(y, wout_ref[...]), fg_ref, final)


def _gla_layer(h, ng, fg, w_in, w_gk2, b_gk, g_o, w_out, *, final, tm=512):
    B, S, D = h.shape
    val = w_out.shape[0]
    key = (w_in.shape[1] - 2 * val - GLA_GATE_RANK) // 2
    main = 2 * key + 2 * val
    pad = LANES - GLA_GATE_RANK
    w_in = w_in.astype(BF16)
    wq = w_in[:, :main]
    wgk = jnp.pad(w_in[:, main:], ((0, 0), (0, pad)))
    wgk2 = jnp.pad(w_gk2.astype(BF16), ((0, pad), (0, 0)))
    row = lambda v: v.reshape(1, -1).astype(F32)
    kern = functools.partial(_gla_kernel, tm=tm, key=key, val=val, final=final)
    return pl.pallas_call(
        kern,
        grid=(B, S // tm),
        in_specs=[_seq_spec(tm, D), _const_spec((1, D)), _const_spec((D, main)), _const_spec((D, LANES)),
                  _const_spec((LANES, key)), _const_spec((1, key)), _const_spec((1, val)),
                  _const_spec((val, D)), _const_spec((1, D))],
        out_specs=_seq_spec(tm, D),
        out_shape=jax.ShapeDtypeStruct((B, S, D), F32),
        scratch_shapes=[pltpu.VMEM((GLA_HEADS, val // GLA_HEADS, key // GLA_HEADS), F32),
                        pltpu.VMEM((tm, val), F32), pltpu.VMEM((tm, val), F32)],
        compiler_params=pltpu.CompilerParams(dimension_semantics=("arbitrary", "arbitrary"),
                                             vmem_limit_bytes=VMEM_LIMIT),
        name="gla_layer",
    )(h, row(ng), wq, wgk, wgk2, row(b_gk), row(jnp.tile(g_o, GLA_HEADS)), w_out.astype(BF16), row(fg))


def _ssd_kernel(h_ref, ng_ref, win_ref, cw_ref, cb_ref, dtb_ref, alog_ref, dexp_ref, gn_ref, e_ref,
                wout_ref, fg_ref, o_ref, xbuf, cbuf, csx, dtx, cst, dtt, zbuf, ybuf, sst, *, tm, inner, final):
    groups, ns, chunk = SSD_GROUPS, SSD_STATE, SSD_CHUNK
    gw = inner // groups
    conv_dim = inner + 2 * groups * ns
    half = SSD_HEADDIM

    @pl.when(pl.program_id(1) == 0)
    def _():
        xbuf[:, 0:SUBLANES, :] = jnp.zeros((conv_dim // LANES, SUBLANES, LANES), F32)
        sst[...] = jnp.zeros_like(sst)

    x = h_ref[...]
    un = _rms(x, ng_ref[...]).astype(BF16)

    dt_raw = _dot(un, win_ref[:, inner + conv_dim:])
    dt = _softplus(dt_raw + dtb_ref[...])
    da = dt * (-jnp.exp(alog_ref[...]))
    tri = _chunk_tri(tm, chunk)
    cs = sum(_dot(tri, part) for part in _split_bf16(da, 3))
    e = e_ref[...]
    csx[...] = sum(_dot(part, e) for part in _split_bf16(cs, 2))
    dtx[...] = _dot(dt.astype(BF16), e)
    cst[...] = cs.T
    dtt[...] = dt.T

    for c0 in range(0, conv_dim, SSD_CONV_BLOCK):
        proj = _dot(un, win_ref[:, inner + c0:inner + c0 + SSD_CONV_BLOCK])
        for c in range(SSD_CONV_BLOCK // LANES):
            xbuf[c0 // LANES + c, SUBLANES:SUBLANES + tm, :] = proj[:, c * LANES:(c + 1) * LANES]
        for c in range(SSD_CONV_BLOCK // LANES):
            t = c0 // LANES + c
            cl = slice(c0 + c * LANES, c0 + (c + 1) * LANES)
            conv = cb_ref[:, cl] + cw_ref[CONV_W - 1:CONV_W, cl] * xbuf[t, SUBLANES:SUBLANES + tm, :]
            for k in range(CONV_W - 1):
                d = CONV_W - 1 - k
                conv = conv + cw_ref[k:k + 1, cl] * xbuf[t, pl.ds(SUBLANES - d, tm, stride=1), :]
            xbuf[t, 0:SUBLANES, :] = xbuf[t, tm:tm + SUBLANES, :]
            cbuf[:, cl] = _silu(conv)
    zbuf[...] = _silu(_dot(un, win_ref[:, 0:inner]))

    ii = lax.broadcasted_iota(jnp.int32, (chunk, LANES), 0)
    ll = lax.broadcasted_iota(jnp.int32, (chunk, LANES), 1)
    causal2 = ll % half <= ii
    low = ll < half
    low_row = lax.broadcasted_iota(jnp.int32, (1, LANES), 1) < half
    dexp = dexp_ref[...]

    def lane_pair(ref, h0, ci):
        t0 = (ci * chunk // LANES) * LANES
        a = ref[h0:h0 + 1, t0:t0 + LANES]
        b = ref[h0 + 1:h0 + 2, t0:t0 + LANES]
        if (ci * chunk) % LANES == 0:
            return jnp.where(low_row, a, pltpu.roll(b, half, 1))
        return jnp.where(low_row, pltpu.roll(a, half, 1), b)

    for ci in range(tm // chunk):
        rs = slice(ci * chunk, (ci + 1) * chunk)
        cs_last = csx[(ci + 1) * chunk - 1:(ci + 1) * chunk, :]
        bgs, cgs, cb2s, y_offs, s_prevs = [], [], [], [], []
        for g in range(groups):
            bg = cbuf[rs, inner + g * ns:inner + (g + 1) * ns].astype(BF16)
            cg = cbuf[rs, inner + (groups + g) * ns:inner + (groups + g + 1) * ns].astype(BF16)
            s_prev = sst[g]
            bgs.append(bg)
            cgs.append(cg)
            s_prevs.append(s_prev)
            cb2s.append(_dot_nt(cg, jnp.concatenate([bg, bg], axis=0)))
            y_offs.append(_dot(cg, s_prev.astype(BF16)))
        for g in range(groups):
            gl = slice(g * gw, (g + 1) * gw)
            xg = cbuf[rs, gl]
            csg = csx[rs, gl]
            yd = []
            for pp in range(gw // LANES):
                ps = slice(pp * LANES, (pp + 1) * LANES)
                h0 = (g * gw + pp * LANES) // SSD_HEADDIM
                row = lane_pair(cst, h0, ci)
                dtrow = lane_pair(dtt, h0, ci)
                lm = jnp.where(causal2, jnp.exp(csg[:, ps] - row), 0.0)
                w = (cb2s[g] * lm * dtrow).astype(BF16)
                xp = xg[:, ps]
                bd = jnp.concatenate([jnp.where(low, xp, 0.0), jnp.where(low, 0.0, xp)],
                                     axis=0).astype(BF16)
                yd.append(_dot(w, bd))
            ybuf[rs, gl] = jnp.concatenate(yd, axis=1) + y_offs[g] * jnp.exp(csg) + dexp[:, gl] * xg
        for g in range(groups):
            gl = slice(g * gw, (g + 1) * gw)
            cl = cs_last[:, gl]
            w1 = jnp.exp(cl - csx[rs, gl]) * dtx[rs, gl]
            sst[g] = s_prevs[g] * jnp.exp(cl) + _dot_tn(bgs[g], (cbuf[rs, gl] * w1).astype(BF16))

    acc = x
    for g in range(groups):
        gl = slice(g * gw, (g + 1) * gw)
        yg = ybuf[:, gl] * zbuf[:, gl]
        yn = yg * lax.rsqrt(jnp.mean(yg * yg, axis=-1, keepdims=True) + NORM_EPS) * gn_ref[:, gl]
        acc = acc + _dot(yn.astype(BF16), wout_ref[gl, :])
    if final:
        acc = _rms(acc, fg_ref[...])
    o_ref[...] = acc


def _ssd_layer(h, ng, fg, w_in, conv_w, conv_b, dt_bias, a_log, d_skip, g_norm, w_out, *, final, tm=256):
    B, S, D = h.shape
    inner = w_out.shape[0]
    heads = inner // SSD_HEADDIM
    conv_dim = inner + 2 * SSD_GROUPS * SSD_STATE
    pad = LANES - heads
    win = jnp.pad(w_in.astype(BF16), ((0, 0), (0, pad)))
    row = lambda v: v.reshape(1, -1).astype(F32)
    padrow = lambda v: jnp.pad(v.astype(F32), (0, pad)).reshape(1, LANES)
    expand = (lax.broadcasted_iota(jnp.int32, (LANES, inner), 1) // SSD_HEADDIM
              == lax.broadcasted_iota(jnp.int32, (LANES, inner), 0)).astype(BF16)
    kern = functools.partial(_ssd_kernel, tm=tm, inner=inner, final=final)
    gw = inner // SSD_GROUPS
    return pl.pallas_call(
        kern,
        grid=(B, S // tm),
        in_specs=[_seq_spec(tm, D), _const_spec((1, D)), _const_spec(win.shape),
                  _const_spec((CONV_W, conv_dim)), _const_spec((1, conv_dim)), _const_spec((1, LANES)),
                  _const_spec((1, LANES)), _const_spec((1, inner)), _const_spec((1, inner)),
                  _const_spec((LANES, inner)), _const_spec((inner, D)), _const_spec((1, D))],
        out_specs=_seq_spec(tm, D),
        out_shape=jax.ShapeDtypeStruct((B, S, D), F32),
        scratch_shapes=[pltpu.VMEM((conv_dim // LANES, tm + SUBLANES, LANES), F32), pltpu.VMEM((tm, conv_dim), F32),
                        pltpu.VMEM((tm, inner), F32), pltpu.VMEM((tm, inner), F32),
                        pltpu.VMEM((LANES, tm), F32), pltpu.VMEM((LANES, tm), F32),
                        pltpu.VMEM((tm, inner), F32), pltpu.VMEM((tm, inner), F32),
                        pltpu.VMEM((SSD_GROUPS, SSD_STATE, gw), F32)],
        compiler_params=pltpu.CompilerParams(dimension_semantics=("arbitrary", "arbitrary"),
                                             vmem_limit_bytes=VMEM_LIMIT),
        name="ssd_layer",
    )(h, row(ng), win, conv_w.astype(F32), row(conv_b), padrow(dt_bias), padrow(a_log),
      row(jnp.repeat(d_skip, SSD_HEADDIM)), row(g_norm), expand, w_out.astype(BF16), row(fg))


def _mla_proj_kernel(h_ref, pos_ref, ng_ref, win_ref, gq_ref, wuq_ref, gkv_ref, wukv_ref, invf_ref,
                     sgn_ref, q_ref, k_ref, v_ref, g_ref, *, tm):
    heads = MLA_HEADS
    kw = heads * LANES
    x = h_ref[...]
    un = _rms(x, ng_ref[...]).astype(BF16)
    t = _dot(un, win_ref[...])
    c_q = t[:, :MLA_Q_RANK]
    c_kv = t[:, MLA_Q_RANK:MLA_Q_RANK + MLA_KV_RANK]
    g0 = MLA_Q_RANK + MLA_KV_RANK
    gate = t[:, g0:g0 + heads * MLA_V]
    k_r = t[:, g0 + heads * MLA_V:]
    g_ref[...] = _silu(gate).astype(BF16)

    ang = pos_ref[...].astype(F32) * invf_ref[...]
    cos = jnp.cos(ang)
    sin = jnp.sin(ang) * sgn_ref[...]
    lane = lax.broadcasted_iota(jnp.int32, (tm, LANES), 1)
    first_half = lane < MLA_NOPE + MLA_ROPE // 2

    def rope(a):
        swapped = jnp.where(first_half, pltpu.roll(a, LANES - MLA_ROPE // 2, 1),
                            pltpu.roll(a, MLA_ROPE // 2, 1))
        return a * cos + swapped * sin

    k_rope = rope(k_r)
    qn = _dot(_rms(c_q, gq_ref[...]).astype(BF16), wuq_ref[...])
    kvn = _dot(_rms(c_kv, gkv_ref[...]).astype(BF16), wukv_ref[...])
    scale = (MLA_NOPE + MLA_ROPE) ** -0.5 * LOG2_E
    for hh in range(heads):
        hs = slice(hh * LANES, (hh + 1) * LANES)
        q_ref[0, hh] = (rope(qn[:, hs]) * scale).astype(BF16)
        k_ref[0, hh] = (kvn[:, hs] + k_rope).astype(BF16)
    tk = v_ref.shape[-1]
    ones_row = jnp.where(lax.broadcasted_iota(jnp.int32, (MLA_VROWS - MLA_V, tk), 0) == 0, 1.0, 0.0)
    for hp in range(heads // 2):
        vt = kvn[:, kw + hp * LANES:kw + (hp + 1) * LANES].T
        for e in range(2):
            for t in range(tm // tk):
                v_ref[0, 2 * hp + e, t] = jnp.concatenate(
                    [vt[e * MLA_V:(e + 1) * MLA_V, t * tk:(t + 1) * tk], ones_row], axis=0).astype(BF16)


def _mla_attn_kernel(q_ref, k_ref, vt_ref, g_ref, h_ref, wout_ref, fg_ref, o_ref, m_sc, acc_sc, obuf,
                     *, tq, final):
    heads = MLA_HEADS
    j = pl.program_id(1)
    key_idx = lax.broadcasted_iota(jnp.int32, (tq, tq), 0)
    qry_idx = lax.broadcasted_iota(jnp.int32, (tq, tq), 1)
    diag = key_idx <= qry_idx
    m_sc[...] = jnp.full(m_sc.shape, -jnp.inf, F32)
    acc_sc[...] = jnp.zeros(acc_sc.shape, F32)

    def kv_tiles(tiles):
        units = [(kt, masked, hh) for kt, masked in tiles for hh in range(heads)]

        def scores(unit):
            kt, _, hh = unit
            k0 = pl.multiple_of(kt * tq, tq)
            return _dot_nt(k_ref[0, hh, pl.ds(k0, tq), :], q_ref[0, hh])

        ready = [scores(u) for u in units[:MLA_LOOKAHEAD]]
        for n, (kt, masked, hh) in enumerate(units):
            if n % MLA_LOOKAHEAD == 0:
                ready.extend(scores(u) for u in units[n + MLA_LOOKAHEAD:n + 2 * MLA_LOOKAHEAD])
            st = ready.pop(0)
            if masked:
                st = jnp.where(diag, st, -jnp.inf)
            m_old = m_sc[hh]
            m_new = jnp.maximum(m_old, jnp.max(st, axis=0, keepdims=True))
            pt = jnp.exp2(st - m_new).astype(BF16)
            acc_sc[hh] = jnp.exp2(m_old - m_new) * acc_sc[hh] + _dot(vt_ref[0, hh, kt], pt)
            m_sc[hh] = m_new

    def full_pair(p, carry):
        kv_tiles([(2 * p, False), (2 * p + 1, False)])
        return carry

    lax.fori_loop(0, j // 2, full_pair, 0)

    @pl.when(j % 2 == 1)
    def _():
        kv_tiles([(j - 1, False), (j, True)])

    @pl.when(j % 2 == 0)
    def _():
        kv_tiles([(j, True)])

    for hp in range(heads // 2):
        halves = []
        for e in range(2):
            acc = acc_sc[2 * hp + e]
            halves.append(acc[:MLA_V] * (1.0 / acc[MLA_V:MLA_V + 1]))
        obuf[:, hp * LANES:(hp + 1) * LANES] = jnp.concatenate(halves, axis=0).T
    y = (obuf[...] * g_ref[...].astype(F32)).astype(BF16)
    o_ref[...] = _residual(h_ref[...], _dot(y, wout_ref[...]), fg_ref, final)


def _mla_layer(h, positions, ng, fg, w_in, g_q, w_uq, g_kv, w_ukv, w_out, *, final, tq=256):
    B, S, D = h.shape
    heads, half = MLA_HEADS, MLA_ROPE // 2
    qk = MLA_NOPE + MLA_ROPE
    width = heads * MLA_V
    c0 = MLA_Q_RANK + MLA_KV_RANK
    w_in = w_in.astype(BF16)
    w_kr = jnp.pad(w_in[:, c0:c0 + MLA_ROPE], ((0, 0), (MLA_NOPE, LANES - qk)))
    win = jnp.concatenate([w_in[:, :c0], w_in[:, c0 + MLA_ROPE:], w_kr], axis=1)
    wuq = jnp.pad(w_uq.reshape(MLA_Q_RANK, heads, qk), ((0, 0), (0, 0), (0, LANES - qk)))
    wuq = wuq.reshape(MLA_Q_RANK, heads * LANES).astype(BF16)
    wkv = w_ukv.reshape(MLA_KV_RANK, heads, MLA_NOPE + MLA_V)
    wk = jnp.pad(wkv[:, :, :MLA_NOPE], ((0, 0), (0, 0), (0, LANES - MLA_NOPE)))
    wukv = jnp.concatenate([wk.reshape(MLA_KV_RANK, heads * LANES),
                            wkv[:, :, MLA_NOPE:].reshape(MLA_KV_RANK, width)], axis=1).astype(BF16)
    inv_freq = ROPE_THETA ** (-jnp.arange(0, MLA_ROPE, 2, dtype=F32) / MLA_ROPE)
    zeros = lambda n: jnp.zeros((n,), F32)
    invf = jnp.concatenate([zeros(MLA_NOPE), inv_freq, inv_freq, zeros(LANES - qk)]).reshape(1, LANES)
    sgn = jnp.concatenate([jnp.ones((MLA_NOPE,), F32), -jnp.ones((half,), F32),
                           jnp.ones((LANES - MLA_NOPE - half,), F32)]).reshape(1, LANES)
    row = lambda v: v.reshape(1, -1).astype(F32)

    tm = tq * MLA_PROJ_TILES
    proj_spec = pl.BlockSpec((1, heads, tm, LANES), lambda b, j: (b, 0, j, 0))
    head_spec = pl.BlockSpec((1, heads, tq, LANES), lambda b, j: (b, 0, j, 0))
    vt_spec = pl.BlockSpec((1, heads, MLA_PROJ_TILES, MLA_VROWS, tq), lambda b, j: (b, 0, j, 0, 0))
    q, k, vt, g = pl.pallas_call(
        functools.partial(_mla_proj_kernel, tm=tm),
        grid=(B, S // tm),
        in_specs=[_seq_spec(tm, D), _seq_spec(tm, 1), _const_spec((1, D)), _const_spec(win.shape),
                  _const_spec((1, MLA_Q_RANK)), _const_spec(wuq.shape), _const_spec((1, MLA_KV_RANK)),
                  _const_spec(wukv.shape), _const_spec((1, LANES)), _const_spec((1, LANES))],
        out_specs=[proj_spec, proj_spec, vt_spec, _seq_spec(tm, width)],
        out_shape=[jax.ShapeDtypeStruct((B, heads, S, LANES), BF16),
                   jax.ShapeDtypeStruct((B, heads, S, LANES), BF16),
                   jax.ShapeDtypeStruct((B, heads, S // tq, MLA_VROWS, tq), BF16),
                   jax.ShapeDtypeStruct((B, S, width), BF16)],
        compiler_params=pltpu.CompilerParams(dimension_semantics=("arbitrary", "arbitrary"),
                                             vmem_limit_bytes=VMEM_LIMIT),
        name="mla_proj",
    )(h, positions.reshape(B, S, 1), row(ng), win, row(g_q), wuq, row(g_kv), wukv, invf, sgn)

    return pl.pallas_call(
        functools.partial(_mla_attn_kernel, tq=tq, final=final),
        grid=(B, S // tq),
        in_specs=[head_spec, pl.BlockSpec((1, heads, S, LANES), lambda b, j: (b, 0, 0, 0)),
                  pl.BlockSpec((1, heads, S // tq, MLA_VROWS, tq), lambda b, j: (b, 0, 0, 0, 0)),
                  _seq_spec(tq, width), _seq_spec(tq, D), _const_spec((width, D)), _const_spec((1, D))],
        out_specs=_seq_spec(tq, D),
        out_shape=jax.ShapeDtypeStruct((B, S, D), F32),
        scratch_shapes=[pltpu.VMEM((heads, 1, tq), F32), pltpu.VMEM((heads, MLA_VROWS, tq), F32),
                        pltpu.VMEM((tq, width), F32)],
        compiler_params=pltpu.CompilerParams(dimension_semantics=("arbitrary", "arbitrary"),
                                             vmem_limit_bytes=VMEM_LIMIT),
        name="mla_attn",
    )(q, k, vt, g, h, w_out.astype(BF16), row(fg))


def kernel(x, positions, norm_g, final_g, mla_w_in, mla_g_q, mla_w_uq, mla_g_kv, mla_w_ukv, mla_w_out, gla_w_in, gla_w_gk2, gla_b_gk, gla_g_o, gla_w_out, lru_w_in, lru_conv_w, lru_conv_b, lru_w_a, lru_b_a, lru_w_x, lru_b_x, lru_lam, lru_w_out, ssd_w_in, ssd_conv_w, ssd_conv_b, ssd_dt_bias, ssd_a_log, ssd_d, ssd_g_norm, ssd_w_out):
    depth = norm_g.shape[0]
    h = x
    for i in range(depth):
        m, j = i % N_MIXERS, i // N_MIXERS
        common = dict(final=(i == depth - 1))
        if m == 0:
            h = _mla_layer(h, positions, norm_g[i], final_g, mla_w_in[j], mla_g_q[j], mla_w_uq[j],
                           mla_g_kv[j], mla_w_ukv[j], mla_w_out[j], **common)
        elif m == 1:
            h = _gla_layer(h, norm_g[i], final_g, gla_w_in[j], gla_w_gk2[j], gla_b_gk[j], gla_g_o[j],
                           gla_w_out[j], **common)
        elif m == 2:
            h = _lru_layer(h, norm_g[i], final_g, lru_w_in[j], lru_conv_w[j], lru_conv_b[j], lru_w_a[j],
                           lru_b_a[j], lru_w_x[j], lru_b_x[j], lru_lam[j], lru_w_out[j], **common)
        else:
            h = _ssd_layer(h, norm_g[i], final_g, ssd_w_in[j], ssd_conv_w[j], ssd_conv_b[j],
                           ssd_dt_bias[j], ssd_a_log[j], ssd_d[j], ssd_g_norm[j], ssd_w_out[j], **common)
    return h
```

```python
import functools

import jax
import jax.numpy as jnp
from jax import lax
from jax.experimental import pallas as pl
from jax.experimental.pallas import tpu as pltpu

F32 = jnp.float32
BF16 = jnp.bfloat16

NORM_EPS = 1e-6
N_MIXERS = 4
MLA_HEADS = 16
MLA_Q_RANK = 384
MLA_KV_RANK = 256
MLA_NOPE = 64
MLA_ROPE = 32
MLA_V = 64
MLA_VROWS = 80
MLA_PROJ_TILES = 4
MLA_LOOKAHEAD = 4
ROPE_THETA = 10000.0
GLA_HEADS = 4
GLA_GATE_RANK = 16
GLA_TAU = 16.0
GLA_CHUNK = 64
LRU_BLOCKS = 10
LRU_C = 8.0
CONV_W = 4
SSD_HEADDIM = 64
SSD_GROUPS = 8
SSD_STATE = 128
SSD_CHUNK = 64
SSD_SUBTILES = 2
SSD_CONV_BLOCK = 1024

LOG2_E = 1.4426950408889634
LANES = 128
SUBLANES = 8
VMEM_LIMIT = 56 * 1024 * 1024


def _dot(a, b):
    return jnp.dot(a, b, preferred_element_type=F32)


def _dot_nt(a, b):
    return lax.dot_general(a, b, (((1,), (1,)), ((), ())), preferred_element_type=F32)


def _dot_tn(a, b):
    return lax.dot_general(a, b, (((0,), (0,)), ((), ())), preferred_element_type=F32)


def _rms(x, g):
    ms = jnp.mean(x * x, axis=-1, keepdims=True)
    return x * lax.rsqrt(ms + NORM_EPS) * g


def _softplus(x):
    return jnp.maximum(x, 0.0) + jnp.log1p(jnp.exp(-jnp.abs(x)))


def _sigmoid(x):
    return 0.5 * jnp.tanh(0.5 * x) + 0.5


def _silu(x):
    hx = 0.5 * x
    return hx * jnp.tanh(hx) + hx


def _residual(x, y, fg_ref, final):
    r = x + y
    if final:
        r = _rms(r, fg_ref[...])
    return r


def _chunk_tri(n, chunk):
    r = lax.broadcasted_iota(jnp.int32, (n, n), 0)
    c = lax.broadcasted_iota(jnp.int32, (n, n), 1)
    keep = jnp.logical_and(r // chunk == c // chunk, c <= r)
    return jnp.where(keep, 1.0, 0.0).astype(BF16)


def _split_bf16(x, parts):
    out = []
    for _ in range(parts - 1):
        hi = x.astype(BF16)
        out.append(hi)
        x = x - hi.astype(F32)
    out.append(x.astype(BF16))
    return out


def _const_spec(shape):
    n = len(shape)
    return pl.BlockSpec(shape, lambda *_: (0,) * n, pipeline_mode=pl.Buffered(1))


def _seq_spec(tm, d):
    return pl.BlockSpec((None, tm, d), lambda b, j: (b, j, 0))


def _lru_kernel(h_ref, ng_ref, win_ref, cw_ref, cb_ref, wax_ref, ba_ref, bx_ref, lam_ref, wout_ref,
                fg_ref, o_ref, ubuf, abuf, bbuf, gbuf, hst, *, tm, nb, width, final):
    hist = (CONV_W - 1) * nb
    tt = tm // nb

    @pl.when(pl.program_id(0) == 0)
    def _():
        ubuf[0:hist, :] = jnp.zeros((hist, width), F32)
        hst[...] = jnp.zeros_like(hst)

    x = pltpu.einshape("btd->tbd", h_ref[...]).reshape(tm, h_ref.shape[-1])
    un = _rms(x, ng_ref[...]).astype(BF16)
    ubuf[hist:hist + tm, :] = _dot(un, win_ref[:, width:])
    gbuf[...] = _silu(_dot(un, win_ref[:, :width]))
    cw = cw_ref[...]
    conv = cb_ref[...] + cw[CONV_W - 1:CONV_W] * ubuf[hist:hist + tm, :]
    for k in range(CONV_W - 1):
        d = (CONV_W - 1 - k) * nb
        conv = conv + cw[k:k + 1] * ubuf[hist - d:hist - d + tm, :]
    ubuf[0:hist, :] = ubuf[tm:tm + hist, :]

    sp = _softplus(-lam_ref[...])
    for n in range(width // LANES):
        sl = slice(n * LANES, (n + 1) * LANES)
        cn = conv[:, sl]
        ra = _dot(cn.astype(BF16), wax_ref[n])
        r = _sigmoid(ra[:, :LANES] + ba_ref[:, sl])
        i = _sigmoid(ra[:, LANES:] + bx_ref[:, sl])
        log_a = -LRU_C * r * sp[:, sl]
        th = jnp.tanh(log_a)
        u = -2.0 * th
        mult = jnp.where(u > 0.0, u * lax.rsqrt(u * (1.0 - th)), 0.0)
        abuf[:, sl] = jnp.exp(log_a)
        bbuf[:, sl] = mult * (i * cn)

    hs = hst[...]
    for t in range(tm // nb):
        rs = slice(t * nb, (t + 1) * nb)
        hs = abuf[rs, :] * hs + bbuf[rs, :]
        bbuf[rs, :] = hs
    hst[...] = hs
    y = (bbuf[...] * gbuf[...]).astype(BF16)
    out = _residual(x, _dot(y, wout_ref[...]), fg_ref, final)
    o_ref[...] = pltpu.einshape("tbd->btd", out.reshape(tt, nb, out.shape[-1]))


def _lru_layer(h, ng, fg, w_in, conv_w, conv_b, w_a, b_a, w_x, b_x, lam, w_out, *, final, tm=1024):
    B, S, D = h.shape
    assert B == SUBLANES, "time-major RG-LRU tiling needs one sublane tile per time step"
    width = w_out.shape[0]
    wax = jnp.concatenate([w_a, w_x], axis=-1).astype(BF16)
    row = lambda v: v.reshape(1, -1).astype(F32)
    kern = functools.partial(_lru_kernel, tm=tm, nb=B, width=width, final=final)
    hist = (CONV_W - 1) * B
    tile = pl.BlockSpec((B, tm // B, D), lambda j: (0, j, 0))
    return pl.pallas_call(
        kern,
        grid=(S * B // tm,),
        in_specs=[tile, _const_spec((1, D)), _const_spec((D, 2 * width)),
                  _const_spec((CONV_W, width)), _const_spec((1, width)), _const_spec(wax.shape),
                  _const_spec((1, width)), _const_spec((1, width)), _const_spec((1, width)),
                  _const_spec((width, D)), _const_spec((1, D))],
        out_specs=tile,
        out_shape=jax.ShapeDtypeStruct((B, S, D), F32),
        scratch_shapes=[pltpu.VMEM((tm + hist, width), F32), pltpu.VMEM((tm, width), F32),
                        pltpu.VMEM((tm, width), F32), pltpu.VMEM((tm, width), F32),
                        pltpu.VMEM((B, width), F32)],
        compiler_params=pltpu.CompilerParams(dimension_semantics=("arbitrary",),
                                             vmem_limit_bytes=VMEM_LIMIT),
        name="lru_layer",
    )(h, row(ng), w_in.astype(BF16), conv_w.astype(F32), row(conv_b), wax, row(b_a), row(b_x),
      row(lam), w_out.astype(BF16), row(fg))


def _gla_kernel(h_ref, ng_ref, wq_ref, wgk_ref, wgk2_ref, bgk_ref, go_ref, wout_ref, fg_ref, o_ref,
                sst, obuf, gbuf, *, tm, key, val, final):
    heads, chunk = GLA_HEADS, GLA_CHUNK
    dk, dv = key // heads, val // heads

    @pl.when(pl.program_id(1) == 0)
    def _():
        sst[...] = jnp.zeros_like(sst)

    x = h_ref[...]
    un = _rms(x, ng_ref[...]).astype(BF16)
    q = _dot(un, wq_ref[:, :key]) * dk ** -0.5
    k = _dot(un, wq_ref[:, key:2 * key])
    gk = _dot(un, wgk_ref[...]).astype(BF16)
    log_a = -_softplus(-(_dot(gk, wgk2_ref[...]) + bgk_ref[...])) * (1.0 / GLA_TAU)
    v = _dot(un, wq_ref[:, 2 * key:2 * key + val]).astype(BF16)
    gbuf[...] = _silu(_dot(un, wq_ref[:, 2 * key + val:]))
    tri = _chunk_tri(tm, chunk)
    b = sum(_dot(tri, part) for part in _split_bf16(log_a, 2))

    rr = lax.broadcasted_iota(jnp.int32, (chunk, chunk), 0)
    cc = lax.broadcasted_iota(jnp.int32, (chunk, chunk), 1)
    causal = cc <= rr
    nch = tm // chunk
    ksl = [slice(hh * dk, (hh + 1) * dk) for hh in range(heads)]
    vsl = [slice(hh * dv, (hh + 1) * dv) for hh in range(heads)]
    rows = [slice(ci * chunk, (ci + 1) * chunk) for ci in range(nch)]
    qt, dec, att, ds = [], [], [], []
    for rs in rows:
        bc = b[rs]
        bl = bc[chunk - 1:chunk, :]
        qt.append((q[rs] * jnp.exp(bc)).astype(BF16))
        kt = (k[rs] * jnp.exp(-bc)).astype(BF16)
        ke = (k[rs] * jnp.exp(bl - bc)).astype(BF16)
        dec.append(jnp.exp(bl))
        att.append([_dot_nt(qt[-1][:, ks], kt[:, ks]) for ks in ksl])
        ds.append([_dot_tn(v[rs, vsl[hh]], ke[:, ksl[hh]]) for hh in range(heads)])
    for hh in range(heads):
        s_in = sst[hh]
        for ci, rs in enumerate(rows):
            p = jnp.where(causal, att[ci][hh], 0.0).astype(BF16)
            obuf[rs, vsl[hh]] = _dot(p, v[rs, vsl[hh]]) + _dot_nt(qt[ci][:, ksl[hh]], s_in.astype(BF16))
            s_in = dec[ci][:, ksl[hh]] * s_in + ds[ci][hh]
        sst[hh] = s_in

    parts = []
    for hh in range(heads):
        oh = obuf[:, hh * dv:(hh + 1) * dv]
        parts.append(oh * lax.rsqrt(jnp.mean(oh * oh, axis=-1, keepdims=True) + NORM_EPS))
    on = jnp.concatenate(parts, axis=1) * go_ref[...]
    y = (on * gbuf[...]).astype(BF16)
    o_ref[...] = _residual(x, _dot(y, wout_ref[...]), fg_ref, final)


def _gla_layer(h, ng, fg, w_in, w_gk2, b_gk, g_o, w_out, *, final, tm=512):
    B, S, D = h.shape
    val = w_out.shape[0]
    key = (w_in.shape[1] - 2 * val - GLA_GATE_RANK) // 2
    main = 2 * key + 2 * val
    pad = LANES - GLA_GATE_RANK
    w_in = w_in.astype(BF16)
    wq = w_in[:, :main]
    wgk = jnp.pad(w_in[:, main:], ((0, 0), (0, pad)))
    wgk2 = jnp.pad(w_gk2.astype(BF16), ((0, pad), (0, 0)))
    row = lambda v: v.reshape(1, -1).astype(F32)
    kern = functools.partial(_gla_kernel, tm=tm, key=key, val=val, final=final)
    return pl.pallas_call(
        kern,
        grid=(B, S // tm),
        in_specs=[_seq_spec(tm, D), _const_spec((1, D)), _const_spec((D, main)), _const_spec((D, LANES)),
                  _const_spec((LANES, key)), _const_spec((1, key)), _const_spec((1, val)),
                  _const_spec((val, D)), _const_spec((1, D))],
        out_specs=_seq_spec(tm, D),
        out_shape=jax.ShapeDtypeStruct((B, S, D), F32),
        scratch_shapes=[pltpu.VMEM((GLA_HEADS, val // GLA_HEADS, key // GLA_HEADS), F32),
                        pltpu.VMEM((tm, val), F32), pltpu.VMEM((tm, val), F32)],
        compiler_params=pltpu.CompilerParams(dimension_semantics=("arbitrary", "arbitrary"),
                                             vmem_limit_bytes=VMEM_LIMIT),
        name="gla_layer",
    )(h, row(ng), wq, wgk, wgk2, row(b_gk), row(jnp.tile(g_o, GLA_HEADS)), w_out.astype(BF16), row(fg))


def _ssd_steps_kernel(h_ref, *refs, tm, inner, final):
    consts, o_ref, scratch = refs[:11], refs[11], refs[12:]

    def tile_body(sub, carry):
        rows = pl.ds(pl.multiple_of(sub * tm, tm), tm)
        first = jnp.logical_and(pl.program_id(1) == 0, sub == 0)
        _ssd_kernel(first, h_ref.at[rows, :], *consts, o_ref.at[rows, :], *scratch,
                    tm=tm, inner=inner, final=final)
        return carry

    lax.fori_loop(0, h_ref.shape[0] // tm, tile_body, 0)


def _ssd_kernel(first, h_ref, ng_ref, win_ref, cw_ref, cb_ref, dtb_ref, alog_ref, dexp_ref, gn_ref, e_ref,
                wout_ref, fg_ref, o_ref, xbuf, cbuf, csx, dtx, cst, dtt, zbuf, ybuf, sst, *, tm, inner, final):
    groups, ns, chunk = SSD_GROUPS, SSD_STATE, SSD_CHUNK
    gw = inner // groups
    conv_dim = inner + 2 * groups * ns
    half = SSD_HEADDIM

    @pl.when(first)
    def _():
        xbuf[:, 0:SUBLANES, :] = jnp.zeros((conv_dim // LANES, SUBLANES, LANES), F32)
        sst[...] = jnp.zeros_like(sst)

    x = h_ref[...]
    un = _rms(x, ng_ref[...]).astype(BF16)

    dt_raw = _dot(un, win_ref[:, inner + conv_dim:])
    dt = _softplus(dt_raw + dtb_ref[...])
    da = dt * (-jnp.exp(alog_ref[...]))
    tri = _chunk_tri(tm, chunk)
    cs = sum(_dot(tri, part) for part in _split_bf16(da, 3))
    e = e_ref[...]
    csx[...] = sum(_dot(part, e) for part in _split_bf16(cs, 2))
    dtx[...] = _dot(dt.astype(BF16), e)
    cst[...] = cs.T
    dtt[...] = dt.T

    for c0 in range(0, conv_dim, SSD_CONV_BLOCK):
        proj = _dot(un, win_ref[:, inner + c0:inner + c0 + SSD_CONV_BLOCK])
        for c in range(SSD_CONV_BLOCK // LANES):
            xbuf[c0 // LANES + c, SUBLANES:SUBLANES + tm, :] = proj[:, c * LANES:(c + 1) * LANES]
        for c in range(SSD_CONV_BLOCK // LANES):
            t = c0 // LANES + c
            cl = slice(c0 + c * LANES, c0 + (c + 1) * LANES)
            conv = cb_ref[:, cl] + cw_ref[CONV_W - 1:CONV_W, cl] * xbuf[t, SUBLANES:SUBLANES + tm, :]
            for k in range(CONV_W - 1):
                d = CONV_W - 1 - k
                conv = conv + cw_ref[k:k + 1, cl] * xbuf[t, pl.ds(SUBLANES - d, tm, stride=1), :]
            xbuf[t, 0:SUBLANES, :] = xbuf[t, tm:tm + SUBLANES, :]
            cbuf[:, cl] = _silu(conv)
    zbuf[...] = _silu(_dot(un, win_ref[:, 0:inner]))

    ii = lax.broadcasted_iota(jnp.int32, (chunk, LANES), 0)
    ll = lax.broadcasted_iota(jnp.int32, (chunk, LANES), 1)
    causal2 = ll % half <= ii
    low = ll < half
    low_row = lax.broadcasted_iota(jnp.int32, (1, LANES), 1) < half
    dexp = dexp_ref[...]

    def lane_pair(ref, h0, ci):
        t0 = (ci * chunk // LANES) * LANES
        a = ref[h0:h0 + 1, t0:t0 + LANES]
        b = ref[h0 + 1:h0 + 2, t0:t0 + LANES]
        if (ci * chunk) % LANES == 0:
            return jnp.where(low_row, a, pltpu.roll(b, half, 1))
        return jnp.where(low_row, pltpu.roll(a, half, 1), b)

    for ci in range(tm // chunk):
        rs = slice(ci * chunk, (ci + 1) * chunk)
        cs_last = csx[(ci + 1) * chunk - 1:(ci + 1) * chunk, :]
        bgs, cgs, cb2s, y_offs, s_prevs = [], [], [], [], []
        for g in range(groups):
            bg = cbuf[rs, inner + g * ns:inner + (g + 1) * ns].astype(BF16)
            cg = cbuf[rs, inner + (groups + g) * ns:inner + (groups + g + 1) * ns].astype(BF16)
            s_prev = sst[g]
            bgs.append(bg)
            cgs.append(cg)
            s_prevs.append(s_prev)
            cb2s.append(_dot_nt(cg, jnp.concatenate([bg, bg], axis=0)))
            y_offs.append(_dot(cg, s_prev.astype(BF16)))
        for g in range(groups):
            gl = slice(g * gw, (g + 1) * gw)
            xg = cbuf[rs, gl]
            csg = csx[rs, gl]
            yd = []
            for pp in range(gw // LANES):
                ps = slice(pp * LANES, (pp + 1) * LANES)
                h0 = (g * gw + pp * LANES) // SSD_HEADDIM
                row = lane_pair(cst, h0, ci)
                dtrow = lane_pair(dtt, h0, ci)
                lm = jnp.where(causal2, jnp.exp(csg[:, ps] - row), 0.0)
                w = (cb2s[g] * lm * dtrow).astype(BF16)
                xp = xg[:, ps]
                bd = jnp.concatenate([jnp.where(low, xp, 0.0), jnp.where(low, 0.0, xp)],
                                     axis=0).astype(BF16)
                yd.append(_dot(w, bd))
            ybuf[rs, gl] = jnp.concatenate(yd, axis=1) + y_offs[g] * jnp.exp(csg) + dexp[:, gl] * xg
        for g in range(groups):
            gl = slice(g * gw, (g + 1) * gw)
            cl = cs_last[:, gl]
            w1 = jnp.exp(cl - csx[rs, gl]) * dtx[rs, gl]
            sst[g] = s_prevs[g] * jnp.exp(cl) + _dot_tn(bgs[g], (cbuf[rs, gl] * w1).astype(BF16))

    acc = x
    for g in range(groups):
        gl = slice(g * gw, (g + 1) * gw)
        yg = ybuf[:, gl] * zbuf[:, gl]
        yn = yg * lax.rsqrt(jnp.mean(yg * yg, axis=-1, keepdims=True) + NORM_EPS) * gn_ref[:, gl]
        acc = acc + _dot(yn.astype(BF16), wout_ref[gl, :])
    if final:
        acc = _rms(acc, fg_ref[...])
    o_ref[...] = acc


def _ssd_layer(h, ng, fg, w_in, conv_w, conv_b, dt_bias, a_log, d_skip, g_norm, w_out, *, final, tm=256):
    B, S, D = h.shape
    inner = w_out.shape[0]
    heads = inner // SSD_HEADDIM
    conv_dim = inner + 2 * SSD_GROUPS * SSD_STATE
    pad = LANES - heads
    win = jnp.pad(w_in.astype(BF16), ((0, 0), (0, pad)))
    row = lambda v: v.reshape(1, -1).astype(F32)
    padrow = lambda v: jnp.pad(v.astype(F32), (0, pad)).reshape(1, LANES)
    expand = (lax.broadcasted_iota(jnp.int32, (LANES, inner), 1) // SSD_HEADDIM
              == lax.broadcasted_iota(jnp.int32, (LANES, inner), 0)).astype(BF16)
    kern = functools.partial(_ssd_steps_kernel, tm=tm, inner=inner, final=final)
    gw = inner // SSD_GROUPS
    step = tm * SSD_SUBTILES
    return pl.pallas_call(
        kern,
        grid=(B, S // step),
        in_specs=[_seq_spec(step, D), _const_spec((1, D)), _const_spec(win.shape),
                  _const_spec((CONV_W, conv_dim)), _const_spec((1, conv_dim)), _const_spec((1, LANES)),
                  _const_spec((1, LANES)), _const_spec((1, inner)), _const_spec((1, inner)),
                  _const_spec((LANES, inner)), _const_spec((inner, D)), _const_spec((1, D))],
        out_specs=_seq_spec(step, D),
        out_shape=jax.ShapeDtypeStruct((B, S, D), F32),
        scratch_shapes=[pltpu.VMEM((conv_dim // LANES, tm + SUBLANES, LANES), F32), pltpu.VMEM((tm, conv_dim), F32),
                        pltpu.VMEM((tm, inner), F32), pltpu.VMEM((tm, inner), F32),
                        pltpu.VMEM((LANES, tm), F32), pltpu.VMEM((LANES, tm), F32),
                        pltpu.VMEM((tm, inner), F32), pltpu.VMEM((tm, inner), F32),
                        pltpu.VMEM((SSD_GROUPS, SSD_STATE, gw), F32)],
        compiler_params=pltpu.CompilerParams(dimension_semantics=("arbitrary", "arbitrary"),
                                             vmem_limit_bytes=VMEM_LIMIT),
        name="ssd_layer",
    )(h, row(ng), win, conv_w.astype(F32), row(conv_b), padrow(dt_bias), padrow(a_log),
      row(jnp.repeat(d_skip, SSD_HEADDIM)), row(g_norm), expand, w_out.astype(BF16), row(fg))


def _mla_proj_kernel(h_ref, pos_ref, ng_ref, win_ref, gq_ref, wuq_ref, gkv_ref, wukv_ref, invf_ref,
                     sgn_ref, q_ref, k_ref, v_ref, g_ref, *, tm):
    heads = MLA_HEADS
    kw = heads * LANES
    x = h_ref[...]
    un = _rms(x, ng_ref[...]).astype(BF16)
    t = _dot(un, win_ref[...])
    c_q = t[:, :MLA_Q_RANK]
    c_kv = t[:, MLA_Q_RANK:MLA_Q_RANK + MLA_KV_RANK]
    g0 = MLA_Q_RANK + MLA_KV_RANK
    gate = t[:, g0:g0 + heads * MLA_V]
    k_r = t[:, g0 + heads * MLA_V:]
    g_ref[...] = _silu(gate).astype(BF16)

    ang = pos_ref[...].astype(F32) * invf_ref[...]
    cos = jnp.cos(ang)
    sin = jnp.sin(ang) * sgn_ref[...]
    lane = lax.broadcasted_iota(jnp.int32, (tm, LANES), 1)
    first_half = lane < MLA_NOPE + MLA_ROPE // 2

    def rope(a):
        swapped = jnp.where(first_half, pltpu.roll(a, LANES - MLA_ROPE // 2, 1),
                            pltpu.roll(a, MLA_ROPE // 2, 1))
        return a * cos + swapped * sin

    k_rope = rope(k_r)
    qn = _dot(_rms(c_q, gq_ref[...]).astype(BF16), wuq_ref[...])
    kvn = _dot(_rms(c_kv, gkv_ref[...]).astype(BF16), wukv_ref[...])
    scale = (MLA_NOPE + MLA_ROPE) ** -0.5 * LOG2_E
    for hh in range(heads):
        hs = slice(hh * LANES, (hh + 1) * LANES)
        q_ref[0, hh] = (rope(qn[:, hs]) * scale).astype(BF16)
        k_ref[0, hh] = (kvn[:, hs] + k_rope).astype(BF16)
    tk = v_ref.shape[-1]
    ones_row = jnp.where(lax.broadcasted_iota(jnp.int32, (MLA_VROWS - MLA_V, tk), 0) == 0, 1.0, 0.0)
    for hp in range(heads // 2):
        vt = kvn[:, kw + hp * LANES:kw + (hp + 1) * LANES].T
        for e in range(2):
            for t in range(tm // tk):
                v_ref[0, 2 * hp + e, t] = jnp.concatenate(
                    [vt[e * MLA_V:(e + 1) * MLA_V, t * tk:(t + 1) * tk], ones_row], axis=0).astype(BF16)


def _mla_attn_kernel(q_ref, k_ref, vt_ref, g_ref, h_ref, wout_ref, fg_ref, o_ref, m_sc, acc_sc, obuf,
                     *, tq, final):
    heads = MLA_HEADS
    j = pl.program_id(1)
    key_idx = lax.broadcasted_iota(jnp.int32, (tq, tq), 0)
    qry_idx = lax.broadcasted_iota(jnp.int32, (tq, tq), 1)
    diag = key_idx <= qry_idx
    m_sc[...] = jnp.full(m_sc.shape, -jnp.inf, F32)
    acc_sc[...] = jnp.zeros(acc_sc.shape, F32)

    def kv_tiles(tiles):
        units = [(kt, masked, hh) for kt, masked in tiles for hh in range(heads)]

        def scores(unit):
            kt, _, hh = unit
            k0 = pl.multiple_of(kt * tq, tq)
            return _dot_nt(k_ref[0, hh, pl.ds(k0, tq), :], q_ref[0, hh])

        ready = [scores(u) for u in units[:MLA_LOOKAHEAD]]
        for n, (kt, masked, hh) in enumerate(units):
            if n % MLA_LOOKAHEAD == 0:
                ready.extend(scores(u) for u in units[n + MLA_LOOKAHEAD:n + 2 * MLA_LOOKAHEAD])
            st = ready.pop(0)
            if masked:
                st = jnp.where(diag, st, -jnp.inf)
            m_old = m_sc[hh]
            m_new = jnp.maximum(m_old, jnp.max(st, axis=0, keepdims=True))
            pt = jnp.exp2(st - m_new).astype(BF16)
            acc_sc[hh] = jnp.exp2(m_old - m_new) * acc_sc[hh] + _dot(vt_ref[0, hh, kt], pt)
            m_sc[hh] = m_new

    def full_pair(p, carry):
        kv_tiles([(2 * p, False), (2 * p + 1, False)])
        return carry

    lax.fori_loop(0, j // 2, full_pair, 0)

    @pl.when(j % 2 == 1)
    def _():
        kv_tiles([(j - 1, False), (j, True)])

    @pl.when(j % 2 == 0)
    def _():
        kv_tiles([(j, True)])

    for hp in range(heads // 2):
        halves = []
        for e in range(2):
            acc = acc_sc[2 * hp + e]
            halves.append(acc[:MLA_V] * (1.0 / acc[MLA_V:MLA_V + 1]))
        obuf[:, hp * LANES:(hp + 1) * LANES] = jnp.concatenate(halves, axis=0).T
    y = (obuf[...] * g_ref[...].astype(F32)).astype(BF16)
    o_ref[...] = _residual(h_ref[...], _dot(y, wout_ref[...]), fg_ref, final)


def _mla_layer(h, positions, ng, fg, w_in, g_q, w_uq, g_kv, w_ukv, w_out, *, final, tq=256):
    B, S, D = h.shape
    heads, half = MLA_HEADS, MLA_ROPE // 2
    qk = MLA_NOPE + MLA_ROPE
    width = heads * MLA_V
    c0 = MLA_Q_RANK + MLA_KV_RANK
    w_in = w_in.astype(BF16)
    w_kr = jnp.pad(w_in[:, c0:c0 + MLA_ROPE], ((0, 0), (MLA_NOPE, LANES - qk)))
    win = jnp.concatenate([w_in[:, :c0], w_in[:, c0 + MLA_ROPE:], w_kr], axis=1)
    wuq = jnp.pad(w_uq.reshape(MLA_Q_RANK, heads, qk), ((0, 0), (0, 0), (0, LANES - qk)))
    wuq = wuq.reshape(MLA_Q_RANK, heads * LANES).astype(BF16)
    wkv = w_ukv.reshape(MLA_KV_RANK, heads, MLA_NOPE + MLA_V)
    wk = jnp.pad(wkv[:, :, :MLA_NOPE], ((0, 0), (0, 0), (0, LANES - MLA_NOPE)))
    wukv = jnp.concatenate([wk.reshape(MLA_KV_RANK, heads * LANES),
                            wkv[:, :, MLA_NOPE:].reshape(MLA_KV_RANK, width)], axis=1).astype(BF16)
    inv_freq = ROPE_THETA ** (-jnp.arange(0, MLA_ROPE, 2, dtype=F32) / MLA_ROPE)
    zeros = lambda n: jnp.zeros((n,), F32)
    invf = jnp.concatenate([zeros(MLA_NOPE), inv_freq, inv_freq, zeros(LANES - qk)]).reshape(1, LANES)
    sgn = jnp.concatenate([jnp.ones((MLA_NOPE,), F32), -jnp.ones((half,), F32),
                           jnp.ones((LANES - MLA_NOPE - half,), F32)]).reshape(1, LANES)
    row = lambda v: v.reshape(1, -1).astype(F32)

    tm = tq * MLA_PROJ_TILES
    proj_spec = pl.BlockSpec((1, heads, tm, LANES), lambda b, j: (b, 0, j, 0))
    head_spec = pl.BlockSpec((1, heads, tq, LANES), lambda b, j: (b, 0, j, 0))
    vt_spec = pl.BlockSpec((1, heads, MLA_PROJ_TILES, MLA_VROWS, tq), lambda b, j: (b, 0, j, 0, 0))
    q, k, vt, g = pl.pallas_call(
        functools.partial(_mla_proj_kernel, tm=tm),
        grid=(B, S // tm),
        in_specs=[_seq_spec(tm, D), _seq_spec(tm, 1), _const_spec((1, D)), _const_spec(win.shape),
                  _const_spec((1, MLA_Q_RANK)), _const_spec(wuq.shape), _const_spec((1, MLA_KV_RANK)),
                  _const_spec(wukv.shape), _const_spec((1, LANES)), _const_spec((1, LANES))],
        out_specs=[proj_spec, proj_spec, vt_spec, _seq_spec(tm, width)],
        out_shape=[jax.ShapeDtypeStruct((B, heads, S, LANES), BF16),
                   jax.ShapeDtypeStruct((B, heads, S, LANES), BF16),
                   jax.ShapeDtypeStruct((B, heads, S // tq, MLA_VROWS, tq), BF16),
                   jax.ShapeDtypeStruct((B, S, width), BF16)],
        compiler_params=pltpu.CompilerParams(dimension_semantics=("arbitrary", "arbitrary"),
                                             vmem_limit_bytes=VMEM_LIMIT),
        name="mla_proj",
    )(h, positions.reshape(B, S, 1), row(ng), win, row(g_q), wuq, row(g_kv), wukv, invf, sgn)

    return pl.pallas_call(
        functools.partial(_mla_attn_kernel, tq=tq, final=final),
        grid=(B, S // tq),
        in_specs=[head_spec, pl.BlockSpec((1, heads, S, LANES), lambda b, j: (b, 0, 0, 0)),
                  pl.BlockSpec((1, heads, S // tq, MLA_VROWS, tq), lambda b, j: (b, 0, 0, 0, 0)),
                  _seq_spec(tq, width), _seq_spec(tq, D), _const_spec((width, D)), _const_spec((1, D))],
        out_specs=_seq_spec(tq, D),
        out_shape=jax.ShapeDtypeStruct((B, S, D), F32),
        scratch_shapes=[pltpu.VMEM((heads, 1, tq), F32), pltpu.VMEM((heads, MLA_VROWS, tq), F32),
                        pltpu.VMEM((tq, width), F32)],
        compiler_params=pltpu.CompilerParams(dimension_semantics=("arbitrary", "arbitrary"),
                                             vmem_limit_bytes=VMEM_LIMIT),
        name="mla_attn",
    )(q, k, vt, g, h, w_out.astype(BF16), row(fg))


def kernel(x, positions, norm_g, final_g, mla_w_in, mla_g_q, mla_w_uq, mla_g_kv, mla_w_ukv, mla_w_out, gla_w_in, gla_w_gk2, gla_b_gk, gla_g_o, gla_w_out, lru_w_in, lru_conv_w, lru_conv_b, lru_w_a, lru_b_a, lru_w_x, lru_b_x, lru_lam, lru_w_out, ssd_w_in, ssd_conv_w, ssd_conv_b, ssd_dt_bias, ssd_a_log, ssd_d, ssd_g_norm, ssd_w_out):
    depth = norm_g.shape[0]
    h = x
    for i in range(depth):
        m, j = i % N_MIXERS, i // N_MIXERS
        common = dict(final=(i == depth - 1))
        if m == 0:
            h = _mla_layer(h, positions, norm_g[i], final_g, mla_w_in[j], mla_g_q[j], mla_w_uq[j],
                           mla_g_kv[j], mla_w_ukv[j], mla_w_out[j], **common)
        elif m == 1:
            h = _gla_layer(h, norm_g[i], final_g, gla_w_in[j], gla_w_gk2[j], gla_b_gk[j], gla_g_o[j],
                           gla_w_out[j], **common)
        elif m == 2:
            h = _lru_layer(h, norm_g[i], final_g, lru_w_in[j], lru_conv_w[j], lru_conv_b[j], lru_w_a[j],
                           lru_b_a[j], lru_w_x[j], lru_b_x[j], lru_lam[j], lru_w_out[j], **common)
        else:
            h = _ssd_layer(h, norm_g[i], final_g, ssd_w_in[j], ssd_conv_w[j], ssd_conv_b[j],
                           ssd_dt_bias[j], ssd_a_log[j], ssd_d[j], ssd_g_norm[j], ssd_w_out[j], **common)
    return h
```

```python
import functools

import jax
import jax.numpy as jnp
from jax import lax
from jax.experimental import pallas as pl
from jax.experimental.pallas import tpu as pltpu

F32 = jnp.float32
BF16 = jnp.bfloat16

NORM_EPS = 1e-6
N_MIXERS = 4
MLA_HEADS = 16
MLA_Q_RANK = 384
MLA_KV_RANK = 256
MLA_NOPE = 64
MLA_ROPE = 32
MLA_V = 64
MLA_VROWS = 80
MLA_PROJ_TILES = 2
MLA_Q_TILES = 2
MLA_LOOKAHEAD = 4
ROPE_THETA = 10000.0
GLA_HEADS = 4
GLA_GATE_RANK = 16
GLA_TAU = 16.0
GLA_CHUNK = 64
LRU_BLOCKS = 10
LRU_C = 8.0
CONV_W = 4
SSD_HEADDIM = 64
SSD_GROUPS = 8
SSD_STATE = 128
SSD_CHUNK = 64
SSD_CONV_BLOCK = 1024

LOG2_E = 1.4426950408889634
LANES = 128
SUBLANES = 8
VMEM_LIMIT = 56 * 1024 * 1024


def _dot(a, b):
    return jnp.dot(a, b, preferred_element_type=F32)


def _dot_nt(a, b):
    return lax.dot_general(a, b, (((1,), (1,)), ((), ())), preferred_element_type=F32)


def _dot_tn(a, b):
    return lax.dot_general(a, b, (((0,), (0,)), ((), ())), preferred_element_type=F32)


def _rms(x, g):
    ms = jnp.mean(x * x, axis=-1, keepdims=True)
    return x * lax.rsqrt(ms + NORM_EPS) * g


def _softplus(x):
    return jnp.maximum(x, 0.0) + jnp.log1p(jnp.exp(-jnp.abs(x)))


def _sigmoid(x):
    return 0.5 * jnp.tanh(0.5 * x) + 0.5


def _silu(x):
    hx = 0.5 * x
    return hx * jnp.tanh(hx) + hx


def _residual(x, y, fg_ref, final):
    r = x + y
    if final:
        r = _rms(r, fg_ref[...])
    return r


def _chunk_tri(n, chunk):
    r = lax.broadcasted_iota(jnp.int32, (n, n), 0)
    c = lax.broadcasted_iota(jnp.int32, (n, n), 1)
    keep = jnp.logical_and(r // chunk == c // chunk, c <= r)
    return jnp.where(keep, 1.0, 0.0).astype(BF16)


def _split_bf16(x, parts):
    out = []
    for _ in range(parts - 1):
        hi = x.astype(BF16)
        out.append(hi)
        x = x - hi.astype(F32)
    out.append(x.astype(BF16))
    return out


def _const_spec(shape):
    n = len(shape)
    return pl.BlockSpec(shape, lambda *_: (0,) * n, pipeline_mode=pl.Buffered(1))


def _seq_spec(tm, d):
    return pl.BlockSpec((None, tm, d), lambda b, j: (b, j, 0))


def _lru_kernel(h_ref, ng_ref, win_ref, cw_ref, cb_ref, wax_ref, ba_ref, bx_ref, lam_ref, wout_ref,
                fg_ref, o_ref, ubuf, abuf, bbuf, gbuf, hst, *, tm, nb, width, final):
    hist = (CONV_W - 1) * nb
    tt = tm // nb

    @pl.when(pl.program_id(0) == 0)
    def _():
        ubuf[0:hist, :] = jnp.zeros((hist, width), F32)
        hst[...] = jnp.zeros_like(hst)

    x = pltpu.einshape("btd->tbd", h_ref[...]).reshape(tm, h_ref.shape[-1])
    un = _rms(x, ng_ref[...]).astype(BF16)
    ubuf[hist:hist + tm, :] = _dot(un, win_ref[:, width:])
    gbuf[...] = _silu(_dot(un, win_ref[:, :width]))
    cw = cw_ref[...]
    conv = cb_ref[...] + cw[CONV_W - 1:CONV_W] * ubuf[hist:hist + tm, :]
    for k in range(CONV_W - 1):
        d = (CONV_W - 1 - k) * nb
        conv = conv + cw[k:k + 1] * ubuf[hist - d:hist - d + tm, :]
    ubuf[0:hist, :] = ubuf[tm:tm + hist, :]

    sp = _softplus(-lam_ref[...])
    for n in range(width // LANES):
        sl = slice(n * LANES, (n + 1) * LANES)
        cn = conv[:, sl]
        ra = _dot(cn.astype(BF16), wax_ref[n])
        r = _sigmoid(ra[:, :LANES] + ba_ref[:, sl])
        i = _sigmoid(ra[:, LANES:] + bx_ref[:, sl])
        log_a = -LRU_C * r * sp[:, sl]
        th = jnp.tanh(log_a)
        u = -2.0 * th
        mult = jnp.where(u > 0.0, u * lax.rsqrt(u * (1.0 - th)), 0.0)
        abuf[:, sl] = jnp.exp(log_a)
        bbuf[:, sl] = mult * (i * cn)

    hs = hst[...]
    for t in range(tm // nb):
        rs = slice(t * nb, (t + 1) * nb)
        hs = abuf[rs, :] * hs + bbuf[rs, :]
        bbuf[rs, :] = hs
    hst[...] = hs
    y = (bbuf[...] * gbuf[...]).astype(BF16)
    out = _residual(x, _dot(y, wout_ref[...]), fg_ref, final)
    o_ref[...] = pltpu.einshape("tbd->btd", out.reshape(tt, nb, out.shape[-1]))


def _lru_layer(h, ng, fg, w_in, conv_w, conv_b, w_a, b_a, w_x, b_x, lam, w_out, *, final, tm=512):
    B, S, D = h.shape
    assert B == SUBLANES, "time-major RG-LRU tiling needs one sublane tile per time step"
    width = w_out.shape[0]
    wax = jnp.concatenate([w_a, w_x], axis=-1).astype(BF16)
    row = lambda v: v.reshape(1, -1).astype(F32)
    kern = functools.partial(_lru_kernel, tm=tm, nb=B, width=width, final=final)
    hist = (CONV_W - 1) * B
    tile = pl.BlockSpec((B, tm // B, D), lambda j: (0, j, 0))
    return pl.pallas_call(
        kern,
        grid=(S * B // tm,),
        in_specs=[tile, _const_spec((1, D)), _const_spec((D, 2 * width)),
                  _const_spec((CONV_W, width)), _const_spec((1, width)), _const_spec(wax.shape),
                  _const_spec((1, width)), _const_spec((1, width)), _const_spec((1, width)),
                  _const_spec((width, D)), _const_spec((1, D))],
        out_specs=tile,
        out_shape=jax.ShapeDtypeStruct((B, S, D), F32),
        scratch_shapes=[pltpu.VMEM((tm + hist, width), F32), pltpu.VMEM((tm, width), F32),
                        pltpu.VMEM((tm, width), F32), pltpu.VMEM((tm, width), F32),
                        pltpu.VMEM((B, width), F32)],
        compiler_params=pltpu.CompilerParams(dimension_semantics=("arbitrary",),
                                             vmem_limit_bytes=VMEM_LIMIT),
        name="lru_layer",
    )(h, row(ng), w_in.astype(BF16), conv_w.astype(F32), row(conv_b), wax, row(b_a), row(b_x),
      row(lam), w_out.astype(BF16), row(fg))


def _gla_kernel(h_ref, ng_ref, wq_ref, wgk_ref, wgk2_ref, bgk_ref, go_ref, wout_ref, fg_ref, o_ref,
                sst, obuf, gbuf, *, tm, key, val, final):
    heads, chunk = GLA_HEADS, GLA_CHUNK
    dk, dv = key // heads, val // heads

    @pl.when(pl.program_id(1) == 0)
    def _():
        sst[...] = jnp.zeros_like(sst)

    x = h_ref[...]
    un = _rms(x, ng_ref[...]).astype(BF16)
    q = _dot(un, wq_ref[:, :key]) * dk ** -0.5
    k = _dot(un, wq_ref[:, key:2 * key])
    gk = _dot(un, wgk_ref[...]).astype(BF16)
    log_a = -_softplus(-(_dot(gk, wgk2_ref[...]) + bgk_ref[...])) * (1.0 / GLA_TAU)
    v = _dot(un, wq_ref[:, 2 * key:2 * key + val]).astype(BF16)
    gbuf[...] = _silu(_dot(un, wq_ref[:, 2 * key + val:]))
    tri = _chunk_tri(tm, chunk)
    b = sum(_dot(tri, part) for part in _split_bf16(log_a, 2))

    rr = lax.broadcasted_iota(jnp.int32, (chunk, chunk), 0)
    cc = lax.broadcasted_iota(jnp.int32, (chunk, chunk), 1)
    causal = cc <= rr
    nch = tm // chunk
    ksl = [slice(hh * dk, (hh + 1) * dk) for hh in range(heads)]
    vsl = [slice(hh * dv, (hh + 1) * dv) for hh in range(heads)]
    rows = [slice(ci * chunk, (ci + 1) * chunk) for ci in range(nch)]
    qt, dec, att, ds = [], [], [], []
    for rs in rows:
        bc = b[rs]
        bl = bc[chunk - 1:chunk, :]
        qt.append((q[rs] * jnp.exp(bc)).astype(BF16))
        kt = (k[rs] * jnp.exp(-bc)).astype(BF16)
        ke = (k[rs] * jnp.exp(bl - bc)).astype(BF16)
        dec.append(jnp.exp(bl))
        att.append([_dot_nt(qt[-1][:, ks], kt[:, ks]) for ks in ksl])
        ds.append([_dot_tn(v[rs, vsl[hh]], ke[:, ksl[hh]]) for hh in range(heads)])
    for hh in range(heads):
        s_in = sst[hh]
        for ci, rs in enumerate(rows):
            p = jnp.where(causal, att[ci][hh], 0.0).astype(BF16)
            obuf[rs, vsl[hh]] = _dot(p, v[rs, vsl[hh]]) + _dot_nt(qt[ci][:, ksl[hh]], s_in.astype(BF16))
            s_in = dec[ci][:, ksl[hh]] * s_in + ds[ci][hh]
        sst[hh] = s_in

    parts = []
    for hh in range(heads):
        oh = obuf[:, hh * dv:(hh + 1) * dv]
        parts.append(oh * lax.rsqrt(jnp.mean(oh * oh, axis=-1, keepdims=True) + NORM_EPS))
    on = jnp.concatenate(parts, axis=1) * go_ref[...]
    y = (on * gbuf[...]).astype(BF16)
    o_ref[...] = _residual(x, _dot(y, wout_ref[...]), fg_ref, final)


def _gla_layer(h, ng, fg, w_in, w_gk2, b_gk, g_o, w_out, *, final, tm=512):
    B, S, D = h.shape
    val = w_out.shape[0]
    key = (w_in.shape[1] - 2 * val - GLA_GATE_RANK) // 2
    main = 2 * key + 2 * val
    pad = LANES - GLA_GATE_RANK
    w_in = w_in.astype(BF16)
    wq = w_in[:, :main]
    wgk = jnp.pad(w_in[:, main:], ((0, 0), (0, pad)))
    wgk2 = jnp.pad(w_gk2.astype(BF16), ((0, pad), (0, 0)))
    row = lambda v: v.reshape(1, -1).astype(F32)
    kern = functools.partial(_gla_kernel, tm=tm, key=key, val=val, final=final)
    return pl.pallas_call(
        kern,
        grid=(B, S // tm),
        in_specs=[_seq_spec(tm, D), _const_spec((1, D)), _const_spec((D, main)), _const_spec((D, LANES)),
                  _const_spec((LANES, key)), _const_spec((1, key)), _const_spec((1, val)),
                  _const_spec((val, D)), _const_spec((1, D))],
        out_specs=_seq_spec(tm, D),
        out_shape=jax.ShapeDtypeStruct((B, S, D), F32),
        scratch_shapes=[pltpu.VMEM((GLA_HEADS, val // GLA_HEADS, key // GLA_HEADS), F32),
                        pltpu.VMEM((tm, val), F32), pltpu.VMEM((tm, val), F32)],
        compiler_params=pltpu.CompilerParams(dimension_semantics=("arbitrary", "arbitrary"),
                                             vmem_limit_bytes=VMEM_LIMIT),
        name="gla_layer",
    )(h, row(ng), wq, wgk, wgk2, row(b_gk), row(jnp.tile(g_o, GLA_HEADS)), w_out.astype(BF16), row(fg))


def _ssd_kernel(h_ref, ng_ref, win_ref, cw_ref, cb_ref, dtb_ref, alog_ref, dexp_ref, gn_ref, e_ref,
                wout_ref, fg_ref, o_ref, xbuf, cbuf, csx, dtx, cst, dtt, zbuf, ybuf, sst, *, tm, inner, final):
    groups, ns, chunk = SSD_GROUPS, SSD_STATE, SSD_CHUNK
    gw = inner // groups
    conv_dim = inner + 2 * groups * ns
    half = SSD_HEADDIM

    @pl.when(pl.program_id(1) == 0)
    def _():
        xbuf[:, 0:SUBLANES, :] = jnp.zeros((conv_dim // LANES, SUBLANES, LANES), F32)
        sst[...] = jnp.zeros_like(sst)

    x = h_ref[...]
    un = _rms(x, ng_ref[...]).astype(BF16)

    dt_raw = _dot(un, win_ref[:, inner + conv_dim:])
    dt = _softplus(dt_raw + dtb_ref[...])
    da = dt * (-jnp.exp(alog_ref[...]))
    tri = _chunk_tri(tm, chunk)
    cs = sum(_dot(tri, part) for part in _split_bf16(da, 3))
    e = e_ref[...]
    csx[...] = sum(_dot(part, e) for part in _split_bf16(cs, 2))
    dtx[...] = _dot(dt.astype(BF16), e)
    cst[...] = cs.T
    dtt[...] = dt.T

    for c0 in range(0, conv_dim, SSD_CONV_BLOCK):
        proj = _dot(un, win_ref[:, inner + c0:inner + c0 + SSD_CONV_BLOCK])
        for c in range(SSD_CONV_BLOCK // LANES):
            xbuf[c0 // LANES + c, SUBLANES:SUBLANES + tm, :] = proj[:, c * LANES:(c + 1) * LANES]
        for c in range(SSD_CONV_BLOCK // LANES):
            t = c0 // LANES + c
            cl = slice(c0 + c * LANES, c0 + (c + 1) * LANES)
            conv = cb_ref[:, cl] + cw_ref[CONV_W - 1:CONV_W, cl] * xbuf[t, SUBLANES:SUBLANES + tm, :]
            for k in range(CONV_W - 1):
                d = CONV_W - 1 - k
                conv = conv + cw_ref[k:k + 1, cl] * xbuf[t, pl.ds(SUBLANES - d, tm, stride=1), :]
            xbuf[t, 0:SUBLANES, :] = xbuf[t, tm:tm + SUBLANES, :]
            cbuf[:, cl] = _silu(conv)
    zbuf[...] = _silu(_dot(un, win_ref[:, 0:inner]))

    ii = lax.broadcasted_iota(jnp.int32, (chunk, LANES), 0)
    ll = lax.broadcasted_iota(jnp.int32, (chunk, LANES), 1)
    causal2 = ll % half <= ii
    low = ll < half
    low_row = lax.broadcasted_iota(jnp.int32, (1, LANES), 1) < half
    dexp = dexp_ref[...]

    def lane_pair(ref, h0, ci):
        t0 = (ci * chunk // LANES) * LANES
        a = ref[h0:h0 + 1, t0:t0 + LANES]
        b = ref[h0 + 1:h0 + 2, t0:t0 + LANES]
        if (ci * chunk) % LANES == 0:
            return jnp.where(low_row, a, pltpu.roll(b, half, 1))
        return jnp.where(low_row, pltpu.roll(a, half, 1), b)

    for ci in range(tm // chunk):
        rs = slice(ci * chunk, (ci + 1) * chunk)
        cs_last = csx[(ci + 1) * chunk - 1:(ci + 1) * chunk, :]
        bgs, cgs, cb2s, y_offs, s_prevs = [], [], [], [], []
        for g in range(groups):
            bg = cbuf[rs, inner + g * ns:inner + (g + 1) * ns].astype(BF16)
            cg = cbuf[rs, inner + (groups + g) * ns:inner + (groups + g + 1) * ns].astype(BF16)
            s_prev = sst[g]
            bgs.append(bg)
            cgs.append(cg)
            s_prevs.append(s_prev)
            cb2s.append(_dot_nt(cg, jnp.concatenate([bg, bg], axis=0)))
            y_offs.append(_dot(cg, s_prev.astype(BF16)))
        for g in range(groups):
            gl = slice(g * gw, (g + 1) * gw)
            xg = cbuf[rs, gl]
            csg = csx[rs, gl]
            yd = []
            for pp in range(gw // LANES):
                ps = slice(pp * LANES, (pp + 1) * LANES)
                h0 = (g * gw + pp * LANES) // SSD_HEADDIM
                row = lane_pair(cst, h0, ci)
                dtrow = lane_pair(dtt, h0, ci)
                lm = jnp.where(causal2, jnp.exp(csg[:, ps] - row), 0.0)
                w = (cb2s[g] * lm * dtrow).astype(BF16)
                xp = xg[:, ps]
                bd = jnp.concatenate([jnp.where(low, xp, 0.0), jnp.where(low, 0.0, xp)],
                                     axis=0).astype(BF16)
                yd.append(_dot(w, bd))
            ybuf[rs, gl] = jnp.concatenate(yd, axis=1) + y_offs[g] * jnp.exp(csg) + dexp[:, gl] * xg
        for g in range(groups):
            gl = slice(g * gw, (g + 1) * gw)
            cl = cs_last[:, gl]
            w1 = jnp.exp(cl - csx[rs, gl]) * dtx[rs, gl]
            sst[g] = s_prevs[g] * jnp.exp(cl) + _dot_tn(bgs[g], (cbuf[rs, gl] * w1).astype(BF16))

    acc = x
    for g in range(groups):
        gl = slice(g * gw, (g + 1) * gw)
        yg = ybuf[:, gl] * zbuf[:, gl]
        yn = yg * lax.rsqrt(jnp.mean(yg * yg, axis=-1, keepdims=True) + NORM_EPS) * gn_ref[:, gl]
        acc = acc + _dot(yn.astype(BF16), wout_ref[gl, :])
    if final:
        acc = _rms(acc, fg_ref[...])
    o_ref[...] = acc


def _ssd_layer(h, ng, fg, w_in, conv_w, conv_b, dt_bias, a_log, d_skip, g_norm, w_out, *, final, tm=256):
    B, S, D = h.shape
    inner = w_out.shape[0]
    heads = inner // SSD_HEADDIM
    conv_dim = inner + 2 * SSD_GROUPS * SSD_STATE
    pad = LANES - heads
    win = jnp.pad(w_in.astype(BF16), ((0, 0), (0, pad)))
    row = lambda v: v.reshape(1, -1).astype(F32)
    padrow = lambda v: jnp.pad(v.astype(F32), (0, pad)).reshape(1, LANES)
    expand = (lax.broadcasted_iota(jnp.int32, (LANES, inner), 1) // SSD_HEADDIM
              == lax.broadcasted_iota(jnp.int32, (LANES, inner), 0)).astype(BF16)
    kern = functools.partial(_ssd_kernel, tm=tm, inner=inner, final=final)
    gw = inner // SSD_GROUPS
    return pl.pallas_call(
        kern,
        grid=(B, S // tm),
        in_specs=[_seq_spec(tm, D), _const_spec((1, D)), _const_spec(win.shape),
                  _const_spec((CONV_W, conv_dim)), _const_spec((1, conv_dim)), _const_spec((1, LANES)),
                  _const_spec((1, LANES)), _const_spec((1, inner)), _const_spec((1, inner)),
                  _const_spec((LANES, inner)), _const_spec((inner, D)), _const_spec((1, D))],
        out_specs=_seq_spec(tm, D),
        out_shape=jax.ShapeDtypeStruct((B, S, D), F32),
        scratch_shapes=[pltpu.VMEM((conv_dim // LANES, tm + SUBLANES, LANES), F32), pltpu.VMEM((tm, conv_dim), F32),
                        pltpu.VMEM((tm, inner), F32), pltpu.VMEM((tm, inner), F32),
                        pltpu.VMEM((LANES, tm), F32), pltpu.VMEM((LANES, tm), F32),
                        pltpu.VMEM((tm, inner), F32), pltpu.VMEM((tm, inner), F32),
                        pltpu.VMEM((SSD_GROUPS, SSD_STATE, gw), F32)],
        compiler_params=pltpu.CompilerParams(dimension_semantics=("arbitrary", "arbitrary"),
                                             vmem_limit_bytes=VMEM_LIMIT),
        name="ssd_layer",
    )(h, row(ng), win, conv_w.astype(F32), row(conv_b), padrow(dt_bias), padrow(a_log),
      row(jnp.repeat(d_skip, SSD_HEADDIM)), row(g_norm), expand, w_out.astype(BF16), row(fg))


def _mla_proj_kernel(h_ref, pos_ref, ng_ref, win_ref, gq_ref, wuq_ref, gkv_ref, wukv_ref, invf_ref,
                     sgn_ref, q_ref, k_ref, v_ref, g_ref, *, tm):
    heads = MLA_HEADS
    kw = heads * LANES
    x = h_ref[...]
    un = _rms(x, ng_ref[...]).astype(BF16)
    t = _dot(un, win_ref[...])
    c_q = t[:, :MLA_Q_RANK]
    c_kv = t[:, MLA_Q_RANK:MLA_Q_RANK + MLA_KV_RANK]
    g0 = MLA_Q_RANK + MLA_KV_RANK
    gate = t[:, g0:g0 + heads * MLA_V]
    k_r = t[:, g0 + heads * MLA_V:]
    g_ref[...] = _silu(gate).astype(BF16)

    ang = pos_ref[...].astype(F32) * invf_ref[...]
    cos = jnp.cos(ang)
    sin = jnp.sin(ang) * sgn_ref[...]
    lane = lax.broadcasted_iota(jnp.int32, (tm, LANES), 1)
    first_half = lane < MLA_NOPE + MLA_ROPE // 2

    def rope(a):
        swapped = jnp.where(first_half, pltpu.roll(a, LANES - MLA_ROPE // 2, 1),
                            pltpu.roll(a, MLA_ROPE // 2, 1))
        return a * cos + swapped * sin

    k_rope = rope(k_r)
    qn = _dot(_rms(c_q, gq_ref[...]).astype(BF16), wuq_ref[...])
    kvn = _dot(_rms(c_kv, gkv_ref[...]).astype(BF16), wukv_ref[...])
    scale = (MLA_NOPE + MLA_ROPE) ** -0.5 * LOG2_E
    for hh in range(heads):
        hs = slice(hh * LANES, (hh + 1) * LANES)
        q_ref[0, hh] = (rope(qn[:, hs]) * scale).astype(BF16)
        k_ref[0, hh] = (kvn[:, hs] + k_rope).astype(BF16)
    tk = v_ref.shape[-1]
    ones_row = jnp.where(lax.broadcasted_iota(jnp.int32, (MLA_VROWS - MLA_V, tk), 0) == 0, 1.0, 0.0)
    for hp in range(heads // 2):
        vt = kvn[:, kw + hp * LANES:kw + (hp + 1) * LANES].T
        for e in range(2):
            for t in range(tm // tk):
                v_ref[0, 2 * hp + e, t] = jnp.concatenate(
                    [vt[e * MLA_V:(e + 1) * MLA_V, t * tk:(t + 1) * tk], ones_row], axis=0).astype(BF16)


def _mla_attn_kernel(q_ref, k_ref, vt_ref, g_ref, h_ref, wout_ref, fg_ref, o_ref, m_sc, acc_sc, obuf,
                     *, tq, final):
    heads = MLA_HEADS
    j = pl.program_id(1)
    key_idx = lax.broadcasted_iota(jnp.int32, (tq, tq), 0)
    qry_idx = lax.broadcasted_iota(jnp.int32, (tq, tq), 1)
    diag = key_idx <= qry_idx

    def kv_tiles(tiles, qrows):
        units = [(kt, masked, hh) for kt, masked in tiles for hh in range(heads)]

        def scores(unit):
            kt, _, hh = unit
            k0 = pl.multiple_of(kt * tq, tq)
            return _dot_nt(k_ref[0, hh, pl.ds(k0, tq), :], q_ref[0, hh, qrows, :])

        ready = [scores(u) for u in units[:MLA_LOOKAHEAD]]
        for n, (kt, masked, hh) in enumerate(units):
            if n % MLA_LOOKAHEAD == 0:
                ready.extend(scores(u) for u in units[n + MLA_LOOKAHEAD:n + 2 * MLA_LOOKAHEAD])
            st = ready.pop(0)
            if masked:
                st = jnp.where(diag, st, -jnp.inf)
            m_old = m_sc[hh]
            m_new = jnp.maximum(m_old, jnp.max(st, axis=0, keepdims=True))
            pt = jnp.exp2(st - m_new).astype(BF16)
            acc_sc[hh] = jnp.exp2(m_old - m_new) * acc_sc[hh] + _dot(vt_ref[0, hh, kt], pt)
            m_sc[hh] = m_new

    for sub in range(MLA_Q_TILES):
        qrows = slice(sub * tq, (sub + 1) * tq)
        jq = MLA_Q_TILES * j + sub
        m_sc[...] = jnp.full(m_sc.shape, -jnp.inf, F32)
        acc_sc[...] = jnp.zeros(acc_sc.shape, F32)

        def full_pair(p, carry, qrows=qrows):
            kv_tiles([(2 * p, False), (2 * p + 1, False)], qrows)
            return carry

        lax.fori_loop(0, j, full_pair, 0)
        kv_tiles([(jq, True)] if sub == 0 else [(jq - 1, False), (jq, True)], qrows)

        for hp in range(heads // 2):
            halves = []
            for e in range(2):
                acc = acc_sc[2 * hp + e]
                halves.append(acc[:MLA_V] * (1.0 / acc[MLA_V:MLA_V + 1]))
            obuf[qrows, hp * LANES:(hp + 1) * LANES] = jnp.concatenate(halves, axis=0).T
    y = (obuf[...] * g_ref[...].astype(F32)).astype(BF16)
    o_ref[...] = _residual(h_ref[...], _dot(y, wout_ref[...]), fg_ref, final)


def _mla_layer(h, positions, ng, fg, w_in, g_q, w_uq, g_kv, w_ukv, w_out, *, final, tq=256):
    B, S, D = h.shape
    heads, half = MLA_HEADS, MLA_ROPE // 2
    qk = MLA_NOPE + MLA_ROPE
    width = heads * MLA_V
    c0 = MLA_Q_RANK + MLA_KV_RANK
    w_in = w_in.astype(BF16)
    w_kr = jnp.pad(w_in[:, c0:c0 + MLA_ROPE], ((0, 0), (MLA_NOPE, LANES - qk)))
    win = jnp.concatenate([w_in[:, :c0], w_in[:, c0 + MLA_ROPE:], w_kr], axis=1)
    wuq = jnp.pad(w_uq.reshape(MLA_Q_RANK, heads, qk), ((0, 0), (0, 0), (0, LANES - qk)))
    wuq = wuq.reshape(MLA_Q_RANK, heads * LANES).astype(BF16)
    wkv = w_ukv.reshape(MLA_KV_RANK, heads, MLA_NOPE + MLA_V)
    wk = jnp.pad(wkv[:, :, :MLA_NOPE], ((0, 0), (0, 0), (0, LANES - MLA_NOPE)))
    wukv = jnp.concatenate([wk.reshape(MLA_KV_RANK, heads * LANES),
                            wkv[:, :, MLA_NOPE:].reshape(MLA_KV_RANK, width)], axis=1).astype(BF16)
    inv_freq = ROPE_THETA ** (-jnp.arange(0, MLA_ROPE, 2, dtype=F32) / MLA_ROPE)
    zeros = lambda n: jnp.zeros((n,), F32)
    invf = jnp.concatenate([zeros(MLA_NOPE), inv_freq, inv_freq, zeros(LANES - qk)]).reshape(1, LANES)
    sgn = jnp.concatenate([jnp.ones((MLA_NOPE,), F32), -jnp.ones((half,), F32),
                           jnp.ones((LANES - MLA_NOPE - half,), F32)]).reshape(1, LANES)
    row = lambda v: v.reshape(1, -1).astype(F32)

    tm = tq * MLA_PROJ_TILES
    proj_spec = pl.BlockSpec((1, heads, tm, LANES), lambda b, j: (b, 0, j, 0))
    head_spec = pl.BlockSpec((1, heads, tq, LANES), lambda b, j: (b, 0, j, 0))
    vt_spec = pl.BlockSpec((1, heads, MLA_PROJ_TILES, MLA_VROWS, tq), lambda b, j: (b, 0, j, 0, 0))
    q, k, vt, g = pl.pallas_call(
        functools.partial(_mla_proj_kernel, tm=tm),
        grid=(B, S // tm),
        in_specs=[_seq_spec(tm, D), _seq_spec(tm, 1), _const_spec((1, D)), _const_spec(win.shape),
                  _const_spec((1, MLA_Q_RANK)), _const_spec(wuq.shape), _const_spec((1, MLA_KV_RANK)),
                  _const_spec(wukv.shape), _const_spec((1, LANES)), _const_spec((1, LANES))],
        out_specs=[proj_spec, proj_spec, vt_spec, _seq_spec(tm, width)],
        out_shape=[jax.ShapeDtypeStruct((B, heads, S, LANES), BF16),
                   jax.ShapeDtypeStruct((B, heads, S, LANES), BF16),
                   jax.ShapeDtypeStruct((B, heads, S // tq, MLA_VROWS, tq), BF16),
                   jax.ShapeDtypeStruct((B, S, width), BF16)],
        compiler_params=pltpu.CompilerParams(dimension_semantics=("arbitrary", "arbitrary"),
                                             vmem_limit_bytes=VMEM_LIMIT),
        name="mla_proj",
    )(h, positions.reshape(B, S, 1), row(ng), win, row(g_q), wuq, row(g_kv), wukv, invf, sgn)

    tg = tq * MLA_Q_TILES
    return pl.pallas_call(
        functools.partial(_mla_attn_kernel, tq=tq, final=final),
        grid=(B, S // tg),
        in_specs=[pl.BlockSpec((1, heads, tg, LANES), lambda b, j: (b, 0, j, 0)),
                  pl.BlockSpec((1, heads, S, LANES), lambda b, j: (b, 0, 0, 0)),
                  pl.BlockSpec((1, heads, S // tq, MLA_VROWS, tq), lambda b, j: (b, 0, 0, 0, 0)),
                  _seq_spec(tg, width), _seq_spec(tg, D), _const_spec((width, D)), _const_spec((1, D))],
        out_specs=_seq_spec(tg, D),
        out_shape=jax.ShapeDtypeStruct((B, S, D), F32),
        scratch_shapes=[pltpu.VMEM((heads, 1, tq), F32), pltpu.VMEM((heads, MLA_VROWS, tq), F32),
                        pltpu.VMEM((tg, width), F32)],
        compiler_params=pltpu.CompilerParams(dimension_semantics=("arbitrary", "arbitrary"),
                                             vmem_limit_bytes=VMEM_LIMIT),
        name="mla_attn",
    )(q, k, vt, g, h, w_out.astype(BF16), row(fg))


def kernel(x, positions, norm_g, final_g, mla_w_in, mla_g_q, mla_w_uq, mla_g_kv, mla_w_ukv, mla_w_out, gla_w_in, gla_w_gk2, gla_b_gk, gla_g_o, gla_w_out, lru_w_in, lru_conv_w, lru_conv_b, lru_w_a, lru_b_a, lru_w_x, lru_b_x, lru_lam, lru_w_out, ssd_w_in, ssd_conv_w, ssd_conv_b, ssd_dt_bias, ssd_a_log, ssd_d, ssd_g_norm, ssd_w_out):
    depth = norm_g.shape[0]
    h = x
    for i in range(depth):
        m, j = i % N_MIXERS, i // N_MIXERS
        common = dict(final=(i == depth - 1))
        if m == 0:
            h = _mla_layer(h, positions, norm_g[i], final_g, mla_w_in[j], mla_g_q[j], mla_w_uq[j],
                           mla_g_kv[j], mla_w_ukv[j], mla_w_out[j], **common)
        elif m == 1:
            h = _gla_layer(h, norm_g[i], final_g, gla_w_in[j], gla_w_gk2[j], gla_b_gk[j], gla_g_o[j],
                           gla_w_out[j], **common)
        elif m == 2:
            h = _lru_layer(h, norm_g[i], final_g, lru_w_in[j], lru_conv_w[j], lru_conv_b[j], lru_w_a[j],
                           lru_b_a[j], lru_w_x[j], lru_b_x[j], lru_lam[j], lru_w_out[j], **common)
        else:
            h = _ssd_layer(h, norm_g[i], final_g, ssd_w_in[j], ssd_conv_w[j], ssd_conv_b[j],
                           ssd_dt_bias[j], ssd_a_log[j], ssd_d[j], ssd_g_norm[j], ssd_w_out[j], **common)
    return h
```

```python
import functools

import jax
import jax.numpy as jnp
from jax import lax
from jax.experimental import pallas as pl
from jax.experimental.pallas import tpu as pltpu

F32 = jnp.float32
BF16 = jnp.bfloat16

NORM_EPS = 1e-6
N_MIXERS = 4
MLA_HEADS = 16
MLA_Q_RANK = 384
MLA_KV_RANK = 256
MLA_NOPE = 64
MLA_ROPE = 32
MLA_V = 64
MLA_VROWS = 80
MLA_PROJ_TILES = 4
MLA_Q_TILES = 2
MLA_LOOKAHEAD = 4
ROPE_THETA = 10000.0
GLA_HEADS = 4
GLA_GATE_RANK = 16
GLA_TAU = 16.0
GLA_CHUNK = 64
LRU_BLOCKS = 10
LRU_C = 8.0
CONV_W = 4
SSD_HEADDIM = 64
SSD_GROUPS = 8
SSD_STATE = 128
SSD_CHUNK = 64
SSD_CONV_BLOCK = 1024

LOG2_E = 1.4426950408889634
LANES = 128
SUBLANES = 8
VMEM_LIMIT = 56 * 1024 * 1024


def _dot(a, b):
    return jnp.dot(a, b, preferred_element_type=F32)


def _dot_nt(a, b):
    return lax.dot_general(a, b, (((1,), (1,)), ((), ())), preferred_element_type=F32)


def _dot_tn(a, b):
    return lax.dot_general(a, b, (((0,), (0,)), ((), ())), preferred_element_type=F32)


def _rms(x, g):
    ms = jnp.mean(x * x, axis=-1, keepdims=True)
    return x * lax.rsqrt(ms + NORM_EPS) * g


def _softplus(x):
    return jnp.maximum(x, 0.0) + jnp.log1p(jnp.exp(-jnp.abs(x)))


def _sigmoid(x):
    return 0.5 * jnp.tanh(0.5 * x) + 0.5


def _silu(x):
    hx = 0.5 * x
    return hx * jnp.tanh(hx) + hx


def _residual(x, y, fg_ref, final):
    r = x + y
    if final:
        r = _rms(r, fg_ref[...])
    return r


def _chunk_tri(n, chunk):
    r = lax.broadcasted_iota(jnp.int32, (n, n), 0)
    c = lax.broadcasted_iota(jnp.int32, (n, n), 1)
    keep = jnp.logical_and(r // chunk == c // chunk, c <= r)
    return jnp.where(keep, 1.0, 0.0).astype(BF16)


def _split_bf16(x, parts):
    out = []
    for _ in range(parts - 1):
        hi = x.astype(BF16)
        out.append(hi)
        x = x - hi.astype(F32)
    out.append(x.astype(BF16))
    return out


def _const_spec(shape):
    n = len(shape)
    return pl.BlockSpec(shape, lambda *_: (0,) * n, pipeline_mode=pl.Buffered(1))


def _seq_spec(tm, d):
    return pl.BlockSpec((None, tm, d), lambda b, j: (b, j, 0))


def _lru_kernel(h_ref, ng_ref, win_ref, cw_ref, cb_ref, wax_ref, ba_ref, bx_ref, lam_ref, wout_ref,
                fg_ref, o_ref, ubuf, abuf, bbuf, gbuf, hst, *, tm, nb, width, final):
    hist = (CONV_W - 1) * nb
    tt = tm // nb

    @pl.when(pl.program_id(0) == 0)
    def _():
        ubuf[0:hist, :] = jnp.zeros((hist, width), F32)
        hst[...] = jnp.zeros_like(hst)

    x = pltpu.einshape("btd->tbd", h_ref[...]).reshape(tm, h_ref.shape[-1])
    un = _rms(x, ng_ref[...]).astype(BF16)
    ubuf[hist:hist + tm, :] = _dot(un, win_ref[:, width:])
    gbuf[...] = _silu(_dot(un, win_ref[:, :width]))
    cw = cw_ref[...]
    conv = cb_ref[...] + cw[CONV_W - 1:CONV_W] * ubuf[hist:hist + tm, :]
    for k in range(CONV_W - 1):
        d = (CONV_W - 1 - k) * nb
        conv = conv + cw[k:k + 1] * ubuf[hist - d:hist - d + tm, :]
    ubuf[0:hist, :] = ubuf[tm:tm + hist, :]

    sp = _softplus(-lam_ref[...])
    for n in range(width // LANES):
        sl = slice(n * LANES, (n + 1) * LANES)
        cn = conv[:, sl]
        ra = _dot(cn.astype(BF16), wax_ref[n])
        r = _sigmoid(ra[:, :LANES] + ba_ref[:, sl])
        i = _sigmoid(ra[:, LANES:] + bx_ref[:, sl])
        log_a = -LRU_C * r * sp[:, sl]
        th = jnp.tanh(log_a)
        u = -2.0 * th
        mult = jnp.where(u > 0.0, u * lax.rsqrt(u * (1.0 - th)), 0.0)
        abuf[:, sl] = jnp.exp(log_a)
        bbuf[:, sl] = mult * (i * cn)

    hs = hst[...]
    for t in range(tm // nb):
        rs = slice(t * nb, (t + 1) * nb)
        hs = abuf[rs, :] * hs + bbuf[rs, :]
        bbuf[rs, :] = hs
    hst[...] = hs
    y = (bbuf[...] * gbuf[...]).astype(BF16)
    out = _residual(x, _dot(y, wout_ref[...]), fg_ref, final)
    o_ref[...] = pltpu.einshape("tbd->btd", out.reshape(tt, nb, out.shape[-1]))


def _lru_layer(h, ng, fg, w_in, conv_w, conv_b, w_a, b_a, w_x, b_x, lam, w_out, *, final, tm=1024):
    B, S, D = h.shape
    assert B == SUBLANES, "time-major RG-LRU tiling needs one sublane tile per time step"
    width = w_out.shape[0]
    wax = jnp.concatenate([w_a, w_x], axis=-1).astype(BF16)
    row = lambda v: v.reshape(1, -1).astype(F32)
    kern = functools.partial(_lru_kernel, tm=tm, nb=B, width=width, final=final)
    hist = (CONV_W - 1) * B
    tile = pl.BlockSpec((B, tm // B, D), lambda j: (0, j, 0))
    return pl.pallas_call(
        kern,
        grid=(S * B // tm,),
        in_specs=[tile, _const_spec((1, D)), _const_spec((D, 2 * width)),
                  _const_spec((CONV_W, width)), _const_spec((1, width)), _const_spec(wax.shape),
                  _const_spec((1, width)), _const_spec((1, width)), _const_spec((1, width)),
                  _const_spec((width, D)), _const_spec((1, D))],
        out_specs=tile,
        out_shape=jax.ShapeDtypeStruct((B, S, D), F32),
        scratch_shapes=[pltpu.VMEM((tm + hist, width), F32), pltpu.VMEM((tm, width), F32),
                        pltpu.VMEM((tm, width), F32), pltpu.VMEM((tm, width), F32),
                        pltpu.VMEM((B, width), F32)],
        compiler_params=pltpu.CompilerParams(dimension_semantics=("arbitrary",),
                                             vmem_limit_bytes=VMEM_LIMIT),
        name="lru_layer",
    )(h, row(ng), w_in.astype(BF16), conv_w.astype(F32), row(conv_b), wax, row(b_a), row(b_x),
      row(lam), w_out.astype(BF16), row(fg))


def _gla_kernel(h_ref, ng_ref, wq_ref, wgk_ref, wgk2_ref, bgk_ref, go_ref, wout_ref, fg_ref, o_ref,
                sst, obuf, gbuf, *, tm, key, val, final):
    heads, chunk = GLA_HEADS, GLA_CHUNK
    dk, dv = key // heads, val // heads

    @pl.when(pl.program_id(1) == 0)
    def _():
        sst[...] = jnp.zeros_like(sst)

    x = h_ref[...]
    un = _rms(x, ng_ref[...]).astype(BF16)
    q = _dot(un, wq_ref[:, :key]) * dk ** -0.5
    k = _dot(un, wq_ref[:, key:2 * key])
    gk = _dot(un, wgk_ref[...]).astype(BF16)
    log_a = -_softplus(-(_dot(gk, wgk2_ref[...]) + bgk_ref[...])) * (1.0 / GLA_TAU)
    v = _dot(un, wq_ref[:, 2 * key:2 * key + val]).astype(BF16)
    gbuf[...] = _silu(_dot(un, wq_ref[:, 2 * key + val:]))
    tri = _chunk_tri(tm, chunk)
    b = sum(_dot(tri, part) for part in _split_bf16(log_a, 2))

    rr = lax.broadcasted_iota(jnp.int32, (chunk, chunk), 0)
    cc = lax.broadcasted_iota(jnp.int32, (chunk, chunk), 1)
    causal = cc <= rr
    nch = tm // chunk
    ksl = [slice(hh * dk, (hh + 1) * dk) for hh in range(heads)]
    vsl = [slice(hh * dv, (hh + 1) * dv) for hh in range(heads)]
    rows = [slice(ci * chunk, (ci + 1) * chunk) for ci in range(nch)]
    qt, dec, att, ds = [], [], [], []
    for rs in rows:
        bc = b[rs]
        bl = bc[chunk - 1:chunk, :]
        qt.append((q[rs] * jnp.exp(bc)).astype(BF16))
        kt = (k[rs] * jnp.exp(-bc)).astype(BF16)
        ke = (k[rs] * jnp.exp(bl - bc)).astype(BF16)
        dec.append(jnp.exp(bl))
        att.append([_dot_nt(qt[-1][:, ks], kt[:, ks]) for ks in ksl])
        ds.append([_dot_tn(v[rs, vsl[hh]], ke[:, ksl[hh]]) for hh in range(heads)])
    for hh in range(heads):
        s_in = sst[hh]
        for ci, rs in enumerate(rows):
            p = jnp.where(causal, att[ci][hh], 0.0).astype(BF16)
            obuf[rs, vsl[hh]] = _dot(p, v[rs, vsl[hh]]) + _dot_nt(qt[ci][:, ksl[hh]], s_in.astype(BF16))
            s_in = dec[ci][:, ksl[hh]] * s_in + ds[ci][hh]
        sst[hh] = s_in

    parts = []
    for hh in range(heads):
        oh = obuf[:, hh * dv:(hh + 1) * dv]
        parts.append(oh * lax.rsqrt(jnp.mean(oh * oh, axis=-1, keepdims=True) + NORM_EPS))
    on = jnp.concatenate(parts, axis=1) * go_ref[...]
    y = (on * gbuf[...]).astype(BF16)
    o_ref[...] = _residual(x, _dot(y, wout_ref[...]), fg_ref, final)


def _gla_layer(h, ng, fg, w_in, w_gk2, b_gk, g_o, w_out, *, final, tm=512):
    B, S, D = h.shape
    val = w_out.shape[0]
    key = (w_in.shape[1] - 2 * val - GLA_GATE_RANK) // 2
    main = 2 * key + 2 * val
    pad = LANES - GLA_GATE_RANK
    w_in = w_in.astype(BF16)
    wq = w_in[:, :main]
    wgk = jnp.pad(w_in[:, main:], ((0, 0), (0, pad)))
    wgk2 = jnp.pad(w_gk2.astype(BF16), ((0, pad), (0, 0)))
    row = lambda v: v.reshape(1, -1).astype(F32)
    kern = functools.partial(_gla_kernel, tm=tm, key=key, val=val, final=final)
    return pl.pallas_call(
        kern,
        grid=(B, S // tm),
        in_specs=[_seq_spec(tm, D), _const_spec((1, D)), _const_spec((D, main)), _const_spec((D, LANES)),
                  _const_spec((LANES, key)), _const_spec((1, key)), _const_spec((1, val)),
                  _const_spec((val, D)), _const_spec((1, D))],
        out_specs=_seq_spec(tm, D),
        out_shape=jax.ShapeDtypeStruct((B, S, D), F32),
        scratch_shapes=[pltpu.VMEM((GLA_HEADS, val // GLA_HEADS, key // GLA_HEADS), F32),
                        pltpu.VMEM((tm, val), F32), pltpu.VMEM((tm, val), F32)],
        compiler_params=pltpu.CompilerParams(dimension_semantics=("arbitrary", "arbitrary"),
                                             vmem_limit_bytes=VMEM_LIMIT),
        name="gla_layer",
    )(h, row(ng), wq, wgk, wgk2, row(b_gk), row(jnp.tile(g_o, GLA_HEADS)), w_out.astype(BF16), row(fg))


def _ssd_kernel(h_ref, ng_ref, win_ref, cw_ref, cb_ref, dtb_ref, alog_ref, dexp_ref, gn_ref, e_ref,
                wout_ref, fg_ref, o_ref, xbuf, cbuf, csx, dtx, cst, dtt, zbuf, ybuf, sst, *, tm, inner, final):
    groups, ns, chunk = SSD_GROUPS, SSD_STATE, SSD_CHUNK
    gw = inner // groups
    conv_dim = inner + 2 * groups * ns
    half = SSD_HEADDIM

    @pl.when(pl.program_id(1) == 0)
    def _():
        xbuf[:, 0:SUBLANES, :] = jnp.zeros((conv_dim // LANES, SUBLANES, LANES), F32)
        sst[...] = jnp.zeros_like(sst)

    x = h_ref[...]
    un = _rms(x, ng_ref[...]).astype(BF16)

    dt_raw = _dot(un, win_ref[:, inner + conv_dim:])
    dt = _softplus(dt_raw + dtb_ref[...])
    da = dt * (-jnp.exp(alog_ref[...]))
    tri = _chunk_tri(tm, chunk)
    cs = sum(_dot(tri, part) for part in _split_bf16(da, 3))
    e = e_ref[...]
    csx[...] = sum(_dot(part, e) for part in _split_bf16(cs, 2))
    dtx[...] = _dot(dt.astype(BF16), e)
    cst[...] = cs.T
    dtt[...] = dt.T

    for c0 in range(0, conv_dim, SSD_CONV_BLOCK):
        proj = _dot(un, win_ref[:, inner + c0:inner + c0 + SSD_CONV_BLOCK])
        for c in range(SSD_CONV_BLOCK // LANES):
            xbuf[c0 // LANES + c, SUBLANES:SUBLANES + tm, :] = proj[:, c * LANES:(c + 1) * LANES]
        for c in range(SSD_CONV_BLOCK // LANES):
            t = c0 // LANES + c
            cl = slice(c0 + c * LANES, c0 + (c + 1) * LANES)
            conv = cb_ref[:, cl] + cw_ref[CONV_W - 1:CONV_W, cl] * xbuf[t, SUBLANES:SUBLANES + tm, :]
            for k in range(CONV_W - 1):
                d = CONV_W - 1 - k
                conv = conv + cw_ref[k:k + 1, cl] * xbuf[t, pl.ds(SUBLANES - d, tm, stride=1), :]
            xbuf[t, 0:SUBLANES, :] = xbuf[t, tm:tm + SUBLANES, :]
            cbuf[:, cl] = _silu(conv)
    zbuf[...] = _silu(_dot(un, win_ref[:, 0:inner]))

    ii = lax.broadcasted_iota(jnp.int32, (chunk, LANES), 0)
    ll = lax.broadcasted_iota(jnp.int32, (chunk, LANES), 1)
    causal2 = ll % half <= ii
    low = ll < half
    low_row = lax.broadcasted_iota(jnp.int32, (1, LANES), 1) < half
    dexp = dexp_ref[...]

    def lane_pair(ref, h0, ci):
        t0 = (ci * chunk // LANES) * LANES
        a = ref[h0:h0 + 1, t0:t0 + LANES]
        b = ref[h0 + 1:h0 + 2, t0:t0 + LANES]
        if (ci * chunk) % LANES == 0:
            return jnp.where(low_row, a, pltpu.roll(b, half, 1))
        return jnp.where(low_row, pltpu.roll(a, half, 1), b)

    for ci in range(tm // chunk):
        rs = slice(ci * chunk, (ci + 1) * chunk)
        cs_last = csx[(ci + 1) * chunk - 1:(ci + 1) * chunk, :]
        bgs, cgs, cb2s, y_offs, s_prevs = [], [], [], [], []
        for g in range(groups):
            bg = cbuf[rs, inner + g * ns:inner + (g + 1) * ns].astype(BF16)
            cg = cbuf[rs, inner + (groups + g) * ns:inner + (groups + g + 1) * ns].astype(BF16)
            s_prev = sst[g]
            bgs.append(bg)
            cgs.append(cg)
            s_prevs.append(s_prev)
            cb2s.append(_dot_nt(cg, jnp.concatenate([bg, bg], axis=0)))
            y_offs.append(_dot(cg, s_prev.astype(BF16)))
        for g in range(groups):
            gl = slice(g * gw, (g + 1) * gw)
            xg = cbuf[rs, gl]
            csg = csx[rs, gl]
            yd = []
            for pp in range(gw // LANES):
                ps = slice(pp * LANES, (pp + 1) * LANES)
                h0 = (g * gw + pp * LANES) // SSD_HEADDIM
                row = lane_pair(cst, h0, ci)
                dtrow = lane_pair(dtt, h0, ci)
                lm = jnp.where(causal2, jnp.exp(csg[:, ps] - row), 0.0)
                w = (cb2s[g] * lm * dtrow).astype(BF16)
                xp = xg[:, ps]
                bd = jnp.concatenate([jnp.where(low, xp, 0.0), jnp.where(low, 0.0, xp)],
                                     axis=0).astype(BF16)
                yd.append(_dot(w, bd))
            ybuf[rs, gl] = jnp.concatenate(yd, axis=1) + y_offs[g] * jnp.exp(csg) + dexp[:, gl] * xg
        for g in range(groups):
            gl = slice(g * gw, (g + 1) * gw)
            cl = cs_last[:, gl]
            w1 = jnp.exp(cl - csx[rs, gl]) * dtx[rs, gl]
            sst[g] = s_prevs[g] * jnp.exp(cl) + _dot_tn(bgs[g], (cbuf[rs, gl] * w1).astype(BF16))

    acc = x
    for g in range(groups):
        gl = slice(g * gw, (g + 1) * gw)
        yg = ybuf[:, gl] * zbuf[:, gl]
        yn = yg * lax.rsqrt(jnp.mean(yg * yg, axis=-1, keepdims=True) + NORM_EPS) * gn_ref[:, gl]
        acc = acc + _dot(yn.astype(BF16), wout_ref[gl, :])
    if final:
        acc = _rms(acc, fg_ref[...])
    o_ref[...] = acc


def _ssd_layer(h, ng, fg, w_in, conv_w, conv_b, dt_bias, a_log, d_skip, g_norm, w_out, *, final, tm=256):
    B, S, D = h.shape
    inner = w_out.shape[0]
    heads = inner // SSD_HEADDIM
    conv_dim = inner + 2 * SSD_GROUPS * SSD_STATE
    pad = LANES - heads
    win = jnp.pad(w_in.astype(BF16), ((0, 0), (0, pad)))
    row = lambda v: v.reshape(1, -1).astype(F32)
    padrow = lambda v: jnp.pad(v.astype(F32), (0, pad)).reshape(1, LANES)
    expand = (lax.broadcasted_iota(jnp.int32, (LANES, inner), 1) // SSD_HEADDIM
              == lax.broadcasted_iota(jnp.int32, (LANES, inner), 0)).astype(BF16)
    kern = functools.partial(_ssd_kernel, tm=tm, inner=inner, final=final)
    gw = inner // SSD_GROUPS
    return pl.pallas_call(
        kern,
        grid=(B, S // tm),
        in_specs=[_seq_spec(tm, D), _const_spec((1, D)), _const_spec(win.shape),
                  _const_spec((CONV_W, conv_dim)), _const_spec((1, conv_dim)), _const_spec((1, LANES)),
                  _const_spec((1, LANES)), _const_spec((1, inner)), _const_spec((1, inner)),
                  _const_spec((LANES, inner)), _const_spec((inner, D)), _const_spec((1, D))],
        out_specs=_seq_spec(tm, D),
        out_shape=jax.ShapeDtypeStruct((B, S, D), F32),
        scratch_shapes=[pltpu.VMEM((conv_dim // LANES, tm + SUBLANES, LANES), F32), pltpu.VMEM((tm, conv_dim), F32),
                        pltpu.VMEM((tm, inner), F32), pltpu.VMEM((tm, inner), F32),
                        pltpu.VMEM((LANES, tm), F32), pltpu.VMEM((LANES, tm), F32),
                        pltpu.VMEM((tm, inner), F32), pltpu.VMEM((tm, inner), F32),
                        pltpu.VMEM((SSD_GROUPS, SSD_STATE, gw), F32)],
        compiler_params=pltpu.CompilerParams(dimension_semantics=("arbitrary", "arbitrary"),
                                             vmem_limit_bytes=VMEM_LIMIT),
        name="ssd_layer",
    )(h, row(ng), win, conv_w.astype(F32), row(conv_b), padrow(dt_bias), padrow(a_log),
      row(jnp.repeat(d_skip, SSD_HEADDIM)), row(g_norm), expand, w_out.astype(BF16), row(fg))


def _mla_proj_kernel(h_ref, pos_ref, ng_ref, win_ref, gq_ref, wuq_ref, gkv_ref, wukv_ref, invf_ref,
                     sgn_ref, q_ref, k_ref, v_ref, g_ref, *, tm):
    heads = MLA_HEADS
    kw = heads * LANES
    x = h_ref[...]
    un = _rms(x, ng_ref[...]).astype(BF16)
    t = _dot(un, win_ref[...])
    c_q = t[:, :MLA_Q_RANK]
    c_kv = t[:, MLA_Q_RANK:MLA_Q_RANK + MLA_KV_RANK]
    g0 = MLA_Q_RANK + MLA_KV_RANK
    gate = t[:, g0:g0 + heads * MLA_V]
    k_r = t[:, g0 + heads * MLA_V:]
    g_ref[...] = _silu(gate).astype(BF16)

    ang = pos_ref[...].astype(F32) * invf_ref[...]
    cos = jnp.cos(ang)
    sin = jnp.sin(ang) * sgn_ref[...]
    lane = lax.broadcasted_iota(jnp.int32, (tm, LANES), 1)
    first_half = lane < MLA_NOPE + MLA_ROPE // 2

    def rope(a):
        swapped = jnp.where(first_half, pltpu.roll(a, LANES - MLA_ROPE // 2, 1),
                            pltpu.roll(a, MLA_ROPE // 2, 1))
        return a * cos + swapped * sin

    k_rope = rope(k_r)
    qn = _dot(_rms(c_q, gq_ref[...]).astype(BF16), wuq_ref[...])
    kvn = _dot(_rms(c_kv, gkv_ref[...]).astype(BF16), wukv_ref[...])
    scale = (MLA_NOPE + MLA_ROPE) ** -0.5 * LOG2_E
    for hh in range(heads):
        hs = slice(hh * LANES, (hh + 1) * LANES)
        q_ref[0, hh] = (rope(qn[:, hs]) * scale).astype(BF16)
        k_ref[0, hh] = (kvn[:, hs] + k_rope).astype(BF16)
    tk = v_ref.shape[-1]
    ones_row = jnp.where(lax.broadcasted_iota(jnp.int32, (MLA_VROWS - MLA_V, tk), 0) == 0, 1.0, 0.0)
    for hp in range(heads // 2):
        vt = kvn[:, kw + hp * LANES:kw + (hp + 1) * LANES].T
        for e in range(2):
            for t in range(tm // tk):
                v_ref[0, 2 * hp + e, t] = jnp.concatenate(
                    [vt[e * MLA_V:(e + 1) * MLA_V, t * tk:(t + 1) * tk], ones_row], axis=0).astype(BF16)


def _mla_attn_kernel(q_ref, k_ref, vt_ref, g_ref, h_ref, wout_ref, fg_ref, o_ref, m_sc, acc_sc, obuf,
                     *, tq, final):
    heads = MLA_HEADS
    j = pl.program_id(1)
    key_idx = lax.broadcasted_iota(jnp.int32, (tq, tq), 0)
    qry_idx = lax.broadcasted_iota(jnp.int32, (tq, tq), 1)
    diag = key_idx <= qry_idx

    def kv_tiles(tiles, qrows):
        units = [(kt, masked, hh) for kt, masked in tiles for hh in range(heads)]

        def scores(unit):
            kt, _, hh = unit
            k0 = pl.multiple_of(kt * tq, tq)
            return _dot_nt(k_ref[0, hh, pl.ds(k0, tq), :], q_ref[0, hh, qrows, :])

        ready = [scores(u) for u in units[:MLA_LOOKAHEAD]]
        for n, (kt, masked, hh) in enumerate(units):
            if n % MLA_LOOKAHEAD == 0:
                ready.extend(scores(u) for u in units[n + MLA_LOOKAHEAD:n + 2 * MLA_LOOKAHEAD])
            st = ready.pop(0)
            if masked:
                st = jnp.where(diag, st, -jnp.inf)
            m_old = m_sc[hh]
            m_new = jnp.maximum(m_old, jnp.max(st, axis=0, keepdims=True))
            pt = jnp.exp2(st - m_new).astype(BF16)
            acc_sc[hh] = jnp.exp2(m_old - m_new) * acc_sc[hh] + _dot(vt_ref[0, hh, kt], pt)
            m_sc[hh] = m_new

    for sub in range(MLA_Q_TILES):
        qrows = slice(sub * tq, (sub + 1) * tq)
        jq = MLA_Q_TILES * j + sub
        m_sc[...] = jnp.full(m_sc.shape, -jnp.inf, F32)
        acc_sc[...] = jnp.zeros(acc_sc.shape, F32)

        def full_pair(p, carry, qrows=qrows):
            kv_tiles([(2 * p, False), (2 * p + 1, False)], qrows)
            return carry

        lax.fori_loop(0, j, full_pair, 0)
        kv_tiles([(jq, True)] if sub == 0 else [(jq - 1, False), (jq, True)], qrows)

        for hp in range(heads // 2):
            halves = []
            for e in range(2):
                acc = acc_sc[2 * hp + e]
                halves.append(acc[:MLA_V] * (1.0 / acc[MLA_V:MLA_V + 1]))
            obuf[qrows, hp * LANES:(hp + 1) * LANES] = jnp.concatenate(halves, axis=0).T
    y = (obuf[...] * g_ref[...].astype(F32)).astype(BF16)
    o_ref[...] = _residual(h_ref[...], _dot(y, wout_ref[...]), fg_ref, final)


def _mla_layer(h, positions, ng, fg, w_in, g_q, w_uq, g_kv, w_ukv, w_out, *, final, tq=256):
    B, S, D = h.shape
    heads, half = MLA_HEADS, MLA_ROPE // 2
    qk = MLA_NOPE + MLA_ROPE
    width = heads * MLA_V
    c0 = MLA_Q_RANK + MLA_KV_RANK
    w_in = w_in.astype(BF16)
    w_kr = jnp.pad(w_in[:, c0:c0 + MLA_ROPE], ((0, 0), (MLA_NOPE, LANES - qk)))
    win = jnp.concatenate([w_in[:, :c0], w_in[:, c0 + MLA_ROPE:], w_kr], axis=1)
    wuq = jnp.pad(w_uq.reshape(MLA_Q_RANK, heads, qk), ((0, 0), (0, 0), (0, LANES - qk)))
    wuq = wuq.reshape(MLA_Q_RANK, heads * LANES).astype(BF16)
    wkv = w_ukv.reshape(MLA_KV_RANK, heads, MLA_NOPE + MLA_V)
    wk = jnp.pad(wkv[:, :, :MLA_NOPE], ((0, 0), (0, 0), (0, LANES - MLA_NOPE)))
    wukv = jnp.concatenate([wk.reshape(MLA_KV_RANK, heads * LANES),
                            wkv[:, :, MLA_NOPE:].reshape(MLA_KV_RANK, width)], axis=1).astype(BF16)
    inv_freq = ROPE_THETA ** (-jnp.arange(0, MLA_ROPE, 2, dtype=F32) / MLA_ROPE)
    zeros = lambda n: jnp.zeros((n,), F32)
    invf = jnp.concatenate([zeros(MLA_NOPE), inv_freq, inv_freq, zeros(LANES - qk)]).reshape(1, LANES)
    sgn = jnp.concatenate([jnp.ones((MLA_NOPE,), F32), -jnp.ones((half,), F32),
                           jnp.ones((LANES - MLA_NOPE - half,), F32)]).reshape(1, LANES)
    row = lambda v: v.reshape(1, -1).astype(F32)

    tm = tq * MLA_PROJ_TILES
    proj_spec = pl.BlockSpec((1, heads, tm, LANES), lambda b, j: (b, 0, j, 0))
    head_spec = pl.BlockSpec((1, heads, tq, LANES), lambda b, j: (b, 0, j, 0))
    vt_spec = pl.BlockSpec((1, heads, MLA_PROJ_TILES, MLA_VROWS, tq), lambda b, j: (b, 0, j, 0, 0))
    q, k, vt, g = pl.pallas_call(
        functools.partial(_mla_proj_kernel, tm=tm),
        grid=(B, S // tm),
        in_specs=[_seq_spec(tm, D), _seq_spec(tm, 1), _const_spec((1, D)), _const_spec(win.shape),
                  _const_spec((1, MLA_Q_RANK)), _const_spec(wuq.shape), _const_spec((1, MLA_KV_RANK)),
                  _const_spec(wukv.shape), _const_spec((1, LANES)), _const_spec((1, LANES))],
        out_specs=[proj_spec, proj_spec, vt_spec, _seq_spec(tm, width)],
        out_shape=[jax.ShapeDtypeStruct((B, heads, S, LANES), BF16),
                   jax.ShapeDtypeStruct((B, heads, S, LANES), BF16),
                   jax.ShapeDtypeStruct((B, heads, S // tq, MLA_VROWS, tq), BF16),
                   jax.ShapeDtypeStruct((B, S, width), BF16)],
        compiler_params=pltpu.CompilerParams(dimension_semantics=("arbitrary", "arbitrary"),
                                             vmem_limit_bytes=VMEM_LIMIT),
        name="mla_proj",
    )(h, positions.reshape(B, S, 1), row(ng), win, row(g_q), wuq, row(g_kv), wukv, invf, sgn)

    tg = tq * MLA_Q_TILES
    return pl.pallas_call(
        functools.partial(_mla_attn_kernel, tq=tq, final=final),
        grid=(B, S // tg),
        in_specs=[pl.BlockSpec((1, heads, tg, LANES), lambda b, j: (b, 0, j, 0)),
                  pl.BlockSpec((1, heads, S, LANES), lambda b, j: (b, 0, 0, 0)),
                  pl.BlockSpec((1, heads, S // tq, MLA_VROWS, tq), lambda b, j: (b, 0, 0, 0, 0)),
                  _seq_spec(tg, width), _seq_spec(tg, D), _const_spec((width, D)), _const_spec((1, D))],
        out_specs=_seq_spec(tg, D),
        out_shape=jax.ShapeDtypeStruct((B, S, D), F32),
        scratch_shapes=[pltpu.VMEM((heads, 1, tq), F32), pltpu.VMEM((heads, MLA_VROWS, tq), F32),
                        pltpu.VMEM((tg, width), F32)],
        compiler_params=pltpu.CompilerParams(dimension_semantics=("arbitrary", "arbitrary"),
                                             vmem_limit_bytes=VMEM_LIMIT),
        name="mla_attn",
    )(q, k, vt, g, h, w_out.astype(BF16), row(fg))


def kernel(x, positions, norm_g, final_g, mla_w_in, mla_g_q, mla_w_uq, mla_g_kv, mla_w_ukv, mla_w_out, gla_w_in, gla_w_gk2, gla_b_gk, gla_g_o, gla_w_out, lru_w_in, lru_conv_w, lru_conv_b, lru_w_a, lru_b_a, lru_w_x, lru_b_x, lru_lam, lru_w_out, ssd_w_in, ssd_conv_w, ssd_conv_b, ssd_dt_bias, ssd_a_log, ssd_d, ssd_g_norm, ssd_w_out):
    depth = norm_g.shape[0]
    h = x
    for i in range(depth):
        m, j = i % N_MIXERS, i // N_MIXERS
        common = dict(final=(i == depth - 1))
        if m == 0:
            h = _mla_layer(h, positions, norm_g[i], final_g, mla_w_in[j], mla_g_q[j], mla_w_uq[j],
                           mla_g_kv[j], mla_w_ukv[j], mla_w_out[j], **common)
        elif m == 1:
            h = _gla_layer(h, norm_g[i], final_g, gla_w_in[j], gla_w_gk2[j], gla_b_gk[j], gla_g_o[j],
                           gla_w_out[j], **common)
        elif m == 2:
            h = _lru_layer(h, norm_g[i], final_g, lru_w_in[j], lru_conv_w[j], lru_conv_b[j], lru_w_a[j],
                           lru_b_a[j], lru_w_x[j], lru_b_x[j], lru_lam[j], lru_w_out[j], **common)
        else:
            h = _ssd_layer(h, norm_g[i], final_g, ssd_w_in[j], ssd_conv_w[j], ssd_conv_b[j],
                           ssd_dt_bias[j], ssd_a_log[j], ssd_d[j], ssd_g_norm[j], ssd_w_out[j], **common)
    return h
```
